```python
import jax, jax.numpy as jnp
from jax import lax
import numpy as np

D_MODEL = 2048
BATCH = 8
SEQ = 2048
DEPTH = 2

GRID_W = 64
CTX_LEN = 256
EPS = 1e-6
NEG_INF = -1e30
ROPE_THETA = 10000.0

A_WIDTH = 1024
A_CHUNK = 128
A_GROUPS = 8
A_GDIM = A_WIDTH // A_GROUPS

NA_HEADS = 8
NA_HDIM = 128
NA_WIDTH = NA_HEADS * NA_HDIM
NA_KH = 8
NA_KW = 16
NA_QB = 16
NA_KSPAN = 32
NA_NCB = GRID_W // NA_QB

M_HEADS = 4
M_DK = 128
M_DV = 256
M_QK_W = M_HEADS * M_DK
M_V_W = M_HEADS * M_DV
M_CHUNK = 128
M_NGATE = 4 * M_HEADS
M_F_BIAS = 3.0

N_BRANCH = 3
BRANCH_W = 1024

N_EXPERTS = 16
N_GROUPS = 4
EXP_PER_GROUP = N_EXPERTS // N_GROUPS
TOP_K = 2
D_FF_EXPERT = 1024

FIELDS = (('a_u', A_WIDTH), ('a_v', A_WIDTH), ('b_q', NA_WIDTH), ('b_k', NA_WIDTH), ('b_v', NA_WIDTH), ('c_q', M_QK_W), ('c_k', M_QK_W), ('c_v', M_V_W), ('c_o', M_V_W), ('c_g', M_NGATE), ('gate', N_BRANCH * D_MODEL))
ALL_FIELDS = ('a_u', 'a_v', 'b_q', 'b_k', 'b_v', 'c_q', 'c_k', 'c_v', 'c_o', 'c_g', 'gate')
CTX_FIELDS = ('b_k', 'b_v', 'c_k', 'c_v', 'c_g')
P_IN = 2 * A_WIDTH + 3 * NA_WIDTH + 2 * M_QK_W + 2 * M_V_W + M_NGATE + N_BRANCH * D_MODEL

kernel_name = 'hybrid_dit_natten_mlstm_sgu_grouped_moe'


def _rmsnorm(x, g):
    x32 = x.astype(jnp.float32)
    y = x32 * lax.rsqrt(jnp.mean(x32 * x32, axis=-1, keepdims=True) + EPS)
    return (y * g.astype(jnp.float32)).astype(x.dtype)


def _modulate(x, g, shift, scale):
    return _rmsnorm(x, g) * (1 + scale) + shift


def _project(h, w_in, names):
    offs, o = {}, 0
    for name, width in FIELDS:
        offs[name] = (o, width)
        o += width
    if len(names) == len(FIELDS):
        w = w_in
    else:
        w = jnp.concatenate([w_in[:, offs[n][0]:offs[n][0] + offs[n][1]] for n in names], axis=1)
    p = h @ w
    out, o = {}, 0
    for n in names:
        out[n] = p[..., o:o + offs[n][1]]
        o += offs[n][1]
    return out


def _heads(t, nh):
    return t.reshape(t.shape[:-1] + (nh, -1))


def _bhtd(t, nh):
    b, tl, _ = t.shape
    return t.reshape(b, tl, nh, -1).transpose(0, 2, 1, 3)


def _rope_1d(x, pos):
    d = x.shape[-1]
    half = d // 2
    inv = ROPE_THETA ** (-jnp.arange(half, dtype=jnp.float32) / half)
    ang = pos.astype(jnp.float32)[:, None] * inv[None, :]
    cos, sin = jnp.cos(ang), jnp.sin(ang)
    x32 = x.astype(jnp.float32)
    x1, x2 = x32[..., :half], x32[..., half:]
    return jnp.concatenate([x1 * cos - x2 * sin, x1 * sin + x2 * cos], axis=-1).astype(x.dtype)


def _rope_2d(x, pos_row, pos_col):
    half = x.shape[-1] // 2
    return jnp.concatenate([_rope_1d(x[..., :half], pos_row), _rope_1d(x[..., half:], pos_col)], axis=-1)


def _sgu(u_pre, v_pre, g_sgu, w_sp, b_sp):
    bsz, tl, _ = v_pre.shape
    u = jax.nn.gelu(u_pre)
    v = _rmsnorm(jax.nn.gelu(v_pre), g_sgu)
    vg = v.reshape(bsz, tl // A_CHUNK, A_CHUNK, A_GROUPS, A_GDIM)
    mixed = jnp.einsum('gij,bnjgc->bnigc', w_sp, vg) + b_sp.T[None, None, :, :, None]
    return u * mixed.reshape(bsz, tl, A_WIDTH)


def _na_col_tables():
    cq = np.arange(GRID_W)
    cs = np.clip(cq - NA_KW // 2, 0, GRID_W - NA_KW)
    ks = np.clip(np.arange(NA_NCB) * NA_QB - NA_KW // 2, 0, GRID_W - NA_KSPAN)
    key_cols = ks[:, None] + np.arange(NA_KSPAN)[None, :]
    kc = key_cols[cq // NA_QB]
    valid = (kc >= cs[:, None]) & (kc < cs[:, None] + NA_KW)
    off = np.clip(kc - cq[:, None], -(NA_KW - 1), NA_KW - 1) + NA_KW - 1
    return key_cols, valid.reshape(NA_NCB, NA_QB, NA_KSPAN), off.reshape(NA_NCB, NA_QB, NA_KSPAN)


def _neighbourhood_attention(q, k, v, k_ctx, v_ctx, rpb, rows):
    bsz = q.shape[0]
    kh = min(NA_KH, rows)
    key_cols, valid, col_off = _na_col_tables()
    scale = NA_HDIM ** -0.5
    qg = q.reshape(bsz, rows, NA_NCB, NA_QB, NA_HEADS, NA_HDIM)
    kg = k.reshape(bsz, rows, GRID_W, NA_HEADS, NA_HDIM)
    vg = v.reshape(bsz, rows, GRID_W, NA_HEADS, NA_HDIM)
    rpb_cols = rpb[:, :, col_off].astype(jnp.float32)
    n_win = kh * NA_KSPAN

    def row_block(r):
        rs = jnp.clip(r - kh // 2, 0, rows - kh)
        q_r = lax.dynamic_index_in_dim(qg, r, axis=1, keepdims=False)
        k_r = lax.dynamic_slice_in_dim(kg, rs, kh, axis=1)[:, :, key_cols]
        v_r = lax.dynamic_slice_in_dim(vg, rs, kh, axis=1)[:, :, key_cols]
        s_win = jnp.einsum('bjqhd,bajkhd->bhjqak', q_r, k_r).astype(jnp.float32) * scale
        row_idx = rs + jnp.arange(kh) - r + NA_KH - 1
        bias = jnp.take(rpb_cols, row_idx, axis=1).transpose(0, 2, 3, 1, 4)
        s_win = jnp.where(valid[:, :, None, :], s_win + bias[None], NEG_INF)
        s_ctx = jnp.einsum('bjqhd,bshd->bhjqs', q_r, k_ctx).astype(jnp.float32) * scale
        s = jnp.concatenate([s_win.reshape(s_win.shape[:4] + (n_win,)), s_ctx], axis=-1)
        p = jax.nn.softmax(s, axis=-1).astype(v.dtype)
        p_win = p[..., :n_win].reshape(s_win.shape)
        o = jnp.einsum('bhjqak,bajkhd->bjqhd', p_win, v_r) + jnp.einsum('bhjqs,bshd->bjqhd', p[..., n_win:], v_ctx)
        return o.reshape(bsz, GRID_W, NA_WIDTH)

    out = lax.map(row_block, jnp.arange(rows))
    return jnp.moveaxis(out, 0, 1).reshape(bsz, rows * GRID_W, NA_WIDTH)


def _ctx_attention(q, k, v):
    bsz, s_len = q.shape[:2]
    s = jnp.einsum('bqhd,bkhd->bhqk', q, k).astype(jnp.float32) * (NA_HDIM ** -0.5)
    p = jax.nn.softmax(s, axis=-1).astype(v.dtype)
    return jnp.einsum('bhqk,bkhd->bqhd', p, v).reshape(bsz, s_len, NA_WIDTH)


def _zero_state(bsz):
    return (jnp.zeros((bsz, M_HEADS, M_DK, M_DV), jnp.float32), jnp.zeros((bsz, M_HEADS, M_DK), jnp.float32), jnp.zeros((bsz, M_HEADS), jnp.float32))


def _flip(a):
    return jnp.flip(a, axis=2)


def _mlstm_chunkwise(q, k, v, i_pre, f_pre, state):
    bsz, nh, tl = k.shape[:3]
    nc = tl // M_CHUNK

    def chunks(a):
        a = a.astype(jnp.float32).reshape((bsz, nh, nc, M_CHUNK) + a.shape[3:])
        return jnp.moveaxis(a, 2, 0)

    log_f = jax.nn.log_sigmoid(f_pre.astype(jnp.float32))
    with_out = q is not None
    xs = (chunks(k), chunks(v), chunks(i_pre), chunks(log_f))
    if with_out:
        xs = xs + (chunks(q),)
    order = jnp.tril(jnp.ones((M_CHUNK, M_CHUNK), dtype=bool))

    def step(carry, xs_c):
        c_st, n_st, m_st = carry
        kc, vc, ic, lfc = xs_c[0], xs_c[1], xs_c[2], xs_c[3]
        b = jnp.cumsum(lfc, axis=-1)
        b_last = b[..., -1]
        w_log = b_last[..., None] - b + ic
        m_new = jnp.maximum(b_last + m_st, jnp.max(w_log, axis=-1))
        decay = jnp.exp(b_last + m_st - m_new)
        kw = kc * jnp.exp(w_log - m_new[..., None])[..., None]
        c_new = decay[..., None, None] * c_st + jnp.einsum('bhsk,bhsv->bhkv', kw, vc)
        n_new = decay[..., None] * n_st + kw.sum(axis=2)
        if not with_out:
            return (c_new, n_new, m_new), None
        qc = xs_c[4]
        d_log = jnp.where(order, b[..., :, None] - b[..., None, :] + ic[..., None, :], -jnp.inf)
        inter = b + m_st[..., None]
        m_row = jnp.maximum(inter, jnp.max(d_log, axis=-1))
        s = jnp.einsum('bhtk,bhsk->bhts', qc, kc) * jnp.exp(d_log - m_row[..., None])
        ew = jnp.exp(inter - m_row)
        num = jnp.einsum('bhts,bhsv->bhtv', s, vc) + ew[..., None] * jnp.einsum('bhtk,bhkv->bhtv', qc, c_st)
        den = s.sum(axis=-1) + ew * jnp.einsum('bhtk,bhk->bht', qc, n_st)
        h = num / jnp.maximum(jnp.abs(den), jnp.exp(-m_row))[..., None]
        return (c_new, n_new, m_new), h

    carry, h = lax.scan(step, state, xs)
    if not with_out:
        return carry
    return jnp.moveaxis(h, 0, 2).reshape(bsz, nh, tl, M_DV), carry


def _mlstm_gates(g_pre, b_mgate):
    bsz, tl, _ = g_pre.shape
    g = g_pre.reshape(bsz, tl, 4, M_HEADS).astype(jnp.float32) + b_mgate.astype(jnp.float32)
    return jnp.transpose(g, (2, 0, 3, 1))


def _mlstm_out(h, o_pre, g_mnorm):
    bsz, nh, tl, dv = h.shape
    y = _rmsnorm(h, g_mnorm.reshape(nh, 1, dv))
    y = y.transpose(0, 2, 1, 3).reshape(bsz, tl, nh * dv).astype(o_pre.dtype)
    return y * jax.nn.sigmoid(o_pre)


def _merge(gate_pre, branches, w_branch, w_out):
    g = jax.nn.sigmoid(gate_pre.reshape(gate_pre.shape[:-1] + (N_BRANCH, D_MODEL)))
    merged = g[..., 0, :] * (branches[0] @ w_branch[0])
    for i in range(1, N_BRANCH):
        merged = merged + g[..., i, :] * (branches[i] @ w_branch[i])
    return merged @ w_out


def _moe(h, w_router, b_router, w_eg, w_eu, w_ed):
    n_tok = h.shape[0]
    scores = jax.nn.sigmoid((h @ w_router).astype(jnp.float32))
    sel = (scores + b_router.astype(jnp.float32)).reshape(n_tok, N_GROUPS, EXP_PER_GROUP)
    group_score = lax.top_k(sel, TOP_K)[0].sum(axis=-1)
    g_idx = jnp.argmax(group_score, axis=-1)
    sel_in = jnp.take_along_axis(sel, g_idx[:, None, None], axis=1)[:, 0]
    _, e_local = lax.top_k(sel_in, TOP_K)
    e_idx = g_idx[:, None] * EXP_PER_GROUP + e_local
    w_sel = jnp.take_along_axis(scores, e_idx, axis=1)
    w_sel = w_sel / jnp.sum(w_sel, axis=-1, keepdims=True)
    combine = jnp.sum(jax.nn.one_hot(e_idx, N_EXPERTS, dtype=jnp.float32) * w_sel[..., None], axis=1).astype(h.dtype)
    out = jnp.zeros_like(h)
    for e in range(N_EXPERTS):
        he = jax.nn.silu(h @ w_eg[e]) * (h @ w_eu[e])
        out = out + combine[:, e:e + 1] * (he @ w_ed[e])
    return out


def _layer(xl, xc, c, c_ctx, w_ada, b_ada, g1, g2, w_in, g_sgu, w_sp, b_sp, rpb, b_mgate, g_mnorm, w_branch, w_out, w_router, b_router, w_eg, w_eu, w_ed, last):
    bsz, seq, d = xl.shape
    ctx_len = xc.shape[1]
    rows = seq // GRID_W
    t = jnp.arange(seq)
    pos_row, pos_col = t // GRID_W, t % GRID_W

    mod = jax.nn.silu(c) @ w_ada + b_ada
    mod_c = jax.nn.silu(c_ctx) @ w_ada + b_ada
    sh1, sc1, gt1, sh2, sc2, gt2 = [m[:, None, :] for m in jnp.split(mod, 6, axis=-1)]
    csh1, csc1, cgt1, csh2, csc2, cgt2 = jnp.split(mod_c, 6, axis=-1)

    pl = _project(_modulate(xl, g1, sh1, sc1), w_in, ALL_FIELDS)
    pc = _project(_modulate(xc, g1, csh1, csc1), w_in, CTX_FIELDS if last else ALL_FIELDS)

    ya = _sgu(pl['a_u'], pl['a_v'], g_sgu, w_sp, b_sp)

    k_bc, v_bc = _heads(pc['b_k'], NA_HEADS), _heads(pc['b_v'], NA_HEADS)
    yb = _neighbourhood_attention(_heads(pl['b_q'], NA_HEADS), _heads(pl['b_k'], NA_HEADS), _heads(pl['b_v'], NA_HEADS), k_bc, v_bc, rpb, rows)

    k_cc, v_cc = _bhtd(pc['c_k'], M_HEADS), _bhtd(pc['c_v'], M_HEADS)
    g_cc = _mlstm_gates(pc['c_g'], b_mgate)
    state0 = _zero_state(bsz)
    if last:
        st_f = _mlstm_chunkwise(None, k_cc, v_cc, g_cc[0], g_cc[1], state0)
        st_b = _mlstm_chunkwise(None, _flip(k_cc), _flip(v_cc), _flip(g_cc[2]), _flip(g_cc[3]), state0)
    else:
        q_cc = _bhtd(pc['c_q'], M_HEADS) * (M_DK ** -0.5)
        hc_f, st_f = _mlstm_chunkwise(q_cc, k_cc, v_cc, g_cc[0], g_cc[1], state0)
        hc_b, st_b = _mlstm_chunkwise(_flip(q_cc), _flip(k_cc), _flip(v_cc), _flip(g_cc[2]), _flip(g_cc[3]), state0)
        yc_ctx = _mlstm_out(hc_f + _flip(hc_b), pc['c_o'], g_mnorm)
    q_m = _rope_2d(_bhtd(pl['c_q'], M_HEADS), pos_row, pos_col) * (M_DK ** -0.5)
    k_m = _rope_2d(_bhtd(pl['c_k'], M_HEADS), pos_row, pos_col)
    v_m = _bhtd(pl['c_v'], M_HEADS)
    g_m = _mlstm_gates(pl['c_g'], b_mgate)
    h_f, _ = _mlstm_chunkwise(q_m, k_m, v_m, g_m[0], g_m[1], st_f)
    h_b, _ = _mlstm_chunkwise(_flip(q_m), _flip(k_m), _flip(v_m), _flip(g_m[2]), _flip(g_m[3]), st_b)
    yc = _mlstm_out(h_f + _flip(h_b), pl['c_o'], g_mnorm)

    xl = xl + gt1 * _merge(pl['gate'], (ya, yb, yc), w_branch, w_out)
    h2l = _modulate(xl, g2, sh2, sc2)
    if last:
        f = _moe(h2l.reshape(-1, d), w_router, b_router, w_eg, w_eu, w_ed)
        return xl + gt2 * f.reshape(bsz, seq, d), None

    ya_ctx = _sgu(pc['a_u'], pc['a_v'], g_sgu, w_sp, b_sp)
    yb_ctx = _ctx_attention(_heads(pc['b_q'], NA_HEADS), k_bc, v_bc)
    xc = xc + cgt1 * _merge(pc['gate'], (ya_ctx, yb_ctx, yc_ctx), w_branch, w_out)
    h2c = _modulate(xc, g2, csh2, csc2)
    f = _moe(jnp.concatenate([h2l.reshape(-1, d), h2c.reshape(-1, d)], axis=0), w_router, b_router, w_eg, w_eu, w_ed)
    n_lat = bsz * seq
    xl = xl + gt2 * f[:n_lat].reshape(bsz, seq, d)
    xc = xc + cgt2 * f[n_lat:].reshape(bsz, ctx_len, d)
    return xl, xc


def setup_inputs(seed: int = 0) -> dict:
    key = jax.random.key(seed)
    ks = jax.random.split(key, 24)
    f32 = jnp.float32

    def nrm(k, shape, scale):
        return jax.random.normal(k, shape, f32) * scale

    def gain(k, shape):
        return 1.0 + 0.02 * jax.random.normal(k, shape, f32)

    gate_offset = jnp.array([0.0, M_F_BIAS, 0.0, M_F_BIAS], f32)[:, None]
    return {
        'x': nrm(ks[0], (BATCH, SEQ, D_MODEL), 1.0),
        'c': nrm(ks[1], (BATCH, D_MODEL), 1.0),
        'ctx': nrm(ks[2], (BATCH, CTX_LEN, D_MODEL), 1.0),
        'c_ctx': nrm(ks[3], (D_MODEL,), 1.0),
        'w_ada': nrm(ks[4], (DEPTH, D_MODEL, 6 * D_MODEL), 0.5 * D_MODEL ** -0.5),
        'b_ada': nrm(ks[5], (DEPTH, 6 * D_MODEL), 0.01),
        'g_norm1': gain(ks[6], (DEPTH, D_MODEL)),
        'g_norm2': gain(ks[7], (DEPTH, D_MODEL)),
        'w_in': nrm(ks[8], (DEPTH, D_MODEL, P_IN), D_MODEL ** -0.5),
        'g_sgu': gain(ks[9], (DEPTH, A_WIDTH)),
        'w_spatial': nrm(ks[10], (DEPTH, A_GROUPS, A_CHUNK, A_CHUNK), A_CHUNK ** -0.5),
        'b_spatial': gain(ks[11], (DEPTH, A_GROUPS, A_CHUNK)),
        'na_rpb': nrm(ks[12], (DEPTH, NA_HEADS, 2 * NA_KH - 1, 2 * NA_KW - 1), 0.1),
        'b_mgate': gate_offset + nrm(ks[13], (DEPTH, 4, M_HEADS), 0.1),
        'g_mnorm': gain(ks[14], (DEPTH, M_V_W)),
        'w_branch': nrm(ks[15], (DEPTH, N_BRANCH, BRANCH_W, D_MODEL), BRANCH_W ** -0.5),
        'w_out': nrm(ks[16], (DEPTH, D_MODEL, D_MODEL), D_MODEL ** -0.5),
        'w_router': nrm(ks[17], (D_MODEL, N_EXPERTS), D_MODEL ** -0.5),
        'b_router': nrm(ks[18], (N_EXPERTS,), 0.01),
        'w_e_gate': nrm(ks[19], (DEPTH, N_EXPERTS, D_MODEL, D_FF_EXPERT), D_MODEL ** -0.5),
        'w_e_up': nrm(ks[20], (DEPTH, N_EXPERTS, D_MODEL, D_FF_EXPERT), D_MODEL ** -0.5),
        'w_e_down': nrm(ks[21], (DEPTH, N_EXPERTS, D_FF_EXPERT, D_MODEL), D_FF_EXPERT ** -0.5),
        'g_final': gain(ks[22], (D_MODEL,)),
    }


def reference(x, c, ctx, c_ctx, w_ada, b_ada, g_norm1, g_norm2, w_in, g_sgu, w_spatial, b_spatial, na_rpb, b_mgate, g_mnorm, w_branch, w_out, w_router, b_router, w_e_gate, w_e_up, w_e_down, g_final):
    xl, xc = x, ctx
    for layer in range(DEPTH):
        xl, xc = _layer(xl, xc, c, c_ctx, w_ada[layer], b_ada[layer], g_norm1[layer], g_norm2[layer], w_in[layer], g_sgu[layer], w_spatial[layer], b_spatial[layer], na_rpb[layer], b_mgate[layer], g_mnorm[layer], w_branch[layer], w_out[layer], w_router, b_router, w_e_gate[layer], w_e_up[layer], w_e_down[layer], last=(layer == DEPTH - 1))
    return _rmsnorm(xl, g_final)
```

```python
import functools

import numpy as np
import jax
import jax.numpy as jnp
from jax import lax
from jax.experimental import pallas as pl
from jax.experimental.pallas import tpu as pltpu

F32 = jnp.float32
BF16 = jnp.bfloat16

GRID_W = 64
EPS = 1e-6
NEG_INF = -1e30
ROPE_THETA = 10000.0

A_WIDTH = 1024
A_CHUNK = 128
A_GROUPS = 8
A_GDIM = A_WIDTH // A_GROUPS

NA_HEADS = 8
NA_HDIM = 128
NA_WIDTH = NA_HEADS * NA_HDIM
NA_KH = 8
NA_KW = 16

M_HEADS = 4
M_DK = 128
M_DV = 256
M_QK_W = M_HEADS * M_DK
M_V_W = M_HEADS * M_DV
M_CHUNK = 128
M_NGATE = 4 * M_HEADS

N_BRANCH = 3
BRANCH_W = 1024

N_EXPERTS = 16
N_GROUPS = 4
EXP_PER_GROUP = N_EXPERTS // N_GROUPS
D_FF_EXPERT = 1024

OFF_AU = 0
OFF_AV = OFF_AU + A_WIDTH
OFF_BQ = OFF_AV + A_WIDTH
OFF_BK = OFF_BQ + NA_WIDTH
OFF_BV = OFF_BK + NA_WIDTH
OFF_CQ = OFF_BV + NA_WIDTH
OFF_CK = OFF_CQ + M_QK_W
OFF_CV = OFF_CK + M_QK_W
OFF_CO = OFF_CV + M_V_W
OFF_CG = OFF_CO + M_V_W
OFF_GATE = OFF_CO + M_V_W

LANE = 128
VMEM_LIMIT = 56 * 1024 * 1024

MOE_TM = 256
ROW_TM = 256


def _cparams(sem):
    return pltpu.CompilerParams(dimension_semantics=sem, vmem_limit_bytes=VMEM_LIMIT)


def _dot(a, b):
    return jnp.dot(a, b, preferred_element_type=F32)


def _dot_nt(a, b):
    return lax.dot_general(a, b, (((1,), (1,)), ((), ())), preferred_element_type=F32)


def _sigmoid(x):
    return 1.0 / (1.0 + jnp.exp(-x))


def _gelu_tanh(x):
    return 0.5 * x * (1.0 + jnp.tanh(np.sqrt(2.0 / np.pi).astype(np.float32) * (x + 0.044715 * (x * x * x))))


def _rms(x, g):
    return x * lax.rsqrt(jnp.mean(x * x, axis=-1, keepdims=True) + EPS) * g


def _ada_kernel(c_ref, w_ref, b_ref, o_ref):
    cc = c_ref[...]
    s = cc * _sigmoid(cc)
    o_ref[...] = _dot(s.astype(BF16), w_ref[...].astype(BF16)) + b_ref[...]


def _ada(cc, w, b):
    rows, d = cc.shape
    n = w.shape[1]
    tn = 1024
    return pl.pallas_call(
        _ada_kernel,
        grid=(n // tn,),
        in_specs=[pl.BlockSpec((rows, d), lambda j: (0, 0)),
                  pl.BlockSpec((d, tn), lambda j: (0, j)),
                  pl.BlockSpec((1, tn), lambda j: (0, j))],
        out_specs=pl.BlockSpec((rows, tn), lambda j: (0, j)),
        out_shape=jax.ShapeDtypeStruct((rows, n), F32),
        compiler_params=_cparams(("arbitrary",)),
        name="ada_mod",
    )(cc, w, b)


def _inproj_kernel(x_ref, sh_ref, sc_ref, g_ref, w_ref, o_ref, h_ref):
    @pl.when(pl.program_id(1) == 0)
    def _():
        h = _rms(x_ref[...], g_ref[...]) * (1.0 + sc_ref[0]) + sh_ref[0]
        h_ref[...] = h.astype(BF16)

    o_ref[...] = _dot(h_ref[...], w_ref[...]).astype(o_ref.dtype)


def _group_of(i, tm, nl, t, b):
    return jnp.where(i * tm < nl, (i * tm) // t, b)


def _inproj(x, mod3, g, w, dims, tm, tn, out_dtype):
    nl, t, b = dims
    nt, d = x.shape
    n = w.shape[1]
    grp = functools.partial(_group_of, tm=tm, nl=nl, t=t, b=b)
    return pl.pallas_call(
        _inproj_kernel,
        grid=(nt // tm, n // tn),
        in_specs=[pl.BlockSpec((tm, d), lambda i, j: (i, 0)),
                  pl.BlockSpec((1, 1, d), lambda i, j: (grp(i) * 6 + 0, 0, 0)),
                  pl.BlockSpec((1, 1, d), lambda i, j: (grp(i) * 6 + 1, 0, 0)),
                  pl.BlockSpec((1, d), lambda i, j: (0, 0)),
                  pl.BlockSpec((d, tn), lambda i, j: (0, j))],
        out_specs=pl.BlockSpec((tm, tn), lambda i, j: (i, j)),
        out_shape=jax.ShapeDtypeStruct((nt, n), out_dtype),
        scratch_shapes=[pltpu.VMEM((tm, d), BF16)],
        compiler_params=_cparams(("parallel", "arbitrary")),
        name="inproj",
    )(x, mod3, mod3, g, w)


def _sgu_kernel(u_ref, v_ref, g_ref, w_ref, b_ref, o_ref):
    n_chunk = u_ref.shape[0] // A_CHUNK
    for c in range(n_chunk):
        rows = slice(c * A_CHUNK, (c + 1) * A_CHUNK)
        u = _gelu_tanh(u_ref[rows, :].astype(F32))
        v = _rms(_gelu_tanh(v_ref[rows, :].astype(F32)), g_ref[...]).astype(BF16)
        for gi in range(A_GROUPS):
            cols = slice(gi * A_GDIM, (gi + 1) * A_GDIM)
            mixed = _dot(w_ref[gi], v[:, cols]) + b_ref[:, cols]
            o_ref[rows, cols] = (u[:, cols] * mixed).astype(o_ref.dtype)


def _sgu(p, g_sgu, w_sp, b_full, n_rows, tm):
    return pl.pallas_call(
        _sgu_kernel,
        grid=(n_rows // tm,),
        in_specs=[pl.BlockSpec((tm, A_WIDTH), lambda i: (i, OFF_AU // A_WIDTH)),
                  pl.BlockSpec((tm, A_WIDTH), lambda i: (i, OFF_AV // A_WIDTH)),
                  pl.BlockSpec((1, A_WIDTH), lambda i: (0, 0)),
                  pl.BlockSpec((A_GROUPS, A_CHUNK, A_CHUNK), lambda i: (0, 0, 0)),
                  pl.BlockSpec((A_CHUNK, A_WIDTH), lambda i: (0, 0))],
        out_specs=pl.BlockSpec((tm, A_WIDTH), lambda i: (i, 0)),
        out_shape=jax.ShapeDtypeStruct((n_rows, A_WIDTH), BF16),
        compiler_params=_cparams(("parallel",)),
        name="sgu",
    )(p, p, g_sgu, w_sp, b_full)


def _na_bias_table(rpb, kh):
    cq = np.arange(GRID_W)[:, None]
    kc = np.arange(GRID_W)[None, :]
    cs = np.clip(cq - NA_KW // 2, 0, GRID_W - NA_KW)
    valid = (kc >= cs) & (kc < cs + NA_KW)
    coff = np.clip(kc - cq, -(NA_KW - 1), NA_KW - 1) + NA_KW - 1
    ridx = np.arange(kh)[None, :] - np.arange(kh)[:, None] + NA_KH - 1
    tab = rpb[:, ridx[:, :, None, None], coff[None, None, :, :]].astype(F32)
    tab = jnp.where(valid[None, None, None], tab, NEG_INF)
    return tab.transpose(0, 1, 3, 2, 4).reshape(rpb.shape[0], kh, GRID_W, kh * GRID_W)


def _natten_kernel(q_ref, k_ref, v_ref, qc_ref, kc_ref, vc_ref, tab_ref, o_ref, oc_ref, *, rows, kh):
    scale = NA_HDIM ** -0.5
    kc = kc_ref[...]
    vc = vc_ref[...]

    def row_block(r, carry):
        rs = jnp.clip(r - kh // 2, 0, rows - kh)
        q0 = pl.multiple_of(r * GRID_W, GRID_W)
        k0 = pl.multiple_of(rs * GRID_W, GRID_W)
        q = q_ref[pl.ds(q0, GRID_W), :]
        kw = k_ref[pl.ds(k0, kh * GRID_W), :]
        vw = v_ref[pl.ds(k0, kh * GRID_W), :]
        s_w = _dot_nt(q, kw) * scale + tab_ref[0, r - rs]
        s_c = _dot_nt(q, kc) * scale
        m = jnp.maximum(jnp.max(s_w, axis=-1, keepdims=True), jnp.max(s_c, axis=-1, keepdims=True))
        p_w = jnp.exp(s_w - m)
        p_c = jnp.exp(s_c - m)
        l = jnp.sum(p_w, axis=-1, keepdims=True) + jnp.sum(p_c, axis=-1, keepdims=True)
        o = (_dot(p_w.astype(BF16), vw) + _dot(p_c.astype(BF16), vc)) / l
        o_ref[pl.ds(q0, GRID_W), :] = o.astype(o_ref.dtype)
        return carry

    lax.fori_loop(0, rows, row_block, 0)

    s = _dot_nt(qc_ref[...], kc) * scale
    m = jnp.max(s, axis=-1, keepdims=True)
    p = jnp.exp(s - m)
    l = jnp.sum(p, axis=-1, keepdims=True)
    oc_ref[...] = (_dot(p.astype(BF16), vc) / l).astype(oc_ref.dtype)


def _natten(p, tab, dims, ctx_len):
    nl, t, b = dims
    rows = t // GRID_W
    kh = min(NA_KH, rows)
    cb = nl // ctx_len
    hd = NA_HDIM
    lat = lambda off: pl.BlockSpec((t, hd), lambda bi, h: (bi, off // hd + h))
    ctx = lambda off: pl.BlockSpec((ctx_len, hd), lambda bi, h: (cb + bi, off // hd + h))
    return pl.pallas_call(
        functools.partial(_natten_kernel, rows=rows, kh=kh),
        grid=(b, NA_HEADS),
        in_specs=[lat(OFF_BQ), lat(OFF_BK), lat(OFF_BV), ctx(OFF_BQ), ctx(OFF_BK), ctx(OFF_BV),
                  pl.BlockSpec((1, kh, GRID_W, kh * GRID_W), lambda bi, h: (h, 0, 0, 0))],
        out_specs=[pl.BlockSpec((t, hd), lambda bi, h: (bi, h)),
                   pl.BlockSpec((ctx_len, hd), lambda bi, h: (bi, h))],
        out_shape=[jax.ShapeDtypeStruct((nl, NA_WIDTH), BF16),
                   jax.ShapeDtypeStruct((b * ctx_len, NA_WIDTH), BF16)],
        compiler_params=_cparams(("parallel", "arbitrary")),
        name="natten",
    )(p, p, p, p, p, p, tab)


def _split3(x):
    hi = x.astype(BF16)
    r = x - hi.astype(F32)
    mid = r.astype(BF16)
    lo = (r - mid.astype(F32)).astype(BF16)
    return hi, mid, lo


def _tri_left(tri, x):
    hi, mid, lo = _split3(x)
    return (_dot(tri, lo) + _dot(tri, mid)) + _dot(tri, hi)


def _tri_right(x, tri):
    hi, mid, lo = _split3(x)
    return (_dot(lo, tri) + _dot(mid, tri)) + _dot(hi, tri)


def _log_sigmoid(x):
    return jnp.minimum(x, 0.0) - jnp.log(1.0 + jnp.exp(-jnp.abs(x)))


def _rope_swap(x):
    lane = lax.broadcasted_iota(jnp.int32, x.shape, 1)
    return jnp.where(lane % 64 < 32, pltpu.roll(x, LANE - 32, 1), pltpu.roll(x, 32, 1))


def _mlstm_scan(n_chunks, reverse, lane_i, lane_f, q_ref, k_ref, v_ref, g_ref, bm_ref, rope, state, emit):
    c_ref, n_ref, m_ref = state
    ln = M_CHUNK
    row = lax.broadcasted_iota(jnp.int32, (ln, ln), 0)
    col = lax.broadcasted_iota(jnp.int32, (ln, ln), 1)
    lower = (row >= col)
    upper = (col >= row)
    tri_l = jnp.where(lower, 1.0, 0.0).astype(BF16)
    tri_u = jnp.where(upper, 1.0, 0.0).astype(BF16)
    order = upper if reverse else lower

    def body(ci, carry):
        c = (n_chunks - 1 - ci) if reverse else ci
        r0 = pl.multiple_of(c * ln, ln)
        gc = g_ref[pl.ds(r0, ln), :] + bm_ref[...]
        gct = gc.T
        lf = _log_sigmoid(gc)
        lft = _log_sigmoid(gct)
        if reverse:
            b_cols = _tri_left(tri_u, lf)
            b_rows = _tri_right(lft, tri_l)
        else:
            b_cols = _tri_left(tri_l, lf)
            b_rows = _tri_right(lft, tri_u)
        i_col = gc[:, lane_i:lane_i + 1]
        b_col = b_cols[:, lane_f:lane_f + 1]
        i_row = gct[lane_i:lane_i + 1, :]
        b_row = b_rows[lane_f:lane_f + 1, :]

        q = q_ref[pl.ds(r0, ln), :].astype(F32)
        k = k_ref[pl.ds(r0, ln), :].astype(F32)
        if rope is not None:
            cos = rope[0][pl.ds(r0, ln), :]
            sin = rope[1][pl.ds(r0, ln), :]
            q = q * cos + _rope_swap(q) * sin
            k = k * cos + _rope_swap(k) * sin
        q = (q * (M_DK ** -0.5)).astype(BF16)
        v = v_ref[pl.ds(r0, ln), :]

        m_prev = m_ref[...]
        c_prev = c_ref[...]
        n_prev = n_ref[...]

        d_log = jnp.where(order, b_col - b_row + i_row, -jnp.inf)
        inter = b_col + m_prev
        m_row = jnp.maximum(inter, jnp.max(d_log, axis=-1, keepdims=True))
        s = _dot_nt(q, k.astype(BF16)) * jnp.exp(d_log - m_row)
        ew = jnp.exp(inter - m_row)
        num = _dot(s.astype(BF16), v) + ew * _dot(q, c_prev.astype(BF16))
        den = jnp.sum(s, axis=-1, keepdims=True) + ew * jnp.sum(q.astype(F32) * n_prev, axis=-1, keepdims=True)
        h = num / jnp.maximum(jnp.abs(den), jnp.exp(-m_row))
        emit(r0, h)

        b_last = b_col[0:1, :] if reverse else b_col[ln - 1:ln, :]
        w_log = b_last - b_col + i_col
        m_new = jnp.maximum(b_last + m_prev, jnp.max(w_log, axis=0, keepdims=True))
        decay = jnp.exp(b_last + m_prev - m_new)
        kw = k * jnp.exp(w_log - m_new)
        c_ref[...] = decay * c_prev + _dot(kw.T.astype(BF16), v)
        n_ref[...] = decay * n_prev + jnp.sum(kw, axis=0, keepdims=True)
        m_ref[...] = m_new
        return carry

    lax.fori_loop(0, n_chunks, body, 0)


def _mlstm_kernel(ql_ref, kl_ref, vl_ref, ol_ref, gl_ref, qc_ref, kc_ref, vc_ref, oc_ref, gc_ref,
                  bm_ref, gn_ref, cos_ref, sin_ref, yl_ref, yc_ref, hl_ref, hc_ref, c_ref, n_ref, m_ref):
    state = (c_ref, n_ref, m_ref)
    n_lat = ql_ref.shape[0] // M_CHUNK
    n_ctx = qc_ref.shape[0] // M_CHUNK
    rope = (cos_ref, sin_ref)

    def reset():
        c_ref[...] = jnp.zeros_like(c_ref)
        n_ref[...] = jnp.zeros_like(n_ref)
        m_ref[...] = jnp.zeros_like(m_ref)

    def keep(h_ref):
        def emit(r0, h):
            h_ref[pl.ds(r0, M_CHUNK), :] = h
        return emit

    def finish(h_ref, o_ref, y_ref):
        def emit(r0, h):
            hs = h_ref[pl.ds(r0, M_CHUNK), :] + h
            y = _rms(hs, gn_ref[...])
            o = o_ref[pl.ds(r0, M_CHUNK), :].astype(F32)
            y_ref[pl.ds(r0, M_CHUNK), :] = (y * _sigmoid(o)).astype(y_ref.dtype)
        return emit

    reset()
    _mlstm_scan(n_ctx, False, 0, 1, qc_ref, kc_ref, vc_ref, gc_ref, bm_ref, None, state, keep(hc_ref))
    _mlstm_scan(n_lat, False, 0, 1, ql_ref, kl_ref, vl_ref, gl_ref, bm_ref, rope, state, keep(hl_ref))
    reset()
    _mlstm_scan(n_ctx, True, 2, 3, qc_ref, kc_ref, vc_ref, gc_ref, bm_ref, None, state, finish(hc_ref, oc_ref, yc_ref))
    _mlstm_scan(n_lat, True, 2, 3, ql_ref, kl_ref, vl_ref, gl_ref, bm_ref, rope, state, finish(hl_ref, ol_ref, yl_ref))


def _rope_tables(t):
    pos = np.arange(t)
    half = M_DK // 4
    inv = ROPE_THETA ** (-np.arange(half, dtype=np.float64) / half)
    ang_r = (pos // GRID_W)[:, None] * inv[None, :]
    ang_c = (pos % GRID_W)[:, None] * inv[None, :]
    ang = np.concatenate([ang_r, ang_r, ang_c, ang_c], axis=1)
    sign = np.tile(np.concatenate([-np.ones(half), np.ones(half)]), 2)[None, :]
    return jnp.asarray(np.cos(ang), F32), jnp.asarray(np.sin(ang) * sign, F32)


def _mlstm(p, g, bm, gn, dims, ctx_len):
    nl, t, b = dims
    cb = nl // ctx_len
    cos, sin = _rope_tables(t)
    dk, dv = M_DK, M_DV
    lat = lambda off, w: pl.BlockSpec((t, w), lambda bi, h: (bi, off // w + h))
    ctx = lambda off, w: pl.BlockSpec((ctx_len, w), lambda bi, h: (cb + bi, off // w + h))
    const = lambda shape: pl.BlockSpec(shape, lambda bi, h: (0, 0))
    return pl.pallas_call(
        _mlstm_kernel,
        grid=(b, M_HEADS),
        in_specs=[lat(OFF_CQ, dk), lat(OFF_CK, dk), lat(OFF_CV, dv), lat(OFF_CO, dv),
                  pl.BlockSpec((t, LANE), lambda bi, h: (bi, h)),
                  ctx(OFF_CQ, dk), ctx(OFF_CK, dk), ctx(OFF_CV, dv), ctx(OFF_CO, dv),
                  pl.BlockSpec((ctx_len, LANE), lambda bi, h: (cb + bi, h)),
                  pl.BlockSpec((1, LANE), lambda bi, h: (0, h)),
                  pl.BlockSpec((1, dv), lambda bi, h: (0, h)),
                  const((t, LANE)), const((t, LANE))],
        out_specs=[pl.BlockSpec((t, dv), lambda bi, h: (bi, h)),
                   pl.BlockSpec((ctx_len, dv), lambda bi, h: (bi, h))],
        out_shape=[jax.ShapeDtypeStruct((nl, M_V_W), BF16),
                   jax.ShapeDtypeStruct((b * ctx_len, M_V_W), BF16)],
        scratch_shapes=[pltpu.VMEM((t, dv), F32), pltpu.VMEM((ctx_len, dv), F32),
                        pltpu.VMEM((dk, dv), F32), pltpu.VMEM((1, dk), F32), pltpu.VMEM((1, 1), F32)],
        compiler_params=_cparams(("parallel", "arbitrary")),
        name="mlstm",
    )(p, p, p, p, g, p, p, p, p, g, bm, gn, cos, sin)


def _merge_kernel(ya_ref, yb_ref, yc_ref, g0_ref, g1_ref, g2_ref, w_ref, o_ref):
    acc = _sigmoid(g0_ref[...].astype(F32)) * _dot(ya_ref[...], w_ref[0])
    acc = acc + _sigmoid(g1_ref[...].astype(F32)) * _dot(yb_ref[...], w_ref[1])
    acc = acc + _sigmoid(g2_ref[...].astype(F32)) * _dot(yc_ref[...], w_ref[2])
    o_ref[...] = acc.astype(o_ref.dtype)


def _merge(ya, yb, yc, p, w_branch, n_rows, tm, tn):
    d = w_branch.shape[2]
    br = lambda: pl.BlockSpec((tm, BRANCH_W), lambda i, j: (i, 0))
    gate = lambda k: pl.BlockSpec((tm, tn), lambda i, j: (i, (OFF_GATE + k * d) // tn + j))
    return pl.pallas_call(
        _merge_kernel,
        grid=(n_rows // tm, d // tn),
        in_specs=[br(), br(), br(), gate(0), gate(1), gate(2),
                  pl.BlockSpec((N_BRANCH, BRANCH_W, tn), lambda i, j: (0, 0, j))],
        out_specs=pl.BlockSpec((tm, tn), lambda i, j: (i, j)),
        out_shape=jax.ShapeDtypeStruct((n_rows, d), BF16),
        compiler_params=_cparams(("parallel", "arbitrary")),
        name="merge",
    )(ya, yb, yc, p, p, p, w_branch)


def _route(scores, sel):
    row = lambda a, e: a[e:e + 1, :]
    best = None
    for gi in range(N_GROUPS):
        a, b, c, d = (row(sel, gi * EXP_PER_GROUP + j) for j in range(EXP_PER_GROUP))
        hi1, lo1 = jnp.maximum(a, b), jnp.minimum(a, b)
        hi2, lo2 = jnp.maximum(c, d), jnp.minimum(c, d)
        gs = jnp.maximum(hi1, hi2) + jnp.maximum(jnp.minimum(hi1, hi2), jnp.maximum(lo1, lo2))
        if best is None:
            best, g_idx = gs, jnp.zeros_like(gs, dtype=jnp.int32)
        else:
            take = gs > best
            best = jnp.where(take, gs, best)
            g_idx = jnp.where(take, gi, g_idx)

    def in_group(a, j):
        out = row(a, j)
        for gi in range(1, N_GROUPS):
            out = jnp.where(g_idx == gi, row(a, gi * EXP_PER_GROUP + j), out)
        return out

    v = [in_group(sel, j) for j in range(EXP_PER_GROUP)]
    u = [in_group(scores, j) for j in range(EXP_PER_GROUP)]

    def argmax_first(vals):
        bv, bi, bu = vals[0], jnp.zeros_like(g_idx), u[0]
        for j in range(1, EXP_PER_GROUP):
            take = vals[j] > bv
            bv = jnp.where(take, vals[j], bv)
            bi = jnp.where(take, j, bi)
            bu = jnp.where(take, u[j], bu)
        return bi, bu

    i1, s1 = argmax_first(v)
    i2, s2 = argmax_first([jnp.where(i1 == j, -jnp.inf, v[j]) for j in range(EXP_PER_GROUP)])
    tot = s1 + s2
    return g_idx * EXP_PER_GROUP + i1, g_idx * EXP_PER_GROUP + i2, s1 / tot, s2 / tot


def _outproj_kernel(mg_ref, x_ref, gt_ref, sh_ref, sc_ref, g_ref, w_ref, wr_ref, br_ref, xo_ref, h_ref, r_ref):
    y = _dot(mg_ref[...], w_ref[...])
    xn = x_ref[...] + gt_ref[0] * y
    xo_ref[...] = xn
    h = _rms(xn, g_ref[...]) * (1.0 + sc_ref[0]) + sh_ref[0]
    h_ref[...] = h
    scores = _sigmoid(_dot_nt(wr_ref[...], h.astype(BF16)))
    e1, e2, w1, w2 = _route(scores, scores + br_ref[...])
    r_ref[...] = jnp.zeros_like(r_ref)
    r_ref[0:1, :] = e1.astype(F32)
    r_ref[1:2, :] = e2.astype(F32)
    r_ref[2:3, :] = w1
    r_ref[3:4, :] = w2


def _outproj(merged, x, mod3, g2, w_out, w_rt, b_r, dims, n_rows, tm):
    nl, t, b = dims
    d = x.shape[1]
    grp = functools.partial(_group_of, tm=tm, nl=nl, t=t, b=b)
    mrow = lambda k: pl.BlockSpec((1, 1, d), lambda i: (grp(i) * 6 + k, 0, 0))
    return pl.pallas_call(
        _outproj_kernel,
        grid=(n_rows // tm,),
        in_specs=[pl.BlockSpec((tm, d), lambda i: (i, 0)),
                  pl.BlockSpec((tm, d), lambda i: (i, 0)),
                  mrow(2), mrow(3), mrow(4),
                  pl.BlockSpec((1, d), lambda i: (0, 0)),
                  pl.BlockSpec((d, d), lambda i: (0, 0)),
                  pl.BlockSpec((N_EXPERTS, d), lambda i: (0, 0)),
                  pl.BlockSpec((N_EXPERTS, 1), lambda i: (0, 0))],
        out_specs=[pl.BlockSpec((tm, d), lambda i: (i, 0)),
                   pl.BlockSpec((tm, d), lambda i: (i, 0)),
                   pl.BlockSpec((8, tm), lambda i: (0, i))],
        out_shape=[jax.ShapeDtypeStruct((n_rows, d), F32),
                   jax.ShapeDtypeStruct((n_rows, d), F32),
                   jax.ShapeDtypeStruct((8, n_rows), F32)],
        compiler_params=_cparams(("parallel",)),
        name="outproj_route",
    )(merged, x, mod3, mod3, mod3, g2, w_out, w_rt, b_r)


def _slot_plan(route, tm, max_tiles):
    n = route.shape[1]
    e = route[0:2, :].astype(jnp.int32).reshape(-1)
    oh = (e[:, None] == jnp.arange(N_EXPERTS, dtype=jnp.int32)[None, :]).astype(jnp.int32)
    cs = jnp.cumsum(oh, axis=0)
    rank = jnp.sum(cs * oh, axis=1) - 1
    counts = cs[-1]
    padded = ((counts + tm - 1) // tm) * tm
    ends = jnp.cumsum(padded)
    slot = jnp.sum(oh * (ends - padded)[None, :], axis=1) + rank
    tile_ends = ends // tm
    n_used = tile_ends[-1]
    tiles = jnp.arange(max_tiles, dtype=jnp.int32)
    src = jnp.minimum(tiles, n_used - 1)
    tile_e = jnp.minimum(jnp.sum((src[:, None] >= tile_ends[None, :]).astype(jnp.int32), axis=1), N_EXPERTS - 1)
    wts = route[2:4, :].T
    return slot.astype(jnp.int32), src.astype(jnp.int32), tile_e.astype(jnp.int32), n_used.reshape(1).astype(jnp.int32), wts


def _row_copy(src_ref, src_row, dst_ref, dst_row, sem):
    return pltpu.make_async_copy(src_ref.at[pl.ds(src_row, 1), :], dst_ref.at[pl.ds(dst_row, 1), :], sem)


def _dispatch_kernel(slot_ref, h_ref, init_ref, hs_ref, sem, *, n_tok):
    del init_ref
    tm = h_ref.shape[0]
    base = pl.program_id(0) * tm

    def issue(j, carry):
        _row_copy(h_ref, j, hs_ref, slot_ref[base + j], sem).start()
        _row_copy(h_ref, j, hs_ref, slot_ref[n_tok + base + j], sem).start()
        return carry

    def drain(j, carry):
        _row_copy(h_ref, 0, hs_ref, 0, sem).wait()
        return carry

    lax.fori_loop(0, tm, issue, 0)
    lax.fori_loop(0, 2 * tm, drain, 0)


def _dispatch(slot, h, n_slots, tm):
    n_tok, d = h.shape
    return pl.pallas_call(
        functools.partial(_dispatch_kernel, n_tok=n_tok),
        grid_spec=pltpu.PrefetchScalarGridSpec(
            num_scalar_prefetch=1,
            grid=(n_tok // tm,),
            in_specs=[pl.BlockSpec((tm, d), lambda i, s: (i, 0)),
                      pl.BlockSpec(memory_space=pl.ANY)],
            out_specs=pl.BlockSpec(memory_space=pl.ANY),
            scratch_shapes=[pltpu.SemaphoreType.DMA(())]),
        out_shape=jax.ShapeDtypeStruct((n_slots, d), F32),
        input_output_aliases={2: 0},
        compiler_params=_cparams(("arbitrary",)),
        name="moe_dispatch",
    )(slot, h, jnp.zeros((n_slots, d), F32))


def _expert_kernel(src_ref, te_ref, nu_ref, x_ref, wg_ref, wu_ref, wd_ref, o_ref):
    used = pl.program_id(0) < nu_ref[0]

    @pl.when(used)
    def _():
        x = x_ref[...].astype(BF16)
        a = _dot(x, wg_ref[0])
        he = a * _sigmoid(a) * _dot(x, wu_ref[0])
        o_ref[...] = _dot(he.astype(BF16), wd_ref[0])

    @pl.when(jnp.logical_not(used))
    def _():
        o_ref[...] = jnp.zeros_like(o_ref)


def _experts(hs, src, tile_e, n_used, wg, wu, wd, tm):
    n_slots, d = hs.shape
    f = wg.shape[2]
    return pl.pallas_call(
        _expert_kernel,
        grid_spec=pltpu.PrefetchScalarGridSpec(
            num_scalar_prefetch=3,
            grid=(n_slots // tm,),
            in_specs=[pl.BlockSpec((tm, d), lambda i, s, e, u: (s[i], 0)),
                      pl.BlockSpec((1, d, f), lambda i, s, e, u: (e[i], 0, 0)),
                      pl.BlockSpec((1, d, f), lambda i, s, e, u: (e[i], 0, 0)),
                      pl.BlockSpec((1, f, d), lambda i, s, e, u: (e[i], 0, 0))],
            out_specs=pl.BlockSpec((tm, d), lambda i, s, e, u: (i, 0))),
        out_shape=jax.ShapeDtypeStruct((n_slots, d), F32),
        compiler_params=_cparams(("arbitrary",)),
        name="moe_experts",
    )(src, tile_e, n_used, hs, wg, wu, wd)


def _combine_kernel(slot_ref, x_ref, w_ref, gt_ref, gf_ref, y_ref, o_ref, buf_ref, sem, *, n_tok, final):
    tm = x_ref.shape[0]
    base = pl.program_id(0) * tm

    def issue(j, carry):
        _row_copy(y_ref, slot_ref[base + j], buf_ref.at[0], j, sem).start()
        _row_copy(y_ref, slot_ref[n_tok + base + j], buf_ref.at[1], j, sem).start()
        return carry

    def drain(j, carry):
        _row_copy(y_ref, 0, buf_ref.at[0], 0, sem).wait()
        return carry

    lax.fori_loop(0, tm, issue, 0)
    lax.fori_loop(0, 2 * tm, drain, 0)
    w = w_ref[...]
    f = w[:, 0:1] * buf_ref[0] + w[:, 1:2] * buf_ref[1]
    xo = x_ref[...] + gt_ref[0] * f
    if final:
        xo = _rms(xo, gf_ref[...])
    o_ref[...] = xo


def _combine(slot, x, wts, mod3, g_final, y, dims, n_tok, tm, final):
    nl, t, b = dims
    d = x.shape[1]
    grp = functools.partial(_group_of, tm=tm, nl=nl, t=t, b=b)
    return pl.pallas_call(
        functools.partial(_combine_kernel, n_tok=n_tok, final=final),
        grid_spec=pltpu.PrefetchScalarGridSpec(
            num_scalar_prefetch=1,
            grid=(n_tok // tm,),
            in_specs=[pl.BlockSpec((tm, d), lambda i, s: (i, 0)),
                      pl.BlockSpec((tm, 2), lambda i, s: (i, 0)),
                      pl.BlockSpec((1, 1, d), lambda i, s: (grp(i) * 6 + 5, 0, 0)),
                      pl.BlockSpec((1, d), lambda i, s: (0, 0)),
                      pl.BlockSpec(memory_space=pl.ANY)],
            out_specs=pl.BlockSpec((tm, d), lambda i, s: (i, 0)),
            scratch_shapes=[pltpu.VMEM((2, tm, d), F32), pltpu.SemaphoreType.DMA(())]),
        out_shape=jax.ShapeDtypeStruct((n_tok, d), F32),
        compiler_params=_cparams(("arbitrary",)),
        name="moe_combine",
    )(slot, x, wts, mod3, g_final, y)


def _moe(h2, route, xn, mod3, g_final, wg, wu, wd, dims, final):
    n_tok, d = h2.shape
    max_tiles = (2 * n_tok) // MOE_TM + N_EXPERTS
    slot, src, tile_e, n_used, wts = _slot_plan(route, MOE_TM, max_tiles)
    hs = _dispatch(slot, h2, max_tiles * MOE_TM, ROW_TM)
    y = _experts(hs, src, tile_e, n_used, wg, wu, wd, MOE_TM)
    return _combine(slot, xn, wts, mod3, g_final, y, dims, n_tok, ROW_TM, final)


def _gate_weight(w_cg):
    d = w_cg.shape[0]
    per_head = w_cg.reshape(d, 4, M_HEADS).transpose(0, 2, 1)
    return jnp.pad(per_head, ((0, 0), (0, 0), (0, LANE - 4))).reshape(d, M_HEADS * LANE)


def _layer(x, mod3, dims, ctx_len, last, g1, g2, w_in, g_sgu, w_sp, b_sp, rpb, b_mgate, g_mnorm, w_branch, w_out,
           w_rt, b_r, wg, wu, wd, g_final):
    nl, t, b = dims
    nt, d = x.shape
    tm = 512 if (t % 512 == 0 and (nt - nl) % 512 == 0) else 256

    w_main = jnp.concatenate([w_in[:, :OFF_CG], w_in[:, OFF_CG + M_NGATE:]], axis=1).astype(BF16)
    w_gate = _gate_weight(w_in[:, OFF_CG:OFF_CG + M_NGATE]).astype(BF16)
    p = _inproj(x, mod3, g1, w_main, dims, tm, 1024, BF16)
    g = _inproj(x, mod3, g1, w_gate, dims, tm, M_HEADS * LANE, F32)

    n_rows = nl if last else nt
    b_full = jnp.repeat(b_sp.T, A_GDIM, axis=1)
    ya = _sgu(p, g_sgu[None, :], w_sp.astype(BF16), b_full, n_rows, tm)

    kh = min(NA_KH, t // GRID_W)
    yb_l, yb_c = _natten(p, _na_bias_table(rpb, kh), dims, ctx_len)

    bm = jnp.pad(b_mgate.T, ((0, 0), (0, LANE - 4))).reshape(1, M_HEADS * LANE)
    yc_l, yc_c = _mlstm(p, g, bm, g_mnorm[None, :], dims, ctx_len)

    if last:
        yb, yc = yb_l, yc_l
    else:
        yb = jnp.concatenate([yb_l, yb_c], axis=0)
        yc = jnp.concatenate([yc_l, yc_c], axis=0)

    merged = _merge(ya, yb, yc, p, w_branch.astype(BF16), n_rows, tm, 512)
    xn, h2, route = _outproj(merged, x, mod3, g2[None, :], w_out.astype(BF16), w_rt, b_r, dims, n_rows, 256)
    return _moe(h2, route, xn, mod3, g_final, wg.astype(BF16), wu.astype(BF16), wd.astype(BF16), dims, last)


def kernel(x, c, ctx, c_ctx, w_ada, b_ada, g_norm1, g_norm2, w_in, g_sgu, w_spatial, b_spatial, na_rpb, b_mgate,
           g_mnorm, w_branch, w_out, w_router, b_router, w_e_gate, w_e_up, w_e_down, g_final):
    b, t, d = x.shape
    ctx_len = ctx.shape[1]
    nl = b * t
    dims = (nl, t, b)
    depth = w_ada.shape[0]
    xs = jnp.concatenate([x.reshape(nl, d), ctx.reshape(b * ctx_len, d)], axis=0)
    n_mod = -(-(b + 1) // 8) * 8
    cc = jnp.zeros((n_mod, d), F32).at[:b].set(c).at[b].set(c_ctx)
    w_rt = w_router.T.astype(BF16)
    b_r = b_router[:, None].astype(F32)
    for layer in range(depth):
        mod = _ada(cc, w_ada[layer], b_ada[layer][None, :])
        mod3 = mod.reshape(n_mod * 6, 1, d)
        xs = _layer(xs, mod3, dims, ctx_len, layer == depth - 1, g_norm1[layer][None, :], g_norm2[layer],
                    w_in[layer], g_sgu[layer], w_spatial[layer], b_spatial[layer], na_rpb[layer], b_mgate[layer],
                    g_mnorm[layer], w_branch[layer], w_out[layer], w_rt, b_r, w_e_gate[layer], w_e_up[layer],
                    w_e_down[layer], g_final[None, :])
    return xs.reshape(b, t, d)
```

```python
import functools

import numpy as np
import jax
import jax.numpy as jnp
from jax import lax
from jax.experimental import pallas as pl
from jax.experimental.pallas import tpu as pltpu

F32 = jnp.float32
BF16 = jnp.bfloat16

GRID_W = 64
EPS = 1e-6
NEG_INF = -1e30
ROPE_THETA = 10000.0

A_WIDTH = 1024
A_CHUNK = 128
A_GROUPS = 8
A_GDIM = A_WIDTH // A_GROUPS

NA_HEADS = 8
NA_HDIM = 128
NA_WIDTH = NA_HEADS * NA_HDIM
NA_KH = 8
NA_KW = 16

M_HEADS = 4
M_DK = 128
M_DV = 256
M_QK_W = M_HEADS * M_DK
M_V_W = M_HEADS * M_DV
M_CHUNK = 128
M_NGATE = 4 * M_HEADS

N_BRANCH = 3
BRANCH_W = 1024

N_EXPERTS = 16
N_GROUPS = 4
EXP_PER_GROUP = N_EXPERTS // N_GROUPS
D_FF_EXPERT = 1024

OFF_AU = 0
OFF_AV = OFF_AU + A_WIDTH
OFF_BQ = OFF_AV + A_WIDTH
OFF_BK = OFF_BQ + NA_WIDTH
OFF_BV = OFF_BK + NA_WIDTH
OFF_CQ = OFF_BV + NA_WIDTH
OFF_CK = OFF_CQ + M_QK_W
OFF_CV = OFF_CK + M_QK_W
OFF_CO = OFF_CV + M_V_W
OFF_CG = OFF_CO + M_V_W
OFF_GATE = OFF_CO + M_V_W

LANE = 128
VMEM_LIMIT = 56 * 1024 * 1024

ROW_SEG = 16
MOE_TM = 256
ROW_TM = 256


def _cparams(sem):
    return pltpu.CompilerParams(dimension_semantics=sem, vmem_limit_bytes=VMEM_LIMIT)


def _dot(a, b):
    return jnp.dot(a, b, preferred_element_type=F32)


def _dot_nt(a, b):
    return lax.dot_general(a, b, (((1,), (1,)), ((), ())), preferred_element_type=F32)


def _sigmoid(x):
    return 1.0 / (1.0 + jnp.exp(-x))


def _gelu_tanh(x):
    return 0.5 * x * (1.0 + jnp.tanh(np.sqrt(2.0 / np.pi).astype(np.float32) * (x + 0.044715 * (x * x * x))))


def _rms(x, g):
    return x * lax.rsqrt(jnp.mean(x * x, axis=-1, keepdims=True) + EPS) * g


def _ada_kernel(c_ref, w_ref, b_ref, o_ref):
    cc = c_ref[...]
    s = cc * _sigmoid(cc)
    o_ref[...] = _dot(s.astype(BF16), w_ref[...].astype(BF16)) + b_ref[...]


def _ada(cc, w, b):
    rows, d = cc.shape
    n = w.shape[1]
    tn = 1024
    return pl.pallas_call(
        _ada_kernel,
        grid=(n // tn,),
        in_specs=[pl.BlockSpec((rows, d), lambda j: (0, 0)),
                  pl.BlockSpec((d, tn), lambda j: (0, j)),
                  pl.BlockSpec((1, tn), lambda j: (0, j))],
        out_specs=pl.BlockSpec((rows, tn), lambda j: (0, j)),
        out_shape=jax.ShapeDtypeStruct((rows, n), F32),
        compiler_params=_cparams(("arbitrary",)),
        name="ada_mod",
    )(cc, w, b)


def _inproj_kernel(x_ref, sh_ref, sc_ref, g_ref, w_ref, o_ref, h_ref):
    @pl.when(pl.program_id(1) == 0)
    def _():
        h = _rms(x_ref[...], g_ref[...]) * (1.0 + sc_ref[0]) + sh_ref[0]
        h_ref[...] = h.astype(BF16)

    o_ref[...] = _dot(h_ref[...], w_ref[...]).astype(o_ref.dtype)


def _group_of(i, tm, nl, t, b):
    return jnp.where(i * tm < nl, (i * tm) // t, b)


def _inproj(x, mod3, g, w, dims, tm, tn, out_dtype):
    nl, t, b = dims
    nt, d = x.shape
    n = w.shape[1]
    grp = functools.partial(_group_of, tm=tm, nl=nl, t=t, b=b)
    return pl.pallas_call(
        _inproj_kernel,
        grid=(nt // tm, n // tn),
        in_specs=[pl.BlockSpec((tm, d), lambda i, j: (i, 0)),
                  pl.BlockSpec((1, 1, d), lambda i, j: (grp(i) * 6 + 0, 0, 0)),
                  pl.BlockSpec((1, 1, d), lambda i, j: (grp(i) * 6 + 1, 0, 0)),
                  pl.BlockSpec((1, d), lambda i, j: (0, 0)),
                  pl.BlockSpec((d, tn), lambda i, j: (0, j))],
        out_specs=pl.BlockSpec((tm, tn), lambda i, j: (i, j)),
        out_shape=jax.ShapeDtypeStruct((nt, n), out_dtype),
        scratch_shapes=[pltpu.VMEM((tm, d), BF16)],
        compiler_params=_cparams(("parallel", "arbitrary")),
        name="inproj",
    )(x, mod3, mod3, g, w)


def _sgu_kernel(u_ref, v_ref, g_ref, w_ref, b_ref, o_ref):
    n_chunk = u_ref.shape[0] // A_CHUNK
    for c in range(n_chunk):
        rows = slice(c * A_CHUNK, (c + 1) * A_CHUNK)
        u = _gelu_tanh(u_ref[rows, :].astype(F32))
        v = _rms(_gelu_tanh(v_ref[rows, :].astype(F32)), g_ref[...]).astype(BF16)
        for gi in range(A_GROUPS):
            cols = slice(gi * A_GDIM, (gi + 1) * A_GDIM)
            mixed = _dot(w_ref[gi], v[:, cols]) + b_ref[:, cols]
            o_ref[rows, cols] = (u[:, cols] * mixed).astype(o_ref.dtype)


def _sgu(p, g_sgu, w_sp, b_full, n_rows, tm):
    return pl.pallas_call(
        _sgu_kernel,
        grid=(n_rows // tm,),
        in_specs=[pl.BlockSpec((tm, A_WIDTH), lambda i: (i, OFF_AU // A_WIDTH)),
                  pl.BlockSpec((tm, A_WIDTH), lambda i: (i, OFF_AV // A_WIDTH)),
                  pl.BlockSpec((1, A_WIDTH), lambda i: (0, 0)),
                  pl.BlockSpec((A_GROUPS, A_CHUNK, A_CHUNK), lambda i: (0, 0, 0)),
                  pl.BlockSpec((A_CHUNK, A_WIDTH), lambda i: (0, 0))],
        out_specs=pl.BlockSpec((tm, A_WIDTH), lambda i: (i, 0)),
        out_shape=jax.ShapeDtypeStruct((n_rows, A_WIDTH), BF16),
        compiler_params=_cparams(("parallel",)),
        name="sgu",
    )(p, p, g_sgu, w_sp, b_full)


NA_GQ = 4


def _na_geometry(rows):
    kh = min(NA_KH, rows)
    gq = min(NA_GQ, rows)
    wr = min(gq + kh - 1, rows)
    starts = [int(np.clip(gq * g - kh // 2, 0, rows - wr)) for g in range(rows // gq)]
    return kh, gq, wr, starts


def _na_bias_table(rpb, rows):
    kh, gq, wr, starts = _na_geometry(rows)
    n_rel = 2 * NA_KH - 1
    cq = np.arange(GRID_W)[:, None]
    kc = np.arange(GRID_W)[None, :]
    cs = np.clip(cq - NA_KW // 2, 0, GRID_W - NA_KW)
    valid = (kc >= cs) & (kc < cs + NA_KW)
    coff = np.clip(kc - cq, -(NA_KW - 1), NA_KW - 1) + NA_KW - 1
    col_sel = (coff[None, :, :] == np.arange(2 * NA_KW - 1)[:, None, None]).astype(np.float32)
    by_col = jnp.einsum('hrc,cqk->hrqk', rpb.astype(F32), col_sel, precision=lax.Precision.HIGHEST)
    by_col = jnp.where(valid[None, None], by_col, NEG_INF)
    by_col = jnp.concatenate([by_col, jnp.full_like(by_col[:, :1], NEG_INF)], axis=1)
    row_sel = np.zeros((len(starts), gq, wr, n_rel + 1), np.float32)
    for g, ws in enumerate(starts):
        for qi in range(gq):
            r = gq * g + qi
            rs = int(np.clip(r - kh // 2, 0, rows - kh))
            assert ws <= rs and rs + kh <= ws + wr
            for a in range(wr):
                inside = rs <= ws + a < rs + kh
                row_sel[g, qi, a, ws + a - r + NA_KH - 1 if inside else n_rel] = 1.0
    tab = jnp.einsum('gqar,hrck->hgqcak', row_sel, by_col, precision=lax.Precision.HIGHEST)
    return tab.reshape(rpb.shape[0], len(starts), gq * GRID_W, wr * GRID_W)


def _softmax_pv(s_parts, v_parts):
    m = functools.reduce(jnp.maximum, [jnp.max(s, axis=-1, keepdims=True) for s in s_parts])
    p_parts = [jnp.exp(s - m) for s in s_parts]
    l = functools.reduce(jnp.add, [jnp.sum(p, axis=-1, keepdims=True) for p in p_parts])
    o = functools.reduce(jnp.add, [_dot(p.astype(BF16), v) for p, v in zip(p_parts, v_parts)])
    return o / l


def _natten_kernel(q_ref, k_ref, v_ref, qc_ref, kc_ref, vc_ref, tab_ref, o_ref, oc_ref, *, rows):
    scale = NA_HDIM ** -0.5
    _, gq, wr, starts = _na_geometry(rows)
    kc = kc_ref[...]
    vc = vc_ref[...]
    for g, ws in enumerate(starts):
        qs = slice(g * gq * GRID_W, (g + 1) * gq * GRID_W)
        ks = slice(ws * GRID_W, (ws + wr) * GRID_W)
        q = q_ref[qs, :]
        s_w = _dot_nt(q, k_ref[ks, :]) * scale + tab_ref[0, g]
        s_c = _dot_nt(q, kc) * scale
        o_ref[qs, :] = _softmax_pv([s_w, s_c], [v_ref[ks, :], vc]).astype(o_ref.dtype)
    s = _dot_nt(qc_ref[...], kc) * scale
    oc_ref[...] = _softmax_pv([s], [vc]).astype(oc_ref.dtype)


def _natten(p, tab, dims, ctx_len):
    nl, t, b = dims
    rows = t // GRID_W
    cb = nl // ctx_len
    hd = NA_HDIM
    lat = lambda off: pl.BlockSpec((t, hd), lambda h, bi: (bi, off // hd + h))
    ctx = lambda off: pl.BlockSpec((ctx_len, hd), lambda h, bi: (cb + bi, off // hd + h))
    return pl.pallas_call(
        functools.partial(_natten_kernel, rows=rows),
        grid=(NA_HEADS, b),
        in_specs=[lat(OFF_BQ), lat(OFF_BK), lat(OFF_BV), ctx(OFF_BQ), ctx(OFF_BK), ctx(OFF_BV),
                  pl.BlockSpec((1,) + tab.shape[1:], lambda h, bi: (h, 0, 0, 0))],
        out_specs=[pl.BlockSpec((t, hd), lambda h, bi: (bi, h)),
                   pl.BlockSpec((ctx_len, hd), lambda h, bi: (bi, h))],
        out_shape=[jax.ShapeDtypeStruct((nl, NA_WIDTH), BF16),
                   jax.ShapeDtypeStruct((b * ctx_len, NA_WIDTH), BF16)],
        compiler_params=_cparams(("parallel", "arbitrary")),
        name="natten",
    )(p, p, p, p, p, p, tab)


def _scan_rows(x, reverse, op, fill):
    n = x.shape[0]
    idx = lax.broadcasted_iota(jnp.int32, x.shape, 0)
    step = 1
    while step < n:
        if reverse:
            x = op(x, jnp.where(idx < n - step, pltpu.roll(x, n - step, 0), fill))
        else:
            x = op(x, jnp.where(idx >= step, pltpu.roll(x, step, 0), fill))
        step *= 2
    return x


def _log_sigmoid(x):
    return jnp.minimum(x, 0.0) - jnp.log(1.0 + jnp.exp(-jnp.abs(x)))


def _rope_swap(x):
    lane = lax.broadcasted_iota(jnp.int32, x.shape, 1)
    return jnp.where(lane % 64 < 32, pltpu.roll(x, LANE - 32, 1), pltpu.roll(x, 32, 1))


def _chunk_rows(c):
    return pl.ds(pl.multiple_of(c * M_CHUNK, M_CHUNK), M_CHUNK)


def _gate_rows(c):
    return pl.ds(pl.multiple_of(c * 8, 8), 4)


def _mlstm_prepare(seq, bm_ref, rope):
    shape = (M_CHUNK, LANE)

    def body(c, carry):
        rows = _chunk_rows(c)
        gc = seq["g"][rows, :] + bm_ref[...]
        lf = _log_sigmoid(gc)
        xs = []
        for d in range(2):
            reverse = d == 1
            b = _scan_rows(lf, reverse, jnp.add, 0.0)
            a_rep = jnp.broadcast_to(b[:, 2 * d + 1:2 * d + 2], shape)
            x_rep = jnp.broadcast_to(gc[:, 2 * d:2 * d + 1], shape) - a_rep
            seq["a"][d, rows, :] = a_rep
            seq["x"][d, rows, :] = x_rep
            seq["mi"][d, rows, :] = a_rep + _scan_rows(x_rep, reverse, jnp.maximum, -jnp.inf)
            seq["row"][pl.ds(c * 8 + 2 * d + 1, 1), :] = jnp.max(x_rep, axis=0, keepdims=True)
            xs.append(x_rep)
        lane = lax.broadcasted_iota(jnp.int32, shape, 1)
        xt = jnp.where(lane < LANE // 2, xs[0], xs[1]).T
        seq["row"][pl.ds(c * 8, 1), :] = xt[0:1, :]
        seq["row"][pl.ds(c * 8 + 2, 1), :] = xt[LANE // 2:LANE // 2 + 1, :]
        q = seq["q"][rows, :].astype(F32)
        k = seq["k"][rows, :].astype(F32)
        if rope is not None:
            cos = rope[0][rows, :]
            sin = rope[1][rows, :]
            q = q * cos + _rope_swap(q) * sin
            k = k * cos + _rope_swap(k) * sin
        seq["qs"][rows, :] = (q * (M_DK ** -0.5)).astype(BF16)
        seq["ks"][rows, :] = k
        return carry

    lax.fori_loop(0, seq["n"], body, 0)


def _mlstm_step(seq, c, reverse, state, order):
    c_ref, m_ref = state
    d = 1 if reverse else 0
    ln = M_CHUNK
    rows = _chunk_rows(c)
    a = seq["a"][d, rows, :]
    x = seq["x"][d, rows, :]
    gate_rows = seq["row"][_gate_rows(c), :]
    x_row, x_max = gate_rows[2 * d:2 * d + 1, :], gate_rows[2 * d + 1:2 * d + 2, :]
    q = seq["qs"][rows, :]
    k = seq["ks"][rows, :]
    vx = jnp.concatenate([seq["v"][rows, :], jnp.ones((ln, LANE), BF16)], axis=1)
    m_prev = m_ref[d]
    c_prev = c_ref[d]

    inter = a + m_prev
    m_row = jnp.maximum(inter, seq["mi"][d, rows, :])
    s = _dot_nt(q, k.astype(BF16)) * jnp.exp(jnp.where(order, a + x_row, -jnp.inf) - m_row)
    ew = jnp.exp(inter - m_row)
    tot = _dot(s.astype(BF16), vx) + jnp.concatenate([ew] * 3, axis=1) * _dot(q, c_prev.astype(BF16))
    inv = 1.0 / jnp.maximum(jnp.abs(tot[:, M_DV:]), jnp.exp(-m_row))
    seq["hb" if reverse else "hf"][rows, :] = tot[:, :M_DV] * jnp.concatenate([inv] * 2, axis=1)

    b_last = a[0:1, :] if reverse else a[ln - 1:ln, :]
    m_new = jnp.maximum(b_last + m_prev, b_last + x_max)
    decay = jnp.exp(b_last + m_prev - m_new)
    kw = k * jnp.exp(b_last + x - m_new)
    c_ref[d] = jnp.concatenate([decay] * 3, axis=1) * c_prev + _dot(kw.T.astype(BF16), vx)
    m_ref[d] = m_new


def _mlstm_scan(seq, state):
    n = seq["n"]
    t_idx = lax.broadcasted_iota(jnp.int32, (M_CHUNK, M_CHUNK), 0)
    s_idx = lax.broadcasted_iota(jnp.int32, (M_CHUNK, M_CHUNK), 1)

    def body(ci, carry):
        _mlstm_step(seq, ci, False, state, t_idx >= s_idx)
        _mlstm_step(seq, n - 1 - ci, True, state, s_idx >= t_idx)
        return carry

    lax.fori_loop(0, n, body, 0)


def _mlstm_finish(seq, gn_ref):
    def body(c, carry):
        rows = _chunk_rows(c)
        y = _rms(seq["hf"][rows, :] + seq["hb"][rows, :], gn_ref[...])
        seq["y"][rows, :] = (y * _sigmoid(seq["o"][rows, :].astype(F32))).astype(seq["y"].dtype)
        return carry

    lax.fori_loop(0, seq["n"], body, 0)


_SEQ_IN = ("q", "k", "v", "o", "g")
_SEQ_SCRATCH = ("qs", "ks", "a", "x", "mi", "row", "hf", "hb")


def _mlstm_kernel(*refs):
    lat = dict(zip(_SEQ_IN, refs[0:5]))
    ctx = dict(zip(_SEQ_IN, refs[5:10]))
    bm_ref, gn_ref, cos_ref, sin_ref = refs[10:14]
    lat["y"], ctx["y"] = refs[14:16]
    lat.update(zip(_SEQ_SCRATCH, refs[16:24]))
    ctx.update(zip(_SEQ_SCRATCH, refs[24:32]))
    state = refs[32:34]
    for seq in (lat, ctx):
        seq["n"] = seq["q"].shape[0] // M_CHUNK

    _mlstm_prepare(ctx, bm_ref, None)
    _mlstm_prepare(lat, bm_ref, (cos_ref, sin_ref))
    for ref in state:
        ref[...] = jnp.zeros_like(ref)
    _mlstm_scan(ctx, state)
    _mlstm_scan(lat, state)
    _mlstm_finish(ctx, gn_ref)
    _mlstm_finish(lat, gn_ref)


def _rope_tables(t):
    pos = np.arange(t)
    half = M_DK // 4
    inv = ROPE_THETA ** (-np.arange(half, dtype=np.float64) / half)
    ang_r = (pos // GRID_W)[:, None] * inv[None, :]
    ang_c = (pos % GRID_W)[:, None] * inv[None, :]
    ang = np.concatenate([ang_r, ang_r, ang_c, ang_c], axis=1)
    sign = np.tile(np.concatenate([-np.ones(half), np.ones(half)]), 2)[None, :]
    return jnp.asarray(np.cos(ang), F32), jnp.asarray(np.sin(ang) * sign, F32)


def _seq_scratch(n):
    col = pltpu.VMEM((2, n, LANE), F32)
    return [pltpu.VMEM((n, M_DK), BF16), pltpu.VMEM((n, M_DK), F32), col, col, col,
            pltpu.VMEM((n // M_CHUNK * 8, LANE), F32), pltpu.VMEM((n, M_DV), F32), pltpu.VMEM((n, M_DV), F32)]


def _mlstm(p, g, bm, gn, dims, ctx_len):
    nl, t, b = dims
    cb = nl // ctx_len
    cos, sin = _rope_tables(t)
    dk, dv = M_DK, M_DV
    lat = lambda off, w: pl.BlockSpec((t, w), lambda bi, h: (bi, off // w + h))
    ctx = lambda off, w: pl.BlockSpec((ctx_len, w), lambda bi, h: (cb + bi, off // w + h))
    const = lambda shape: pl.BlockSpec(shape, lambda bi, h: (0, 0))
    return pl.pallas_call(
        _mlstm_kernel,
        grid=(b, M_HEADS),
        in_specs=[lat(OFF_CQ, dk), lat(OFF_CK, dk), lat(OFF_CV, dv), lat(OFF_CO, dv),
                  pl.BlockSpec((t, LANE), lambda bi, h: (bi, h)),
                  ctx(OFF_CQ, dk), ctx(OFF_CK, dk), ctx(OFF_CV, dv), ctx(OFF_CO, dv),
                  pl.BlockSpec((ctx_len, LANE), lambda bi, h: (cb + bi, h)),
                  pl.BlockSpec((1, LANE), lambda bi, h: (0, h)),
                  pl.BlockSpec((1, dv), lambda bi, h: (0, h)),
                  const((t, LANE)), const((t, LANE))],
        out_specs=[pl.BlockSpec((t, dv), lambda bi, h: (bi, h)),
                   pl.BlockSpec((ctx_len, dv), lambda bi, h: (bi, h))],
        out_shape=[jax.ShapeDtypeStruct((nl, M_V_W), BF16),
                   jax.ShapeDtypeStruct((b * ctx_len, M_V_W), BF16)],
        scratch_shapes=_seq_scratch(t) + _seq_scratch(ctx_len) + [
            pltpu.VMEM((2, dk, dv + LANE), F32), pltpu.VMEM((2, 1, LANE), F32)],
        compiler_params=_cparams(("parallel", "arbitrary")),
        name="mlstm",
    )(p, p, p, p, g, p, p, p, p, g, bm, gn, cos, sin)


def _merge_kernel(ya_ref, yb_ref, yc_ref, g0_ref, g1_ref, g2_ref, w_ref, o_ref):
    acc = _sigmoid(g0_ref[...].astype(F32)) * _dot(ya_ref[...], w_ref[0])
    acc = acc + _sigmoid(g1_ref[...].astype(F32)) * _dot(yb_ref[...], w_ref[1])
    acc = acc + _sigmoid(g2_ref[...].astype(F32)) * _dot(yc_ref[...], w_ref[2])
    o_ref[...] = acc.astype(o_ref.dtype)


def _merge(ya, yb, yc, p, w_branch, n_rows, tm, tn):
    d = w_branch.shape[2]
    br = lambda: pl.BlockSpec((tm, BRANCH_W), lambda i, j: (i, 0))
    gate = lambda k: pl.BlockSpec((tm, tn), lambda i, j: (i, (OFF_GATE + k * d) // tn + j))
    return pl.pallas_call(
        _merge_kernel,
        grid=(n_rows // tm, d // tn),
        in_specs=[br(), br(), br(), gate(0), gate(1), gate(2),
                  pl.BlockSpec((N_BRANCH, BRANCH_W, tn), lambda i, j: (0, 0, j))],
        out_specs=pl.BlockSpec((tm, tn), lambda i, j: (i, j)),
        out_shape=jax.ShapeDtypeStruct((n_rows, d), BF16),
        compiler_params=_cparams(("parallel", "arbitrary")),
        name="merge",
    )(ya, yb, yc, p, p, p, w_branch)


def _route(scores, sel):
    row = lambda a, e: a[e:e + 1, :]
    best = None
    for gi in range(N_GROUPS):
        a, b, c, d = (row(sel, gi * EXP_PER_GROUP + j) for j in range(EXP_PER_GROUP))
        hi1, lo1 = jnp.maximum(a, b), jnp.minimum(a, b)
        hi2, lo2 = jnp.maximum(c, d), jnp.minimum(c, d)
        gs = jnp.maximum(hi1, hi2) + jnp.maximum(jnp.minimum(hi1, hi2), jnp.maximum(lo1, lo2))
        if best is None:
            best, g_idx = gs, jnp.zeros_like(gs, dtype=jnp.int32)
        else:
            take = gs > best
            best = jnp.where(take, gs, best)
            g_idx = jnp.where(take, gi, g_idx)

    def in_group(a, j):
        out = row(a, j)
        for gi in range(1, N_GROUPS):
            out = jnp.where(g_idx == gi, row(a, gi * EXP_PER_GROUP + j), out)
        return out

    v = [in_group(sel, j) for j in range(EXP_PER_GROUP)]
    u = [in_group(scores, j) for j in range(EXP_PER_GROUP)]

    def argmax_first(vals):
        bv, bi, bu = vals[0], jnp.zeros_like(g_idx), u[0]
        for j in range(1, EXP_PER_GROUP):
            take = vals[j] > bv
            bv = jnp.where(take, vals[j], bv)
            bi = jnp.where(take, j, bi)
            bu = jnp.where(take, u[j], bu)
        return bi, bu

    i1, s1 = argmax_first(v)
    i2, s2 = argmax_first([jnp.where(i1 == j, -jnp.inf, v[j]) for j in range(EXP_PER_GROUP)])
    tot = s1 + s2
    return g_idx * EXP_PER_GROUP + i1, g_idx * EXP_PER_GROUP + i2, s1 / tot, s2 / tot


def _to_tall(ref, val):
    rows = val.shape[0]
    for j in range(ROW_SEG):
        ref[pl.ds(j, rows, stride=ROW_SEG), :] = val[:, j * LANE:(j + 1) * LANE]


def _from_tall(ref):
    rows = ref.shape[0] // ROW_SEG
    return jnp.concatenate([ref[pl.ds(j, rows, stride=ROW_SEG), :] for j in range(ROW_SEG)], axis=1)


def _outproj_kernel(mg_ref, x_ref, gt_ref, sh_ref, sc_ref, g_ref, w_ref, wr_ref, br_ref, xo_ref, h_ref, r_ref):
    y = _dot(mg_ref[...], w_ref[...])
    xn = x_ref[...] + gt_ref[0] * y
    xo_ref[...] = xn
    h = _rms(xn, g_ref[...]) * (1.0 + sc_ref[0]) + sh_ref[0]
    _to_tall(h_ref, h)
    scores = _sigmoid(_dot_nt(wr_ref[...], h.astype(BF16)))
    e1, e2, w1, w2 = _route(scores, scores + br_ref[...])
    r_ref[...] = jnp.zeros_like(r_ref)
    r_ref[0:1, :] = e1.astype(F32)
    r_ref[1:2, :] = e2.astype(F32)
    r_ref[2:3, :] = w1
    r_ref[3:4, :] = w2


def _outproj(merged, x, mod3, g2, w_out, w_rt, b_r, dims, n_rows, tm):
    nl, t, b = dims
    d = x.shape[1]
    grp = functools.partial(_group_of, tm=tm, nl=nl, t=t, b=b)
    mrow = lambda k: pl.BlockSpec((1, 1, d), lambda i: (grp(i) * 6 + k, 0, 0))
    return pl.pallas_call(
        _outproj_kernel,
        grid=(n_rows // tm,),
        in_specs=[pl.BlockSpec((tm, d), lambda i: (i, 0)),
                  pl.BlockSpec((tm, d), lambda i: (i, 0)),
                  mrow(2), mrow(3), mrow(4),
                  pl.BlockSpec((1, d), lambda i: (0, 0)),
                  pl.BlockSpec((d, d), lambda i: (0, 0)),
                  pl.BlockSpec((N_EXPERTS, d), lambda i: (0, 0)),
                  pl.BlockSpec((N_EXPERTS, 1), lambda i: (0, 0))],
        out_specs=[pl.BlockSpec((tm, d), lambda i: (i, 0)),
                   pl.BlockSpec((tm * ROW_SEG, LANE), lambda i: (i, 0)),
                   pl.BlockSpec((8, tm), lambda i: (0, i))],
        out_shape=[jax.ShapeDtypeStruct((n_rows, d), F32),
                   jax.ShapeDtypeStruct((n_rows * ROW_SEG, LANE), F32),
                   jax.ShapeDtypeStruct((8, n_rows), F32)],
        compiler_params=_cparams(("parallel",)),
        name="outproj_route",
    )(merged, x, mod3, mod3, mod3, g2, w_out, w_rt, b_r)


def _slot_plan(route, tm, max_tiles):
    n = route.shape[1]
    e = route[0:2, :].astype(jnp.int32).reshape(-1)
    oh = (e[:, None] == jnp.arange(N_EXPERTS, dtype=jnp.int32)[None, :]).astype(jnp.int32)
    cs = jnp.cumsum(oh, axis=0)
    rank = jnp.sum(cs * oh, axis=1) - 1
    counts = cs[-1]
    padded = ((counts + tm - 1) // tm) * tm
    ends = jnp.cumsum(padded)
    slot = jnp.sum(oh * (ends - padded)[None, :], axis=1) + rank
    tile_ends = ends // tm
    n_used = tile_ends[-1]
    tiles = jnp.arange(max_tiles, dtype=jnp.int32)
    src = jnp.minimum(tiles, n_used - 1)
    tile_e = jnp.minimum(jnp.sum((src[:, None] >= tile_ends[None, :]).astype(jnp.int32), axis=1), N_EXPERTS - 1)
    wts = route[2:4, :].T
    return slot.astype(jnp.int32), src.astype(jnp.int32), tile_e.astype(jnp.int32), n_used.reshape(1).astype(jnp.int32), wts


def _row_copy(src_ref, src_row, dst_ref, dst_row, sem):
    src = src_ref.at[pl.ds(pl.multiple_of(src_row * ROW_SEG, ROW_SEG), ROW_SEG), :]
    dst = dst_ref.at[pl.ds(pl.multiple_of(dst_row * ROW_SEG, ROW_SEG), ROW_SEG), :]
    return pltpu.make_async_copy(src, dst, sem)


def _dispatch_kernel(slot_ref, h_ref, init_ref, hs_ref, sem, *, n_tok):
    del init_ref
    tm = h_ref.shape[0] // ROW_SEG
    base = pl.program_id(0) * tm

    def issue(j, carry):
        _row_copy(h_ref, j, hs_ref, slot_ref[base + j], sem).start()
        _row_copy(h_ref, j, hs_ref, slot_ref[n_tok + base + j], sem).start()
        return carry

    def drain(j, carry):
        _row_copy(h_ref, 0, hs_ref, 0, sem).wait()
        return carry

    lax.fori_loop(0, tm, issue, 0)
    lax.fori_loop(0, 2 * tm, drain, 0)


def _dispatch(slot, h, n_slots, tm):
    n_tok = h.shape[0] // ROW_SEG
    return pl.pallas_call(
        functools.partial(_dispatch_kernel, n_tok=n_tok),
        grid_spec=pltpu.PrefetchScalarGridSpec(
            num_scalar_prefetch=1,
            grid=(n_tok // tm,),
            in_specs=[pl.BlockSpec((tm * ROW_SEG, LANE), lambda i, s: (i, 0)),
                      pl.BlockSpec(memory_space=pl.ANY)],
            out_specs=pl.BlockSpec(memory_space=pl.ANY),
            scratch_shapes=[pltpu.SemaphoreType.DMA(())]),
        out_shape=jax.ShapeDtypeStruct((n_slots * ROW_SEG, LANE), F32),
        input_output_aliases={2: 0},
        compiler_params=_cparams(("arbitrary",)),
        name="moe_dispatch",
    )(slot, h, jnp.zeros((n_slots * ROW_SEG, LANE), F32))


def _expert_kernel(src_ref, te_ref, nu_ref, x_ref, wg_ref, wu_ref, wd_ref, o_ref):
    used = pl.program_id(0) < nu_ref[0]

    @pl.when(used)
    def _():
        x = _from_tall(x_ref).astype(BF16)
        a = _dot(x, wg_ref[0])
        he = a * _sigmoid(a) * _dot(x, wu_ref[0])
        _to_tall(o_ref, _dot(he.astype(BF16), wd_ref[0]))

    @pl.when(jnp.logical_not(used))
    def _():
        o_ref[...] = jnp.zeros_like(o_ref)


def _experts(hs, src, tile_e, n_used, wg, wu, wd, tm):
    d, f = wg.shape[1:]
    tall = lambda index: pl.BlockSpec((tm * ROW_SEG, LANE), index)
    return pl.pallas_call(
        _expert_kernel,
        grid_spec=pltpu.PrefetchScalarGridSpec(
            num_scalar_prefetch=3,
            grid=(hs.shape[0] // (tm * ROW_SEG),),
            in_specs=[tall(lambda i, s, e, u: (s[i], 0)),
                      pl.BlockSpec((1, d, f), lambda i, s, e, u: (e[i], 0, 0)),
                      pl.BlockSpec((1, d, f), lambda i, s, e, u: (e[i], 0, 0)),
                      pl.BlockSpec((1, f, d), lambda i, s, e, u: (e[i], 0, 0))],
            out_specs=tall(lambda i, s, e, u: (i, 0))),
        out_shape=jax.ShapeDtypeStruct(hs.shape, F32),
        compiler_params=_cparams(("arbitrary",)),
        name="moe_experts",
    )(src, tile_e, n_used, hs, wg, wu, wd)


def _combine_kernel(slot_ref, x_ref, w_ref, gt_ref, gf_ref, y_ref, o_ref, buf_ref, sem, *, n_tok, final):
    tm = x_ref.shape[0]
    base = pl.program_id(0) * tm

    def issue(j, carry):
        _row_copy(y_ref, slot_ref[base + j], buf_ref.at[0], j, sem).start()
        _row_copy(y_ref, slot_ref[n_tok + base + j], buf_ref.at[1], j, sem).start()
        return carry

    def drain(j, carry):
        _row_copy(y_ref, 0, buf_ref.at[0], 0, sem).wait()
        return carry

    lax.fori_loop(0, tm, issue, 0)
    lax.fori_loop(0, 2 * tm, drain, 0)
    w = w_ref[...]
    f = w[:, 0:1] * _from_tall(buf_ref.at[0]) + w[:, 1:2] * _from_tall(buf_ref.at[1])
    xo = x_ref[...] + gt_ref[0] * f
    if final:
        xo = _rms(xo, gf_ref[...])
    o_ref[...] = xo


def _combine(slot, x, wts, mod3, g_final, y, dims, n_tok, tm, final):
    nl, t, b = dims
    d = x.shape[1]
    grp = functools.partial(_group_of, tm=tm, nl=nl, t=t, b=b)
    return pl.pallas_call(
        functools.partial(_combine_kernel, n_tok=n_tok, final=final),
        grid_spec=pltpu.PrefetchScalarGridSpec(
            num_scalar_prefetch=1,
            grid=(n_tok // tm,),
            in_specs=[pl.BlockSpec((tm, d), lambda i, s: (i, 0)),
                      pl.BlockSpec((tm, 2), lambda i, s: (i, 0)),
                      pl.BlockSpec((1, 1, d), lambda i, s: (grp(i) * 6 + 5, 0, 0)),
                      pl.BlockSpec((1, d), lambda i, s: (0, 0)),
                      pl.BlockSpec(memory_space=pl.ANY)],
            out_specs=pl.BlockSpec((tm, d), lambda i, s: (i, 0)),
            scratch_shapes=[pltpu.VMEM((2, tm * ROW_SEG, LANE), F32), pltpu.SemaphoreType.DMA(())]),
        out_shape=jax.ShapeDtypeStruct((n_tok, d), F32),
        compiler_params=_cparams(("arbitrary",)),
        name="moe_combine",
    )(slot, x, wts, mod3, g_final, y)


def _moe(h2, route, xn, mod3, g_final, wg, wu, wd, dims, final):
    n_tok = xn.shape[0]
    max_tiles = (2 * n_tok) // MOE_TM + N_EXPERTS
    slot, src, tile_e, n_used, wts = _slot_plan(route, MOE_TM, max_tiles)
    hs = _dispatch(slot, h2, max_tiles * MOE_TM, ROW_TM)
    y = _experts(hs, src, tile_e, n_used, wg, wu, wd, MOE_TM)
    return _combine(slot, xn, wts, mod3, g_final, y, dims, n_tok, ROW_TM, final)


def _gate_weight(w_cg):
    d = w_cg.shape[0]
    per_head = w_cg.reshape(d, 4, M_HEADS).transpose(0, 2, 1)
    return jnp.pad(per_head, ((0, 0), (0, 0), (0, LANE - 4))).reshape(d, M_HEADS * LANE)


def _layer(x, mod3, dims, ctx_len, last, g1, g2, w_in, g_sgu, w_sp, b_sp, rpb, b_mgate, g_mnorm, w_branch, w_out,
           w_rt, b_r, wg, wu, wd, g_final):
    nl, t, b = dims
    nt, d = x.shape
    tm = 512 if (t % 512 == 0 and (nt - nl) % 512 == 0) else 256

    w_main = jnp.concatenate([w_in[:, :OFF_CG], w_in[:, OFF_CG + M_NGATE:]], axis=1).astype(BF16)
    w_gate = _gate_weight(w_in[:, OFF_CG:OFF_CG + M_NGATE]).astype(BF16)
    p = _inproj(x, mod3, g1, w_main, dims, tm, 1024, BF16)
    g = _inproj(x, mod3, g1, w_gate, dims, tm, M_HEADS * LANE, F32)

    n_rows = nl if last else nt
    b_full = jnp.repeat(b_sp.T, A_GDIM, axis=1)
    ya = _sgu(p, g_sgu[None, :], w_sp.astype(BF16), b_full, n_rows, tm)

    yb_l, yb_c = _natten(p, _na_bias_table(rpb, t // GRID_W), dims, ctx_len)

    bm = jnp.pad(b_mgate.T, ((0, 0), (0, LANE - 4))).reshape(1, M_HEADS * LANE)
    yc_l, yc_c = _mlstm(p, g, bm, g_mnorm[None, :], dims, ctx_len)

    if last:
        yb, yc = yb_l, yc_l
    else:
        yb = jnp.concatenate([yb_l, yb_c], axis=0)
        yc = jnp.concatenate([yc_l, yc_c], axis=0)

    merged = _merge(ya, yb, yc, p, w_branch.astype(BF16), n_rows, tm, 512)
    xn, h2, route = _outproj(merged, x, mod3, g2[None, :], w_out.astype(BF16), w_rt, b_r, dims, n_rows, 256)
    return _moe(h2, route, xn, mod3, g_final, wg.astype(BF16), wu.astype(BF16), wd.astype(BF16), dims, last)


def kernel(x, c, ctx, c_ctx, w_ada, b_ada, g_norm1, g_norm2, w_in, g_sgu, w_spatial, b_spatial, na_rpb, b_mgate,
           g_mnorm, w_branch, w_out, w_router, b_router, w_e_gate, w_e_up, w_e_down, g_final):
    b, t, d = x.shape
    ctx_len = ctx.shape[1]
    nl = b * t
    dims = (nl, t, b)
    depth = w_ada.shape[0]
    xs = jnp.concatenate([x.reshape(nl, d), ctx.reshape(b * ctx_len, d)], axis=0)
    n_mod = -(-(b + 1) // 8) * 8
    cc = jnp.zeros((n_mod, d), F32).at[:b].set(c).at[b].set(c_ctx)
    w_rt = w_router.T.astype(BF16)
    b_r = b_router[:, None].astype(F32)
    for layer in range(depth):
        mod = _ada(cc, w_ada[layer], b_ada[layer][None, :])
        mod3 = mod.reshape(n_mod * 6, 1, d)
        xs = _layer(xs, mod3, dims, ctx_len, layer == depth - 1, g_norm1[layer][None, :], g_norm2[layer],
                    w_in[layer], g_sgu[layer], w_spatial[layer], b_spatial[layer], na_rpb[layer], b_mgate[layer],
                    g_mnorm[layer], w_branch[layer], w_out[layer], w_rt, b_r, w_e_gate[layer], w_e_up[layer],
                    w_e_down[layer], g_final[None, :])
    return xs.reshape(b, t, d)
```

```python
import functools

import numpy as np
import jax
import jax.numpy as jnp
from jax import lax
from jax.experimental import pallas as pl
from jax.experimental.pallas import tpu as pltpu

F32 = jnp.float32
BF16 = jnp.bfloat16

GRID_W = 64
EPS = 1e-6
NEG_INF = -1e30
ROPE_THETA = 10000.0

A_WIDTH = 1024
A_CHUNK = 128
A_GROUPS = 8
A_GDIM = A_WIDTH // A_GROUPS

NA_HEADS = 8
NA_HDIM = 128
NA_WIDTH = NA_HEADS * NA_HDIM
NA_KH = 8
NA_KW = 16

M_HEADS = 4
M_DK = 128
M_DV = 256
M_QK_W = M_HEADS * M_DK
M_V_W = M_HEADS * M_DV
M_CHUNK = 128
M_NGATE = 4 * M_HEADS

N_BRANCH = 3
BRANCH_W = 1024

N_EXPERTS = 16
N_GROUPS = 4
EXP_PER_GROUP = N_EXPERTS // N_GROUPS
D_FF_EXPERT = 1024

OFF_AU = 0
OFF_AV = OFF_AU + A_WIDTH
OFF_BQ = OFF_AV + A_WIDTH
OFF_BK = OFF_BQ + NA_WIDTH
OFF_BV = OFF_BK + NA_WIDTH
OFF_CQ = OFF_BV + NA_WIDTH
OFF_CK = OFF_CQ + M_QK_W
OFF_CV = OFF_CK + M_QK_W
OFF_CO = OFF_CV + M_V_W
OFF_CG = OFF_CO + M_V_W
OFF_GATE = OFF_CO + M_V_W

LANE = 128
VMEM_LIMIT = 56 * 1024 * 1024

ROW_SEG = 16
MOE_TM = 256
ROW_TM = 256
DMA_UNROLL = 8


def _cparams(sem):
    return pltpu.CompilerParams(dimension_semantics=sem, vmem_limit_bytes=VMEM_LIMIT)


def _dot(a, b):
    return jnp.dot(a, b, preferred_element_type=F32)


def _dot_nt(a, b):
    return lax.dot_general(a, b, (((1,), (1,)), ((), ())), preferred_element_type=F32)


def _sigmoid(x):
    return 1.0 / (1.0 + jnp.exp(-x))


def _gelu_tanh(x):
    return 0.5 * x * (1.0 + jnp.tanh(np.sqrt(2.0 / np.pi).astype(np.float32) * (x + 0.044715 * (x * x * x))))


def _rms(x, g):
    return x * lax.rsqrt(jnp.mean(x * x, axis=-1, keepdims=True) + EPS) * g


def _ada_kernel(c_ref, w_ref, b_ref, o_ref):
    cc = c_ref[...]
    s = cc * _sigmoid(cc)
    o_ref[...] = _dot(s.astype(BF16), w_ref[...].astype(BF16)) + b_ref[...]


def _ada(cc, w, b):
    rows, d = cc.shape
    n = w.shape[1]
    tn = 1024
    return pl.pallas_call(
        _ada_kernel,
        grid=(n // tn,),
        in_specs=[pl.BlockSpec((rows, d), lambda j: (0, 0)),
                  pl.BlockSpec((d, tn), lambda j: (0, j)),
                  pl.BlockSpec((1, tn), lambda j: (0, j))],
        out_specs=pl.BlockSpec((rows, tn), lambda j: (0, j)),
        out_shape=jax.ShapeDtypeStruct((rows, n), F32),
        compiler_params=_cparams(("arbitrary",)),
        name="ada_mod",
    )(cc, w, b)


def _inproj_kernel(x_ref, sh_ref, sc_ref, g_ref, w_ref, o_ref, h_ref):
    @pl.when(pl.program_id(1) == 0)
    def _():
        h = _rms(x_ref[...], g_ref[...]) * (1.0 + sc_ref[0]) + sh_ref[0]
        h_ref[...] = h.astype(BF16)

    o_ref[...] = _dot(h_ref[...], w_ref[...]).astype(o_ref.dtype)


def _group_of(i, tm, nl, t, b):
    return jnp.where(i * tm < nl, (i * tm) // t, b)


def _inproj(x, mod3, g, w, dims, tm, tn, out_dtype):
    nl, t, b = dims
    nt, d = x.shape
    n = w.shape[1]
    grp = functools.partial(_group_of, tm=tm, nl=nl, t=t, b=b)
    return pl.pallas_call(
        _inproj_kernel,
        grid=(nt // tm, n // tn),
        in_specs=[pl.BlockSpec((tm, d), lambda i, j: (i, 0)),
                  pl.BlockSpec((1, 1, d), lambda i, j: (grp(i) * 6 + 0, 0, 0)),
                  pl.BlockSpec((1, 1, d), lambda i, j: (grp(i) * 6 + 1, 0, 0)),
                  pl.BlockSpec((1, d), lambda i, j: (0, 0)),
                  pl.BlockSpec((d, tn), lambda i, j: (0, j))],
        out_specs=pl.BlockSpec((tm, tn), lambda i, j: (i, j)),
        out_shape=jax.ShapeDtypeStruct((nt, n), out_dtype),
        scratch_shapes=[pltpu.VMEM((tm, d), BF16)],
        compiler_params=_cparams(("parallel", "arbitrary")),
        name="inproj",
    )(x, mod3, mod3, g, w)


def _sgu_kernel(u_ref, v_ref, g_ref, w_ref, b_ref, o_ref):
    n_chunk = u_ref.shape[0] // A_CHUNK
    for c in range(n_chunk):
        rows = slice(c * A_CHUNK, (c + 1) * A_CHUNK)
        u = _gelu_tanh(u_ref[rows, :].astype(F32))
        v = _rms(_gelu_tanh(v_ref[rows, :].astype(F32)), g_ref[...]).astype(BF16)
        for gi in range(A_GROUPS):
            cols = slice(gi * A_GDIM, (gi + 1) * A_GDIM)
            mixed = _dot(w_ref[gi], v[:, cols]) + b_ref[:, cols]
            o_ref[rows, cols] = (u[:, cols] * mixed).astype(o_ref.dtype)


def _sgu(p, g_sgu, w_sp, b_full, n_rows, tm):
    return pl.pallas_call(
        _sgu_kernel,
        grid=(n_rows // tm,),
        in_specs=[pl.BlockSpec((tm, A_WIDTH), lambda i: (i, OFF_AU // A_WIDTH)),
                  pl.BlockSpec((tm, A_WIDTH), lambda i: (i, OFF_AV // A_WIDTH)),
                  pl.BlockSpec((1, A_WIDTH), lambda i: (0, 0)),
                  pl.BlockSpec((A_GROUPS, A_CHUNK, A_CHUNK), lambda i: (0, 0, 0)),
                  pl.BlockSpec((A_CHUNK, A_WIDTH), lambda i: (0, 0))],
        out_specs=pl.BlockSpec((tm, A_WIDTH), lambda i: (i, 0)),
        out_shape=jax.ShapeDtypeStruct((n_rows, A_WIDTH), BF16),
        compiler_params=_cparams(("parallel",)),
        name="sgu",
    )(p, p, g_sgu, w_sp, b_full)


NA_GQ = 4


def _na_geometry(rows):
    kh = min(NA_KH, rows)
    gq = min(NA_GQ, rows)
    wr = min(gq + kh - 1, rows)
    starts = [int(np.clip(gq * g - kh // 2, 0, rows - wr)) for g in range(rows // gq)]
    return kh, gq, wr, starts


def _na_row_select(rows):
    kh, gq, wr, starts = _na_geometry(rows)
    n_rel = 2 * NA_KH - 1
    row_sel = np.zeros((len(starts), gq, wr, n_rel + 1), np.float32)
    for g, ws in enumerate(starts):
        for qi in range(gq):
            r = gq * g + qi
            rs = int(np.clip(r - kh // 2, 0, rows - kh))
            assert ws <= rs and rs + kh <= ws + wr
            for a in range(wr):
                inside = rs <= ws + a < rs + kh
                row_sel[g, qi, a, ws + a - r + NA_KH - 1 if inside else n_rel] = 1.0
    kinds, kind_of = [], []
    for g in range(len(starts)):
        same = [k for k, rep in enumerate(kinds) if np.array_equal(row_sel[rep], row_sel[g])]
        if not same:
            kinds.append(g)
        kind_of.append(same[0] if same else len(kinds) - 1)
    return row_sel[kinds], kind_of


def _na_bias_table(rpb, rows):
    kh, gq, wr, starts = _na_geometry(rows)
    n_rel = 2 * NA_KH - 1
    cq = np.arange(GRID_W)[:, None]
    kc = np.arange(GRID_W)[None, :]
    cs = np.clip(cq - NA_KW // 2, 0, GRID_W - NA_KW)
    valid = (kc >= cs) & (kc < cs + NA_KW)
    coff = np.clip(kc - cq, -(NA_KW - 1), NA_KW - 1) + NA_KW - 1
    col_sel = (coff[None, :, :] == np.arange(2 * NA_KW - 1)[:, None, None]).astype(np.float32)
    by_col = jnp.einsum('hrc,cqk->hrqk', rpb.astype(F32), col_sel, precision=lax.Precision.HIGHEST)
    by_col = jnp.where(valid[None, None], by_col, NEG_INF)
    by_col = jnp.concatenate([by_col, jnp.full_like(by_col[:, :1], NEG_INF)], axis=1)
    row_sel, _ = _na_row_select(rows)
    tab = jnp.einsum('gqar,hrck->hgqcak', row_sel, by_col, precision=lax.Precision.HIGHEST)
    return tab.reshape(rpb.shape[0], row_sel.shape[0], gq * GRID_W, wr * GRID_W)


def _softmax_pv(s_parts, v_parts):
    m = functools.reduce(jnp.maximum, [jnp.max(s, axis=-1, keepdims=True) for s in s_parts])
    p_parts = [jnp.exp(s - m) for s in s_parts]
    l = functools.reduce(jnp.add, [jnp.sum(p, axis=-1, keepdims=True) for p in p_parts])
    o = functools.reduce(jnp.add, [_dot(p.astype(BF16), v) for p, v in zip(p_parts, v_parts)])
    return o / l


def _natten_kernel(q_ref, k_ref, v_ref, qc_ref, kc_ref, vc_ref, tab_ref, o_ref, oc_ref, *, rows):
    scale = NA_HDIM ** -0.5
    _, gq, wr, starts = _na_geometry(rows)
    _, kind_of = _na_row_select(rows)
    kc = kc_ref[...]
    vc = vc_ref[...]
    for g, ws in enumerate(starts):
        qs = slice(g * gq * GRID_W, (g + 1) * gq * GRID_W)
        ks = slice(ws * GRID_W, (ws + wr) * GRID_W)
        q = q_ref[qs, :]
        s_w = _dot_nt(q, k_ref[ks, :]) * scale + tab_ref[0, kind_of[g]]
        s_c = _dot_nt(q, kc) * scale
        o_ref[qs, :] = _softmax_pv([s_w, s_c], [v_ref[ks, :], vc]).astype(o_ref.dtype)
    s = _dot_nt(qc_ref[...], kc) * scale
    oc_ref[...] = _softmax_pv([s], [vc]).astype(oc_ref.dtype)


def _natten(p, tab, dims, ctx_len):
    nl, t, b = dims
    rows = t // GRID_W
    cb = nl // ctx_len
    hd = NA_HDIM
    lat = lambda off: pl.BlockSpec((t, hd), lambda h, bi: (bi, off // hd + h))
    ctx = lambda off: pl.BlockSpec((ctx_len, hd), lambda h, bi: (cb + bi, off // hd + h))
    return pl.pallas_call(
        functools.partial(_natten_kernel, rows=rows),
        grid=(NA_HEADS, b),
        in_specs=[lat(OFF_BQ), lat(OFF_BK), lat(OFF_BV), ctx(OFF_BQ), ctx(OFF_BK), ctx(OFF_BV),
                  pl.BlockSpec((1,) + tab.shape[1:], lambda h, bi: (h, 0, 0, 0))],
        out_specs=[pl.BlockSpec((t, hd), lambda h, bi: (bi, h)),
                   pl.BlockSpec((ctx_len, hd), lambda h, bi: (bi, h))],
        out_shape=[jax.ShapeDtypeStruct((nl, NA_WIDTH), BF16),
                   jax.ShapeDtypeStruct((b * ctx_len, NA_WIDTH), BF16)],
        compiler_params=_cparams(("parallel", "arbitrary")),
        name="natten",
    )(p, p, p, p, p, p, tab)


def _scan_rows(x, reverse, op, fill):
    n = x.shape[0]
    idx = lax.broadcasted_iota(jnp.int32, x.shape, 0)
    step = 1
    while step < n:
        if reverse:
            x = op(x, jnp.where(idx < n - step, pltpu.roll(x, n - step, 0), fill))
        else:
            x = op(x, jnp.where(idx >= step, pltpu.roll(x, step, 0), fill))
        step *= 2
    return x


def _log_sigmoid(x):
    return jnp.minimum(x, 0.0) - jnp.log(1.0 + jnp.exp(-jnp.abs(x)))


def _rope_swap(x):
    lane = lax.broadcasted_iota(jnp.int32, x.shape, 1)
    return jnp.where(lane % 64 < 32, pltpu.roll(x, LANE - 32, 1), pltpu.roll(x, 32, 1))


def _chunk_rows(c):
    return pl.ds(pl.multiple_of(c * M_CHUNK, M_CHUNK), M_CHUNK)


def _gate_rows(c):
    return pl.ds(pl.multiple_of(c * 8, 8), 4)


def _mlstm_prepare(seq, bm_ref, rope):
    shape = (M_CHUNK, LANE)

    def body(c, carry):
        rows = _chunk_rows(c)
        gc = seq["g"][rows, :] + bm_ref[...]
        lf = _log_sigmoid(gc)
        xs = []
        for d in range(2):
            reverse = d == 1
            b = _scan_rows(lf, reverse, jnp.add, 0.0)
            a_rep = jnp.broadcast_to(b[:, 2 * d + 1:2 * d + 2], shape)
            x_rep = jnp.broadcast_to(gc[:, 2 * d:2 * d + 1], shape) - a_rep
            seq["a"][d, rows, :] = a_rep
            seq["x"][d, rows, :] = x_rep
            seq["mi"][d, rows, :] = a_rep + _scan_rows(x_rep, reverse, jnp.maximum, -jnp.inf)
            seq["row"][pl.ds(c * 8 + 2 * d + 1, 1), :] = jnp.max(x_rep, axis=0, keepdims=True)
            xs.append(x_rep)
        lane = lax.broadcasted_iota(jnp.int32, shape, 1)
        xt = jnp.where(lane < LANE // 2, xs[0], xs[1]).T
        seq["row"][pl.ds(c * 8, 1), :] = xt[0:1, :]
        seq["row"][pl.ds(c * 8 + 2, 1), :] = xt[LANE // 2:LANE // 2 + 1, :]
        q = seq["q"][rows, :].astype(F32)
        k = seq["k"][rows, :].astype(F32)
        if rope is not None:
            cos = rope[0][rows, :]
            sin = rope[1][rows, :]
            q = q * cos + _rope_swap(q) * sin
            k = k * cos + _rope_swap(k) * sin
        seq["qs"][rows, :] = (q * (M_DK ** -0.5)).astype(BF16)
        seq["ks"][rows, :] = k
        return carry

    lax.fori_loop(0, seq["n"], body, 0)


def _mlstm_step(seq, c, reverse, state, order):
    c_ref, m_ref = state
    d = 1 if reverse else 0
    ln = M_CHUNK
    rows = _chunk_rows(c)
    a = seq["a"][d, rows, :]
    x = seq["x"][d, rows, :]
    gate_rows = seq["row"][_gate_rows(c), :]
    x_row, x_max = gate_rows[2 * d:2 * d + 1, :], gate_rows[2 * d + 1:2 * d + 2, :]
    q = seq["qs"][rows, :]
    k = seq["ks"][rows, :]
    vx = jnp.concatenate([seq["v"][rows, :], jnp.ones((ln, LANE), BF16)], axis=1)
    m_prev = m_ref[d]
    c_prev = c_ref[d]

    inter = a + m_prev
    m_row = jnp.maximum(inter, seq["mi"][d, rows, :])
    s = _dot_nt(q, k.astype(BF16)) * jnp.exp(jnp.where(order, a + x_row, -jnp.inf) - m_row)
    ew = jnp.exp(inter - m_row)
    tot = _dot(s.astype(BF16), vx) + jnp.concatenate([ew] * 3, axis=1) * _dot(q, c_prev.astype(BF16))
    inv = 1.0 / jnp.maximum(jnp.abs(tot[:, M_DV:]), jnp.exp(-m_row))
    seq["hb" if reverse else "hf"][rows, :] = tot[:, :M_DV] * jnp.concatenate([inv] * 2, axis=1)

    b_last = a[0:1, :] if reverse else a[ln - 1:ln, :]
    m_new = jnp.maximum(b_last + m_prev, b_last + x_max)
    decay = jnp.exp(b_last + m_prev - m_new)
    kw = k * jnp.exp(b_last + x - m_new)
    c_ref[d] = jnp.concatenate([decay] * 3, axis=1) * c_prev + _dot(kw.T.astype(BF16), vx)
    m_ref[d] = m_new


def _mlstm_scan(seq, state):
    n = seq["n"]
    t_idx = lax.broadcasted_iota(jnp.int32, (M_CHUNK, M_CHUNK), 0)
    s_idx = lax.broadcasted_iota(jnp.int32, (M_CHUNK, M_CHUNK), 1)

    def body(ci, carry):
        _mlstm_step(seq, ci, False, state, t_idx >= s_idx)
        _mlstm_step(seq, n - 1 - ci, True, state, s_idx >= t_idx)
        return carry

    lax.fori_loop(0, n, body, 0)


def _mlstm_finish(seq, gn_ref):
    def body(c, carry):
        rows = _chunk_rows(c)
        y = _rms(seq["hf"][rows, :] + seq["hb"][rows, :], gn_ref[...])
        seq["y"][rows, :] = (y * _sigmoid(seq["o"][rows, :].astype(F32))).astype(seq["y"].dtype)
        return carry

    lax.fori_loop(0, seq["n"], body, 0)


_SEQ_IN = ("q", "k", "v", "o", "g")
_SEQ_SCRATCH = ("qs", "ks", "a", "x", "mi", "row", "hf", "hb")


def _mlstm_kernel(*refs):
    lat = dict(zip(_SEQ_IN, refs[0:5]))
    ctx = dict(zip(_SEQ_IN, refs[5:10]))
    bm_ref, gn_ref, cos_ref, sin_ref = refs[10:14]
    lat["y"], ctx["y"] = refs[14:16]
    lat.update(zip(_SEQ_SCRATCH, refs[16:24]))
    ctx.update(zip(_SEQ_SCRATCH, refs[24:32]))
    state = refs[32:34]
    for seq in (lat, ctx):
        seq["n"] = seq["q"].shape[0] // M_CHUNK

    _mlstm_prepare(ctx, bm_ref, None)
    _mlstm_prepare(lat, bm_ref, (cos_ref, sin_ref))
    for ref in state:
        ref[...] = jnp.zeros_like(ref)
    _mlstm_scan(ctx, state)
    _mlstm_scan(lat, state)
    _mlstm_finish(ctx, gn_ref)
    _mlstm_finish(lat, gn_ref)


def _rope_tables(t):
    pos = np.arange(t)
    half = M_DK // 4
    inv = ROPE_THETA ** (-np.arange(half, dtype=np.float64) / half)
    ang_r = (pos // GRID_W)[:, None] * inv[None, :]
    ang_c = (pos % GRID_W)[:, None] * inv[None, :]
    ang = np.concatenate([ang_r, ang_r, ang_c, ang_c], axis=1)
    sign = np.tile(np.concatenate([-np.ones(half), np.ones(half)]), 2)[None, :]
    return jnp.asarray(np.cos(ang), F32), jnp.asarray(np.sin(ang) * sign, F32)


def _seq_scratch(n):
    col = pltpu.VMEM((2, n, LANE), F32)
    return [pltpu.VMEM((n, M_DK), BF16), pltpu.VMEM((n, M_DK), F32), col, col, col,
            pltpu.VMEM((n // M_CHUNK * 8, LANE), F32), pltpu.VMEM((n, M_DV), F32), pltpu.VMEM((n, M_DV), F32)]


def _mlstm(p, g, bm, gn, dims, ctx_len):
    nl, t, b = dims
    cb = nl // ctx_len
    cos, sin = _rope_tables(t)
    dk, dv = M_DK, M_DV
    lat = lambda off, w: pl.BlockSpec((t, w), lambda bi, h: (bi, off // w + h))
    ctx = lambda off, w: pl.BlockSpec((ctx_len, w), lambda bi, h: (cb + bi, off // w + h))
    const = lambda shape: pl.BlockSpec(shape, lambda bi, h: (0, 0))
    return pl.pallas_call(
        _mlstm_kernel,
        grid=(b, M_HEADS),
        in_specs=[lat(OFF_CQ, dk), lat(OFF_CK, dk), lat(OFF_CV, dv), lat(OFF_CO, dv),
                  pl.BlockSpec((t, LANE), lambda bi, h: (bi, h)),
                  ctx(OFF_CQ, dk), ctx(OFF_CK, dk), ctx(OFF_CV, dv), ctx(OFF_CO, dv),
                  pl.BlockSpec((ctx_len, LANE), lambda bi, h: (cb + bi, h)),
                  pl.BlockSpec((1, LANE), lambda bi, h: (0, h)),
                  pl.BlockSpec((1, dv), lambda bi, h: (0, h)),
                  const((t, LANE)), const((t, LANE))],
        out_specs=[pl.BlockSpec((t, dv), lambda bi, h: (bi, h)),
                   pl.BlockSpec((ctx_len, dv), lambda bi, h: (bi, h))],
        out_shape=[jax.ShapeDtypeStruct((nl, M_V_W), BF16),
                   jax.ShapeDtypeStruct((b * ctx_len, M_V_W), BF16)],
        scratch_shapes=_seq_scratch(t) + _seq_scratch(ctx_len) + [
            pltpu.VMEM((2, dk, dv + LANE), F32), pltpu.VMEM((2, 1, LANE), F32)],
        compiler_params=_cparams(("parallel", "arbitrary")),
        name="mlstm",
    )(p, p, p, p, g, p, p, p, p, g, bm, gn, cos, sin)


def _merge_kernel(ya_ref, yb_ref, yc_ref, g0_ref, g1_ref, g2_ref, w_ref, o_ref):
    acc = _sigmoid(g0_ref[...].astype(F32)) * _dot(ya_ref[...], w_ref[0])
    acc = acc + _sigmoid(g1_ref[...].astype(F32)) * _dot(yb_ref[...], w_ref[1])
    acc = acc + _sigmoid(g2_ref[...].astype(F32)) * _dot(yc_ref[...], w_ref[2])
    o_ref[...] = acc.astype(o_ref.dtype)


def _merge(ya, yb, yc, p, w_branch, n_rows, tm, tn):
    d = w_branch.shape[2]
    br = lambda: pl.BlockSpec((tm, BRANCH_W), lambda i, j: (i, 0))
    gate = lambda k: pl.BlockSpec((tm, tn), lambda i, j: (i, (OFF_GATE + k * d) // tn + j))
    return pl.pallas_call(
        _merge_kernel,
        grid=(n_rows // tm, d // tn),
        in_specs=[br(), br(), br(), gate(0), gate(1), gate(2),
                  pl.BlockSpec((N_BRANCH, BRANCH_W, tn), lambda i, j: (0, 0, j))],
        out_specs=pl.BlockSpec((tm, tn), lambda i, j: (i, j)),
        out_shape=jax.ShapeDtypeStruct((n_rows, d), BF16),
        compiler_params=_cparams(("parallel", "arbitrary")),
        name="merge",
    )(ya, yb, yc, p, p, p, w_branch)


def _route(scores, sel):
    row = lambda a, e: a[e:e + 1, :]
    best = None
    for gi in range(N_GROUPS):
        a, b, c, d = (row(sel, gi * EXP_PER_GROUP + j) for j in range(EXP_PER_GROUP))
        hi1, lo1 = jnp.maximum(a, b), jnp.minimum(a, b)
        hi2, lo2 = jnp.maximum(c, d), jnp.minimum(c, d)
        gs = jnp.maximum(hi1, hi2) + jnp.maximum(jnp.minimum(hi1, hi2), jnp.maximum(lo1, lo2))
        if best is None:
            best, g_idx = gs, jnp.zeros_like(gs, dtype=jnp.int32)
        else:
            take = gs > best
            best = jnp.where(take, gs, best)
            g_idx = jnp.where(take, gi, g_idx)

    def in_group(a, j):
        out = row(a, j)
        for gi in range(1, N_GROUPS):
            out = jnp.where(g_idx == gi, row(a, gi * EXP_PER_GROUP + j), out)
        return out

    v = [in_group(sel, j) for j in range(EXP_PER_GROUP)]
    u = [in_group(scores, j) for j in range(EXP_PER_GROUP)]

    def argmax_first(vals):
        bv, bi, bu = vals[0], jnp.zeros_like(g_idx), u[0]
        for j in range(1, EXP_PER_GROUP):
            take = vals[j] > bv
            bv = jnp.where(take, vals[j], bv)
            bi = jnp.where(take, j, bi)
            bu = jnp.where(take, u[j], bu)
        return bi, bu

    i1, s1 = argmax_first(v)
    i2, s2 = argmax_first([jnp.where(i1 == j, -jnp.inf, v[j]) for j in range(EXP_PER_GROUP)])
    tot = s1 + s2
    return g_idx * EXP_PER_GROUP + i1, g_idx * EXP_PER_GROUP + i2, s1 / tot, s2 / tot


def _to_tall(ref, val):
    rows = val.shape[0]
    for j in range(ROW_SEG):
        ref[pl.ds(j, rows, stride=ROW_SEG), :] = val[:, j * LANE:(j + 1) * LANE]


def _from_tall(ref):
    rows = ref.shape[0] // ROW_SEG
    return jnp.concatenate([ref[pl.ds(j, rows, stride=ROW_SEG), :] for j in range(ROW_SEG)], axis=1)


def _outproj_kernel(mg_ref, x_ref, gt_ref, sh_ref, sc_ref, g_ref, w_ref, wr_ref, br_ref, xo_ref, h_ref, r_ref):
    y = _dot(mg_ref[...], w_ref[...])
    xn = x_ref[...] + gt_ref[0] * y
    xo_ref[...] = xn
    h = _rms(xn, g_ref[...]) * (1.0 + sc_ref[0]) + sh_ref[0]
    _to_tall(h_ref, h)
    scores = _sigmoid(_dot_nt(wr_ref[...], h.astype(BF16)))
    e1, e2, w1, w2 = _route(scores, scores + br_ref[...])
    r_ref[...] = jnp.zeros_like(r_ref)
    r_ref[0:1, :] = e1.astype(F32)
    r_ref[1:2, :] = e2.astype(F32)
    r_ref[2:3, :] = w1
    r_ref[3:4, :] = w2


def _outproj(merged, x, mod3, g2, w_out, w_rt, b_r, dims, n_rows, tm):
    nl, t, b = dims
    d = x.shape[1]
    grp = functools.partial(_group_of, tm=tm, nl=nl, t=t, b=b)
    mrow = lambda k: pl.BlockSpec((1, 1, d), lambda i: (grp(i) * 6 + k, 0, 0))
    return pl.pallas_call(
        _outproj_kernel,
        grid=(n_rows // tm,),
        in_specs=[pl.BlockSpec((tm, d), lambda i: (i, 0)),
                  pl.BlockSpec((tm, d), lambda i: (i, 0)),
                  mrow(2), mrow(3), mrow(4),
                  pl.BlockSpec((1, d), lambda i: (0, 0)),
                  pl.BlockSpec((d, d), lambda i: (0, 0)),
                  pl.BlockSpec((N_EXPERTS, d), lambda i: (0, 0)),
                  pl.BlockSpec((N_EXPERTS, 1), lambda i: (0, 0))],
        out_specs=[pl.BlockSpec((tm, d), lambda i: (i, 0)),
                   pl.BlockSpec((tm * ROW_SEG, LANE), lambda i: (i, 0)),
                   pl.BlockSpec((8, tm), lambda i: (0, i))],
        out_shape=[jax.ShapeDtypeStruct((n_rows, d), F32),
                   jax.ShapeDtypeStruct((n_rows * ROW_SEG, LANE), F32),
                   jax.ShapeDtypeStruct((8, n_rows), F32)],
        compiler_params=_cparams(("parallel",)),
        name="outproj_route",
    )(merged, x, mod3, mod3, mod3, g2, w_out, w_rt, b_r)


def _slot_plan(route, tm, max_tiles):
    n = route.shape[1]
    e = route[0:2, :].astype(jnp.int32).reshape(-1)
    oh = (e[:, None] == jnp.arange(N_EXPERTS, dtype=jnp.int32)[None, :]).astype(jnp.int32)
    cs = jnp.cumsum(oh, axis=0)
    rank = jnp.sum(cs * oh, axis=1) - 1
    counts = cs[-1]
    padded = ((counts + tm - 1) // tm) * tm
    ends = jnp.cumsum(padded)
    slot = jnp.sum(oh * (ends - padded)[None, :], axis=1) + rank
    tile_ends = ends // tm
    n_used = tile_ends[-1]
    tiles = jnp.arange(max_tiles, dtype=jnp.int32)
    src = jnp.minimum(tiles, n_used - 1)
    tile_e = jnp.minimum(jnp.sum((src[:, None] >= tile_ends[None, :]).astype(jnp.int32), axis=1), N_EXPERTS - 1)
    left = jnp.take(counts, tile_e) - (tiles - jnp.take(tile_ends - padded // tm, tile_e)) * tm
    n_valid = jnp.where(tiles < n_used, jnp.clip(left, 0, tm), 0)
    flat = jnp.arange(2 * n, dtype=jnp.int32)
    tok_of_slot = jnp.zeros((max_tiles * tm,), jnp.int32).at[slot].set(flat % n)
    dst_of_slot = jnp.zeros((max_tiles * tm,), jnp.int32).at[slot].set(flat)
    wts = route[2:4, :].T
    i32 = lambda a: a.astype(jnp.int32)
    return i32(tok_of_slot), i32(dst_of_slot), i32(tile_e), i32(n_valid), i32(n_used.reshape(1)), wts


def _row_copy(src_ref, src_row, dst_ref, dst_row, sem):
    src = src_ref.at[pl.ds(pl.multiple_of(src_row * ROW_SEG, ROW_SEG), ROW_SEG), :]
    dst = dst_ref.at[pl.ds(pl.multiple_of(dst_row * ROW_SEG, ROW_SEG), ROW_SEG), :]
    return pltpu.make_async_copy(src, dst, sem)


def _expert_kernel(tok_ref, dst_ref, te_ref, nv_ref, nu_ref, h_ref, wg_ref, wu_ref, wd_ref, y_ref,
                   xbuf, ybuf, gsem, ssem, *, tm):
    del te_ref
    i = pl.program_id(0)
    n_used = nu_ref[0]
    used = i < n_used
    cur = i % 2

    def for_rows(n_rows, body):
        def group(g, carry):
            for u in range(DMA_UNROLL):
                body(g * DMA_UNROLL + u)
            return carry

        def single(j, carry):
            body(j)
            return carry

        n_groups = n_rows // DMA_UNROLL
        lax.fori_loop(0, n_groups, group, 0)
        if not isinstance(n_rows, int) or n_rows % DMA_UNROLL:
            lax.fori_loop(n_groups * DMA_UNROLL, n_rows, single, 0)

    def wait_rows(buf_ref, n_rows, sem):
        part = buf_ref.at[pl.ds(0, n_rows * ROW_SEG), :]
        pltpu.make_async_copy(part, part, sem).wait()

    def gather(tile, buf, wait):
        if wait:
            wait_rows(xbuf.at[buf], tm, gsem.at[buf])
        else:
            for_rows(tm, lambda j: _row_copy(h_ref, tok_ref[tile * tm + j], xbuf.at[buf], j,
                                             gsem.at[buf]).start(priority=1))

    def scatter(tile, buf, wait):
        if wait:
            wait_rows(ybuf.at[buf], nv_ref[tile], ssem.at[buf])
        else:
            for_rows(nv_ref[tile], lambda j: _row_copy(ybuf.at[buf], j, y_ref, dst_ref[tile * tm + j],
                                                       ssem.at[buf]).start())

    @pl.when(i == 0)
    def _():
        gather(0, 0, False)

    @pl.when(i + 1 < n_used)
    def _():
        gather(i + 1, 1 - cur, False)

    @pl.when(used)
    def _():
        gather(i, cur, True)
        x = _from_tall(xbuf.at[cur]).astype(BF16)
        a = _dot(x, wg_ref[0])
        he = a * _sigmoid(a) * _dot(x, wu_ref[0])
        _to_tall(ybuf.at[cur], _dot(he.astype(BF16), wd_ref[0]))

    @pl.when(jnp.logical_and(i >= 1, i - 1 < n_used))
    def _():
        scatter(i - 1, 1 - cur, True)

    @pl.when(used)
    def _():
        scatter(i, cur, False)

    @pl.when(jnp.logical_and(i == pl.num_programs(0) - 1, used))
    def _():
        scatter(i, cur, True)


def _experts(h, tok_of_slot, dst_of_slot, tile_e, n_valid, n_used, wg, wu, wd, tm):
    d, f = wg.shape[1:]
    wspec = lambda shape: pl.BlockSpec(shape, lambda i, t, s, e, v, u: (e[i], 0, 0))
    buf = pltpu.VMEM((2, tm * ROW_SEG, LANE), F32)
    return pl.pallas_call(
        functools.partial(_expert_kernel, tm=tm),
        grid_spec=pltpu.PrefetchScalarGridSpec(
            num_scalar_prefetch=5,
            grid=(tile_e.shape[0],),
            in_specs=[pl.BlockSpec(memory_space=pl.ANY), wspec((1, d, f)), wspec((1, d, f)), wspec((1, f, d))],
            out_specs=pl.BlockSpec(memory_space=pl.ANY),
            scratch_shapes=[buf, buf, pltpu.SemaphoreType.DMA((2,)), pltpu.SemaphoreType.DMA((2,))]),
        out_shape=jax.ShapeDtypeStruct((2 * h.shape[0], LANE), F32),
        compiler_params=_cparams(("arbitrary",)),
        name="moe_experts",
    )(tok_of_slot, dst_of_slot, tile_e, n_valid, n_used, h, wg, wu, wd)


def _combine_kernel(x_ref, w_ref, gt_ref, gf_ref, y1_ref, y2_ref, o_ref, *, final):
    w = w_ref[...]
    f = w[:, 0:1] * _from_tall(y1_ref) + w[:, 1:2] * _from_tall(y2_ref)
    xo = x_ref[...] + gt_ref[0] * f
    if final:
        xo = _rms(xo, gf_ref[...])
    o_ref[...] = xo


def _combine(x, wts, mod3, g_final, y, dims, n_tok, tm, final):
    nl, t, b = dims
    d = x.shape[1]
    grp = functools.partial(_group_of, tm=tm, nl=nl, t=t, b=b)
    return pl.pallas_call(
        functools.partial(_combine_kernel, final=final),
        grid=(n_tok // tm,),
        in_specs=[pl.BlockSpec((tm, d), lambda i: (i, 0)),
                  pl.BlockSpec((tm, 2), lambda i: (i, 0)),
                  pl.BlockSpec((1, 1, d), lambda i: (grp(i) * 6 + 5, 0, 0)),
                  pl.BlockSpec((1, d), lambda i: (0, 0)),
                  pl.BlockSpec((tm * ROW_SEG, LANE), lambda i: (i, 0)),
                  pl.BlockSpec((tm * ROW_SEG, LANE), lambda i: (n_tok // tm + i, 0))],
        out_specs=pl.BlockSpec((tm, d), lambda i: (i, 0)),
        out_shape=jax.ShapeDtypeStruct((n_tok, d), F32),
        compiler_params=_cparams(("parallel",)),
        name="moe_combine",
    )(x, wts, mod3, g_final, y, y)


def _moe(h2, route, xn, mod3, g_final, wg, wu, wd, dims, final):
    n_tok = xn.shape[0]
    max_tiles = (2 * n_tok) // MOE_TM + N_EXPERTS
    tok_of_slot, dst_of_slot, tile_e, n_valid, n_used, wts = _slot_plan(route, MOE_TM, max_tiles)
    y = _experts(h2, tok_of_slot, dst_of_slot, tile_e, n_valid, n_used, wg, wu, wd, MOE_TM)
    return _combine(xn, wts, mod3, g_final, y, dims, n_tok, ROW_TM, final)


def _gate_weight(w_cg):
    d = w_cg.shape[0]
    per_head = w_cg.reshape(d, 4, M_HEADS).transpose(0, 2, 1)
    return jnp.pad(per_head, ((0, 0), (0, 0), (0, LANE - 4))).reshape(d, M_HEADS * LANE)


def _layer(x, mod3, dims, ctx_len, last, g1, g2, w_in, g_sgu, w_sp, b_sp, rpb, b_mgate, g_mnorm, w_branch, w_out,
           w_rt, b_r, wg, wu, wd, g_final):
    nl, t, b = dims
    nt, d = x.shape
    tm = 512 if (t % 512 == 0 and (nt - nl) % 512 == 0) else 256

    w_main = jnp.concatenate([w_in[:, :OFF_CG], w_in[:, OFF_CG + M_NGATE:]], axis=1).astype(BF16)
    w_gate = _gate_weight(w_in[:, OFF_CG:OFF_CG + M_NGATE]).astype(BF16)
    p = _inproj(x, mod3, g1, w_main, dims, tm, 1024, BF16)
    g = _inproj(x, mod3, g1, w_gate, dims, tm, M_HEADS * LANE, F32)

    n_rows = nl if last else nt
    b_full = jnp.repeat(b_sp.T, A_GDIM, axis=1)
    ya = _sgu(p, g_sgu[None, :], w_sp.astype(BF16), b_full, n_rows, tm)

    yb_l, yb_c = _natten(p, _na_bias_table(rpb, t // GRID_W), dims, ctx_len)

    bm = jnp.pad(b_mgate.T, ((0, 0), (0, LANE - 4))).reshape(1, M_HEADS * LANE)
    yc_l, yc_c = _mlstm(p, g, bm, g_mnorm[None, :], dims, ctx_len)

    if last:
        yb, yc = yb_l, yc_l
    else:
        yb = jnp.concatenate([yb_l, yb_c], axis=0)
        yc = jnp.concatenate([yc_l, yc_c], axis=0)

    merged = _merge(ya, yb, yc, p, w_branch.astype(BF16), n_rows, 256, d)
    xn, h2, route = _outproj(merged, x, mod3, g2[None, :], w_out.astype(BF16), w_rt, b_r, dims, n_rows, 256)
    return _moe(h2, route, xn, mod3, g_final, wg.astype(BF16), wu.astype(BF16), wd.astype(BF16), dims, last)


def kernel(x, c, ctx, c_ctx, w_ada, b_ada, g_norm1, g_norm2, w_in, g_sgu, w_spatial, b_spatial, na_rpb, b_mgate,
           g_mnorm, w_branch, w_out, w_router, b_router, w_e_gate, w_e_up, w_e_down, g_final):
    b, t, d = x.shape
    ctx_len = ctx.shape[1]
    nl = b * t
    dims = (nl, t, b)
    depth = w_ada.shape[0]
    xs = jnp.concatenate([x.reshape(nl, d), ctx.reshape(b * ctx_len, d)], axis=0)
    n_mod = -(-(b + 1) // 8) * 8
    cc = jnp.zeros((n_mod, d), F32).at[:b].set(c).at[b].set(c_ctx)
    w_rt = w_router.T.astype(BF16)
    b_r = b_router[:, None].astype(F32)
    for layer in range(depth):
        mod = _ada(cc, w_ada[layer], b_ada[layer][None, :])
        mod3 = mod.reshape(n_mod * 6, 1, d)
        xs = _layer(xs, mod3, dims, ctx_len, layer == depth - 1, g_norm1[layer][None, :], g_norm2[layer],
                    w_in[layer], g_sgu[layer], w_spatial[layer], b_spatial[layer], na_rpb[layer], b_mgate[layer],
                    g_mnorm[layer], w_branch[layer], w_out[layer], w_rt, b_r, w_e_gate[layer], w_e_up[layer],
                    w_e_down[layer], g_final[None, :])
    return xs.reshape(b, t, d)
```

```python
import functools

import numpy as np
import jax
import jax.numpy as jnp
from jax import lax
from jax.experimental import pallas as pl
from jax.experimental.pallas import tpu as pltpu

F32 = jnp.float32
BF16 = jnp.bfloat16

GRID_W = 64
EPS = 1e-6
NEG_INF = -1e30
ROPE_THETA = 10000.0

A_WIDTH = 1024
A_CHUNK = 128
A_GROUPS = 8
A_GDIM = A_WIDTH // A_GROUPS

NA_HEADS = 8
NA_HDIM = 128
NA_WIDTH = NA_HEADS * NA_HDIM
NA_KH = 8
NA_KW = 16

M_HEADS = 4
M_DK = 128
M_DV = 256
M_QK_W = M_HEADS * M_DK
M_V_W = M_HEADS * M_DV
M_CHUNK = 128
M_NGATE = 4 * M_HEADS

N_BRANCH = 3
BRANCH_W = 1024

N_EXPERTS = 16
N_GROUPS = 4
EXP_PER_GROUP = N_EXPERTS // N_GROUPS
D_FF_EXPERT = 1024

OFF_AU = 0
OFF_AV = OFF_AU + A_WIDTH
OFF_BQ = OFF_AV + A_WIDTH
OFF_BK = OFF_BQ + NA_WIDTH
OFF_BV = OFF_BK + NA_WIDTH
OFF_CQ = OFF_BV + NA_WIDTH
OFF_CK = OFF_CQ + M_QK_W
OFF_CV = OFF_CK + M_QK_W
OFF_CO = OFF_CV + M_V_W
OFF_CG = OFF_CO + M_V_W
OFF_GATE = OFF_CO + M_V_W

LANE = 128
VMEM_LIMIT = 56 * 1024 * 1024

ROW_SEG = 16
MOE_TM = 256
ROW_TM = 256
DMA_UNROLL = 8


def _cparams(sem):
    return pltpu.CompilerParams(dimension_semantics=sem, vmem_limit_bytes=VMEM_LIMIT)


def _dot(a, b):
    return jnp.dot(a, b, preferred_element_type=F32)


def _dot_nt(a, b):
    return lax.dot_general(a, b, (((1,), (1,)), ((), ())), preferred_element_type=F32)


def _sigmoid(x):
    return 1.0 / (1.0 + jnp.exp(-x))


def _gelu_tanh(x):
    return 0.5 * x * (1.0 + jnp.tanh(np.sqrt(2.0 / np.pi).astype(np.float32) * (x + 0.044715 * (x * x * x))))


def _rms(x, g):
    return x * lax.rsqrt(jnp.mean(x * x, axis=-1, keepdims=True) + EPS) * g


def _ada_kernel(c_ref, w_ref, b_ref, o_ref):
    cc = c_ref[...]
    s = cc * _sigmoid(cc)
    o_ref[...] = _dot(s.astype(BF16), w_ref[...].astype(BF16)) + b_ref[...]


def _ada(cc, w, b):
    rows, d = cc.shape
    n = w.shape[1]
    tn = 1024
    return pl.pallas_call(
        _ada_kernel,
        grid=(n // tn,),
        in_specs=[pl.BlockSpec((rows, d), lambda j: (0, 0)),
                  pl.BlockSpec((d, tn), lambda j: (0, j)),
                  pl.BlockSpec((1, tn), lambda j: (0, j))],
        out_specs=pl.BlockSpec((rows, tn), lambda j: (0, j)),
        out_shape=jax.ShapeDtypeStruct((rows, n), F32),
        compiler_params=_cparams(("arbitrary",)),
        name="ada_mod",
    )(cc, w, b)


def _inproj_kernel(x_ref, sh_ref, sc_ref, g_ref, w_ref, o_ref, h_ref):
    @pl.when(pl.program_id(1) == 0)
    def _():
        h = _rms(x_ref[...], g_ref[...]) * (1.0 + sc_ref[0]) + sh_ref[0]
        h_ref[...] = h.astype(BF16)

    o_ref[...] = _dot(h_ref[...], w_ref[...]).astype(o_ref.dtype)


def _group_of(i, tm, nl, t, b):
    return jnp.where(i * tm < nl, (i * tm) // t, b)


def _inproj(x, mod3, g, w, dims, tm, tn, out_dtype):
    nl, t, b = dims
    nt, d = x.shape
    n = w.shape[1]
    grp = functools.partial(_group_of, tm=tm, nl=nl, t=t, b=b)
    return pl.pallas_call(
        _inproj_kernel,
        grid=(nt // tm, n // tn),
        in_specs=[pl.BlockSpec((tm, d), lambda i, j: (i, 0)),
                  pl.BlockSpec((1, 1, d), lambda i, j: (grp(i) * 6 + 0, 0, 0)),
                  pl.BlockSpec((1, 1, d), lambda i, j: (grp(i) * 6 + 1, 0, 0)),
                  pl.BlockSpec((1, d), lambda i, j: (0, 0)),
                  pl.BlockSpec((d, tn), lambda i, j: (0, j))],
        out_specs=pl.BlockSpec((tm, tn), lambda i, j: (i, j)),
        out_shape=jax.ShapeDtypeStruct((nt, n), out_dtype),
        scratch_shapes=[pltpu.VMEM((tm, d), BF16)],
        compiler_params=_cparams(("parallel", "arbitrary")),
        name="inproj",
    )(x, mod3, mod3, g, w)


def _sgu_kernel(u_ref, v_ref, g_ref, w_ref, b_ref, o_ref):
    n_chunk = u_ref.shape[0] // A_CHUNK
    for c in range(n_chunk):
        rows = slice(c * A_CHUNK, (c + 1) * A_CHUNK)
        u = _gelu_tanh(u_ref[rows, :].astype(F32))
        v = _rms(_gelu_tanh(v_ref[rows, :].astype(F32)), g_ref[...]).astype(BF16)
        for gi in range(A_GROUPS):
            cols = slice(gi * A_GDIM, (gi + 1) * A_GDIM)
            mixed = _dot(w_ref[gi], v[:, cols]) + b_ref[:, cols]
            o_ref[rows, cols] = (u[:, cols] * mixed).astype(o_ref.dtype)


def _sgu(p, g_sgu, w_sp, b_full, n_rows, tm):
    return pl.pallas_call(
        _sgu_kernel,
        grid=(n_rows // tm,),
        in_specs=[pl.BlockSpec((tm, A_WIDTH), lambda i: (i, OFF_AU // A_WIDTH)),
                  pl.BlockSpec((tm, A_WIDTH), lambda i: (i, OFF_AV // A_WIDTH)),
                  pl.BlockSpec((1, A_WIDTH), lambda i: (0, 0)),
                  pl.BlockSpec((A_GROUPS, A_CHUNK, A_CHUNK), lambda i: (0, 0, 0)),
                  pl.BlockSpec((A_CHUNK, A_WIDTH), lambda i: (0, 0))],
        out_specs=pl.BlockSpec((tm, A_WIDTH), lambda i: (i, 0)),
        out_shape=jax.ShapeDtypeStruct((n_rows, A_WIDTH), BF16),
        compiler_params=_cparams(("parallel",)),
        name="sgu",
    )(p, p, g_sgu, w_sp, b_full)


NA_GQ = 4


def _na_geometry(rows):
    kh = min(NA_KH, rows)
    gq = min(NA_GQ, rows)
    wr = min(gq + kh - 1, rows)
    starts = [int(np.clip(gq * g - kh // 2, 0, rows - wr)) for g in range(rows // gq)]
    return kh, gq, wr, starts


def _na_row_select(rows):
    kh, gq, wr, starts = _na_geometry(rows)
    n_rel = 2 * NA_KH - 1
    row_sel = np.zeros((len(starts), gq, wr, n_rel + 1), np.float32)
    for g, ws in enumerate(starts):
        for qi in range(gq):
            r = gq * g + qi
            rs = int(np.clip(r - kh // 2, 0, rows - kh))
            assert ws <= rs and rs + kh <= ws + wr
            for a in range(wr):
                inside = rs <= ws + a < rs + kh
                row_sel[g, qi, a, ws + a - r + NA_KH - 1 if inside else n_rel] = 1.0
    kinds, kind_of = [], []
    for g in range(len(starts)):
        same = [k for k, rep in enumerate(kinds) if np.array_equal(row_sel[rep], row_sel[g])]
        if not same:
            kinds.append(g)
        kind_of.append(same[0] if same else len(kinds) - 1)
    return row_sel[kinds], kind_of


def _na_bias_table(rpb, rows):
    kh, gq, wr, starts = _na_geometry(rows)
    n_rel = 2 * NA_KH - 1
    cq = np.arange(GRID_W)[:, None]
    kc = np.arange(GRID_W)[None, :]
    cs = np.clip(cq - NA_KW // 2, 0, GRID_W - NA_KW)
    valid = (kc >= cs) & (kc < cs + NA_KW)
    coff = np.clip(kc - cq, -(NA_KW - 1), NA_KW - 1) + NA_KW - 1
    col_sel = (coff[None, :, :] == np.arange(2 * NA_KW - 1)[:, None, None]).astype(np.float32)
    by_col = jnp.einsum('hrc,cqk->hrqk', rpb.astype(F32), col_sel, precision=lax.Precision.HIGHEST)
    by_col = jnp.where(valid[None, None], by_col, NEG_INF)
    by_col = jnp.concatenate([by_col, jnp.full_like(by_col[:, :1], NEG_INF)], axis=1)
    row_sel, _ = _na_row_select(rows)
    tab = jnp.einsum('gqar,hrck->hgqcak', row_sel, by_col, precision=lax.Precision.HIGHEST)
    return tab.reshape(rpb.shape[0], row_sel.shape[0], gq * GRID_W, wr * GRID_W)


def _softmax_pv(s_parts, v_parts):
    m = functools.reduce(jnp.maximum, [jnp.max(s, axis=-1, keepdims=True) for s in s_parts])
    p_parts = [jnp.exp(s - m) for s in s_parts]
    l = functools.reduce(jnp.add, [jnp.sum(p, axis=-1, keepdims=True) for p in p_parts])
    o = functools.reduce(jnp.add, [_dot(p.astype(BF16), v) for p, v in zip(p_parts, v_parts)])
    return o / l


def _natten_kernel(q_ref, k_ref, v_ref, qc_ref, kc_ref, vc_ref, tab_ref, o_ref, oc_ref, *, rows):
    scale = NA_HDIM ** -0.5
    _, gq, wr, starts = _na_geometry(rows)
    _, kind_of = _na_row_select(rows)
    kc = kc_ref[...]
    vc = vc_ref[...]
    for g, ws in enumerate(starts):
        qs = slice(g * gq * GRID_W, (g + 1) * gq * GRID_W)
        ks = slice(ws * GRID_W, (ws + wr) * GRID_W)
        q = q_ref[qs, :]
        s_w = _dot_nt(q, k_ref[ks, :]) * scale + tab_ref[0, kind_of[g]]
        s_c = _dot_nt(q, kc) * scale
        o_ref[qs, :] = _softmax_pv([s_w, s_c], [v_ref[ks, :], vc]).astype(o_ref.dtype)
    s = _dot_nt(qc_ref[...], kc) * scale
    oc_ref[...] = _softmax_pv([s], [vc]).astype(oc_ref.dtype)


def _natten(p, tab, dims, ctx_len):
    nl, t, b = dims
    rows = t // GRID_W
    cb = nl // ctx_len
    hd = NA_HDIM
    lat = lambda off: pl.BlockSpec((t, hd), lambda h, bi: (bi, off // hd + h))
    ctx = lambda off: pl.BlockSpec((ctx_len, hd), lambda h, bi: (cb + bi, off // hd + h))
    return pl.pallas_call(
        functools.partial(_natten_kernel, rows=rows),
        grid=(NA_HEADS, b),
        in_specs=[lat(OFF_BQ), lat(OFF_BK), lat(OFF_BV), ctx(OFF_BQ), ctx(OFF_BK), ctx(OFF_BV),
                  pl.BlockSpec((1,) + tab.shape[1:], lambda h, bi: (h, 0, 0, 0))],
        out_specs=[pl.BlockSpec((t, hd), lambda h, bi: (bi, h)),
                   pl.BlockSpec((ctx_len, hd), lambda h, bi: (bi, h))],
        out_shape=[jax.ShapeDtypeStruct((nl, NA_WIDTH), BF16),
                   jax.ShapeDtypeStruct((b * ctx_len, NA_WIDTH), BF16)],
        compiler_params=_cparams(("parallel", "arbitrary")),
        name="natten",
    )(p, p, p, p, p, p, tab)


def _scan_rows(x, reverse, op, fill):
    n = x.shape[0]
    idx = lax.broadcasted_iota(jnp.int32, x.shape, 0)
    step = 1
    while step < n:
        if reverse:
            x = op(x, jnp.where(idx < n - step, pltpu.roll(x, n - step, 0), fill))
        else:
            x = op(x, jnp.where(idx >= step, pltpu.roll(x, step, 0), fill))
        step *= 2
    return x


def _log_sigmoid(x):
    return jnp.minimum(x, 0.0) - jnp.log(1.0 + jnp.exp(-jnp.abs(x)))


def _rope_swap(x):
    lane = lax.broadcasted_iota(jnp.int32, x.shape, 1)
    return jnp.where(lane % 64 < 32, pltpu.roll(x, LANE - 32, 1), pltpu.roll(x, 32, 1))


def _chunk_rows(c):
    return pl.ds(pl.multiple_of(c * M_CHUNK, M_CHUNK), M_CHUNK)


def _gate_rows(c):
    return pl.ds(pl.multiple_of(c * 8, 8), 4)


def _mlstm_prepare(seq, bm_ref, rope):
    shape = (M_CHUNK, LANE)

    def body(c, carry):
        rows = _chunk_rows(c)
        gc = seq["g"][rows, :] + bm_ref[...]
        lf = _log_sigmoid(gc)
        xs = []
        for d in range(2):
            reverse = d == 1
            b = _scan_rows(lf, reverse, jnp.add, 0.0)
            a_rep = jnp.broadcast_to(b[:, 2 * d + 1:2 * d + 2], shape)
            x_rep = jnp.broadcast_to(gc[:, 2 * d:2 * d + 1], shape) - a_rep
            seq["a"][d, rows, :] = a_rep
            seq["x"][d, rows, :] = x_rep
            seq["mi"][d, rows, :] = a_rep + _scan_rows(x_rep, reverse, jnp.maximum, -jnp.inf)
            seq["row"][pl.ds(c * 8 + 2 * d + 1, 1), :] = jnp.max(x_rep, axis=0, keepdims=True)
            xs.append(x_rep)
        lane = lax.broadcasted_iota(jnp.int32, shape, 1)
        xt = jnp.where(lane < LANE // 2, xs[0], xs[1]).T
        seq["row"][pl.ds(c * 8, 1), :] = xt[0:1, :]
        seq["row"][pl.ds(c * 8 + 2, 1), :] = xt[LANE // 2:LANE // 2 + 1, :]
        q = seq["q"][rows, :].astype(F32)
        k = seq["k"][rows, :].astype(F32)
        if rope is not None:
            cos = rope[0][rows, :]
            sin = rope[1][rows, :]
            q = q * cos + _rope_swap(q) * sin
            k = k * cos + _rope_swap(k) * sin
        seq["qs"][rows, :] = (q * (M_DK ** -0.5)).astype(BF16)
        seq["ks"][rows, :] = k
        return carry

    lax.fori_loop(0, seq["n"], body, 0)


def _mlstm_step(seq, c, reverse, state, order):
    c_ref, m_ref = state
    d = 1 if reverse else 0
    ln = M_CHUNK
    rows = _chunk_rows(c)
    a = seq["a"][d, rows, :]
    x = seq["x"][d, rows, :]
    gate_rows = seq["row"][_gate_rows(c), :]
    x_row, x_max = gate_rows[2 * d:2 * d + 1, :], gate_rows[2 * d + 1:2 * d + 2, :]
    q = seq["qs"][rows, :]
    k = seq["ks"][rows, :]
    vx = jnp.concatenate([seq["v"][rows, :], jnp.ones((ln, LANE), BF16)], axis=1)
    m_prev = m_ref[d]
    c_prev = c_ref[d]

    inter = a + m_prev
    m_row = jnp.maximum(inter, seq["mi"][d, rows, :])
    s = _dot_nt(q, k.astype(BF16)) * jnp.exp(jnp.where(order, a + x_row, -jnp.inf) - m_row)
    ew = jnp.exp(inter - m_row)
    tot = _dot(s.astype(BF16), vx) + jnp.concatenate([ew] * 3, axis=1) * _dot(q, c_prev.astype(BF16))
    inv = 1.0 / jnp.maximum(jnp.abs(tot[:, M_DV:]), jnp.exp(-m_row))
    seq["hb" if reverse else "hf"][rows, :] = tot[:, :M_DV] * jnp.concatenate([inv] * 2, axis=1)

    b_last = a[0:1, :] if reverse else a[ln - 1:ln, :]
    m_new = jnp.maximum(b_last + m_prev, b_last + x_max)
    decay = jnp.exp(b_last + m_prev - m_new)
    kw = k * jnp.exp(b_last + x - m_new)
    c_ref[d] = jnp.concatenate([decay] * 3, axis=1) * c_prev + _dot(kw.T.astype(BF16), vx)
    m_ref[d] = m_new


def _mlstm_scan(seq, state):
    n = seq["n"]
    t_idx = lax.broadcasted_iota(jnp.int32, (M_CHUNK, M_CHUNK), 0)
    s_idx = lax.broadcasted_iota(jnp.int32, (M_CHUNK, M_CHUNK), 1)

    def body(ci, carry):
        _mlstm_step(seq, ci, False, state, t_idx >= s_idx)
        _mlstm_step(seq, n - 1 - ci, True, state, s_idx >= t_idx)
        return carry

    lax.fori_loop(0, n, body, 0, unroll=min(4, n))


def _mlstm_finish(seq, gn_ref):
    def body(c, carry):
        rows = _chunk_rows(c)
        y = _rms(seq["hf"][rows, :] + seq["hb"][rows, :], gn_ref[...])
        seq["y"][rows, :] = (y * _sigmoid(seq["o"][rows, :].astype(F32))).astype(seq["y"].dtype)
        return carry

    lax.fori_loop(0, seq["n"], body, 0)


_SEQ_IN = ("q", "k", "v", "o", "g")
_SEQ_SCRATCH = ("qs", "ks", "a", "x", "mi", "row", "hf", "hb")


def _mlstm_kernel(*refs):
    lat = dict(zip(_SEQ_IN, refs[0:5]))
    ctx = dict(zip(_SEQ_IN, refs[5:10]))
    bm_ref, gn_ref, cos_ref, sin_ref = refs[10:14]
    lat["y"], ctx["y"] = refs[14:16]
    lat.update(zip(_SEQ_SCRATCH, refs[16:24]))
    ctx.update(zip(_SEQ_SCRATCH, refs[24:32]))
    state = refs[32:34]
    for seq in (lat, ctx):
        seq["n"] = seq["q"].shape[0] // M_CHUNK

    _mlstm_prepare(ctx, bm_ref, None)
    _mlstm_prepare(lat, bm_ref, (cos_ref, sin_ref))
    for ref in state:
        ref[...] = jnp.zeros_like(ref)
    _mlstm_scan(ctx, state)
    _mlstm_scan(lat, state)
    _mlstm_finish(ctx, gn_ref)
    _mlstm_finish(lat, gn_ref)


def _rope_tables(t):
    pos = np.arange(t)
    half = M_DK // 4
    inv = ROPE_THETA ** (-np.arange(half, dtype=np.float64) / half)
    ang_r = (pos // GRID_W)[:, None] * inv[None, :]
    ang_c = (pos % GRID_W)[:, None] * inv[None, :]
    ang = np.concatenate([ang_r, ang_r, ang_c, ang_c], axis=1)
    sign = np.tile(np.concatenate([-np.ones(half), np.ones(half)]), 2)[None, :]
    return jnp.asarray(np.cos(ang), F32), jnp.asarray(np.sin(ang) * sign, F32)


def _seq_scratch(n):
    col = pltpu.VMEM((2, n, LANE), F32)
    return [pltpu.VMEM((n, M_DK), BF16), pltpu.VMEM((n, M_DK), F32), col, col, col,
            pltpu.VMEM((n // M_CHUNK * 8, LANE), F32), pltpu.VMEM((n, M_DV), F32), pltpu.VMEM((n, M_DV), F32)]


def _mlstm(p, g, bm, gn, dims, ctx_len):
    nl, t, b = dims
    cb = nl // ctx_len
    cos, sin = _rope_tables(t)
    dk, dv = M_DK, M_DV
    lat = lambda off, w: pl.BlockSpec((t, w), lambda bi, h: (bi, off // w + h))
    ctx = lambda off, w: pl.BlockSpec((ctx_len, w), lambda bi, h: (cb + bi, off // w + h))
    const = lambda shape: pl.BlockSpec(shape, lambda bi, h: (0, 0))
    return pl.pallas_call(
        _mlstm_kernel,
        grid=(b, M_HEADS),
        in_specs=[lat(OFF_CQ, dk), lat(OFF_CK, dk), lat(OFF_CV, dv), lat(OFF_CO, dv),
                  pl.BlockSpec((t, LANE), lambda bi, h: (bi, h)),
                  ctx(OFF_CQ, dk), ctx(OFF_CK, dk), ctx(OFF_CV, dv), ctx(OFF_CO, dv),
                  pl.BlockSpec((ctx_len, LANE), lambda bi, h: (cb + bi, h)),
                  pl.BlockSpec((1, LANE), lambda bi, h: (0, h)),
                  pl.BlockSpec((1, dv), lambda bi, h: (0, h)),
                  const((t, LANE)), const((t, LANE))],
        out_specs=[pl.BlockSpec((t, dv), lambda bi, h: (bi, h)),
                   pl.BlockSpec((ctx_len, dv), lambda bi, h: (bi, h))],
        out_shape=[jax.ShapeDtypeStruct((nl, M_V_W), BF16),
                   jax.ShapeDtypeStruct((b * ctx_len, M_V_W), BF16)],
        scratch_shapes=_seq_scratch(t) + _seq_scratch(ctx_len) + [
            pltpu.VMEM((2, dk, dv + LANE), F32), pltpu.VMEM((2, 1, LANE), F32)],
        compiler_params=_cparams(("parallel", "arbitrary")),
        name="mlstm",
    )(p, p, p, p, g, p, p, p, p, g, bm, gn, cos, sin)


def _merge_kernel(ya_ref, yb_ref, yc_ref, g0_ref, g1_ref, g2_ref, w_ref, o_ref):
    acc = _sigmoid(g0_ref[...].astype(F32)) * _dot(ya_ref[...], w_ref[0])
    acc = acc + _sigmoid(g1_ref[...].astype(F32)) * _dot(yb_ref[...], w_ref[1])
    acc = acc + _sigmoid(g2_ref[...].astype(F32)) * _dot(yc_ref[...], w_ref[2])
    o_ref[...] = acc.astype(o_ref.dtype)


def _merge(ya, yb, yc, p, w_branch, n_rows, tm, tn):
    d = w_branch.shape[2]
    br = lambda: pl.BlockSpec((tm, BRANCH_W), lambda i, j: (i, 0))
    gate = lambda k: pl.BlockSpec((tm, tn), lambda i, j: (i, (OFF_GATE + k * d) // tn + j))
    return pl.pallas_call(
        _merge_kernel,
        grid=(n_rows // tm, d // tn),
        in_specs=[br(), br(), br(), gate(0), gate(1), gate(2),
                  pl.BlockSpec((N_BRANCH, BRANCH_W, tn), lambda i, j: (0, 0, j))],
        out_specs=pl.BlockSpec((tm, tn), lambda i, j: (i, j)),
        out_shape=jax.ShapeDtypeStruct((n_rows, d), BF16),
        compiler_params=_cparams(("parallel", "arbitrary")),
        name="merge",
    )(ya, yb, yc, p, p, p, w_branch)


def _route(scores, sel):
    row = lambda a, e: a[e:e + 1, :]
    best = None
    for gi in range(N_GROUPS):
        a, b, c, d = (row(sel, gi * EXP_PER_GROUP + j) for j in range(EXP_PER_GROUP))
        hi1, lo1 = jnp.maximum(a, b), jnp.minimum(a, b)
        hi2, lo2 = jnp.maximum(c, d), jnp.minimum(c, d)
        gs = jnp.maximum(hi1, hi2) + jnp.maximum(jnp.minimum(hi1, hi2), jnp.maximum(lo1, lo2))
        if best is None:
            best, g_idx = gs, jnp.zeros_like(gs, dtype=jnp.int32)
        else:
            take = gs > best
            best = jnp.where(take, gs, best)
            g_idx = jnp.where(take, gi, g_idx)

    def in_group(a, j):
        out = row(a, j)
        for gi in range(1, N_GROUPS):
            out = jnp.where(g_idx == gi, row(a, gi * EXP_PER_GROUP + j), out)
        return out

    v = [in_group(sel, j) for j in range(EXP_PER_GROUP)]
    u = [in_group(scores, j) for j in range(EXP_PER_GROUP)]

    def argmax_first(vals):
        bv, bi, bu = vals[0], jnp.zeros_like(g_idx), u[0]
        for j in range(1, EXP_PER_GROUP):
            take = vals[j] > bv
            bv = jnp.where(take, vals[j], bv)
            bi = jnp.where(take, j, bi)
            bu = jnp.where(take, u[j], bu)
        return bi, bu

    i1, s1 = argmax_first(v)
    i2, s2 = argmax_first([jnp.where(i1 == j, -jnp.inf, v[j]) for j in range(EXP_PER_GROUP)])
    tot = s1 + s2
    return g_idx * EXP_PER_GROUP + i1, g_idx * EXP_PER_GROUP + i2, s1 / tot, s2 / tot


def _to_tall(ref, val):
    rows = val.shape[0]
    for j in range(ROW_SEG):
        ref[pl.ds(j, rows, stride=ROW_SEG), :] = val[:, j * LANE:(j + 1) * LANE]


def _from_tall(ref):
    rows = ref.shape[0] // ROW_SEG
    return jnp.concatenate([ref[pl.ds(j, rows, stride=ROW_SEG), :] for j in range(ROW_SEG)], axis=1)


def _outproj_kernel(mg_ref, x_ref, gt_ref, sh_ref, sc_ref, g_ref, w_ref, wr_ref, br_ref, xo_ref, h_ref, r_ref):
    y = _dot(mg_ref[...], w_ref[...])
    xn = x_ref[...] + gt_ref[0] * y
    xo_ref[...] = xn
    h = _rms(xn, g_ref[...]) * (1.0 + sc_ref[0]) + sh_ref[0]
    _to_tall(h_ref, h)
    scores = _sigmoid(_dot_nt(wr_ref[...], h.astype(BF16)))
    e1, e2, w1, w2 = _route(scores, scores + br_ref[...])
    r_ref[...] = jnp.zeros_like(r_ref)
    r_ref[0:1, :] = e1.astype(F32)
    r_ref[1:2, :] = e2.astype(F32)
    r_ref[2:3, :] = w1
    r_ref[3:4, :] = w2


def _outproj(merged, x, mod3, g2, w_out, w_rt, b_r, dims, n_rows, tm):
    nl, t, b = dims
    d = x.shape[1]
    grp = functools.partial(_group_of, tm=tm, nl=nl, t=t, b=b)
    mrow = lambda k: pl.BlockSpec((1, 1, d), lambda i: (grp(i) * 6 + k, 0, 0))
    return pl.pallas_call(
        _outproj_kernel,
        grid=(n_rows // tm,),
        in_specs=[pl.BlockSpec((tm, d), lambda i: (i, 0)),
                  pl.BlockSpec((tm, d), lambda i: (i, 0)),
                  mrow(2), mrow(3), mrow(4),
                  pl.BlockSpec((1, d), lambda i: (0, 0)),
                  pl.BlockSpec((d, d), lambda i: (0, 0)),
                  pl.BlockSpec((N_EXPERTS, d), lambda i: (0, 0)),
                  pl.BlockSpec((N_EXPERTS, 1), lambda i: (0, 0))],
        out_specs=[pl.BlockSpec((tm, d), lambda i: (i, 0)),
                   pl.BlockSpec((tm * ROW_SEG, LANE), lambda i: (i, 0)),
                   pl.BlockSpec((8, tm), lambda i: (0, i))],
        out_shape=[jax.ShapeDtypeStruct((n_rows, d), F32),
                   jax.ShapeDtypeStruct((n_rows * ROW_SEG, LANE), F32),
                   jax.ShapeDtypeStruct((8, n_rows), F32)],
        compiler_params=_cparams(("parallel",)),
        name="outproj_route",
    )(merged, x, mod3, mod3, mod3, g2, w_out, w_rt, b_r)


def _slot_plan(route, tm, max_tiles):
    n = route.shape[1]
    e = route[0:2, :].astype(jnp.int32).reshape(-1)
    oh = (e[:, None] == jnp.arange(N_EXPERTS, dtype=jnp.int32)[None, :]).astype(jnp.int32)
    cs = jnp.cumsum(oh, axis=0)
    rank = jnp.sum(cs * oh, axis=1) - 1
    counts = cs[-1]
    padded = ((counts + tm - 1) // tm) * tm
    ends = jnp.cumsum(padded)
    slot = jnp.sum(oh * (ends - padded)[None, :], axis=1) + rank
    tile_ends = ends // tm
    n_used = tile_ends[-1]
    tiles = jnp.arange(max_tiles, dtype=jnp.int32)
    src = jnp.minimum(tiles, n_used - 1)
    tile_e = jnp.minimum(jnp.sum((src[:, None] >= tile_ends[None, :]).astype(jnp.int32), axis=1), N_EXPERTS - 1)
    left = jnp.take(counts, tile_e) - (tiles - jnp.take(tile_ends - padded // tm, tile_e)) * tm
    n_valid = jnp.where(tiles < n_used, jnp.clip(left, 0, tm), 0)
    tok_of_slot, dst_of_slot = _invert_slots(slot.astype(jnp.int32), n, max_tiles * tm)
    wts = route[2:4, :].T
    i32 = lambda a: a.astype(jnp.int32)
    return tok_of_slot, dst_of_slot, i32(tile_e), i32(n_valid), i32(n_used.reshape(1)), wts


def _invert_kernel(slot_ref, tok_ref, dst_ref, *, n):
    def clear(s, carry):
        tok_ref[s] = 0
        dst_ref[s] = 0
        return carry

    def put(f, carry):
        s = slot_ref[f]
        tok_ref[s] = jnp.where(f >= n, f - n, f)
        dst_ref[s] = f
        return carry

    lax.fori_loop(0, tok_ref.shape[0], clear, 0, unroll=8)
    lax.fori_loop(0, 2 * n, put, 0, unroll=8)


def _invert_slots(slot, n, n_slots):
    smem = pl.BlockSpec(memory_space=pltpu.SMEM)
    out = jax.ShapeDtypeStruct((n_slots,), jnp.int32)
    return pl.pallas_call(
        functools.partial(_invert_kernel, n=n),
        in_specs=[smem], out_specs=[smem, smem], out_shape=[out, out],
        name="moe_invert_slots",
    )(slot)


def _row_copy(src_ref, src_row, dst_ref, dst_row, sem):
    src = src_ref.at[pl.ds(pl.multiple_of(src_row * ROW_SEG, ROW_SEG), ROW_SEG), :]
    dst = dst_ref.at[pl.ds(pl.multiple_of(dst_row * ROW_SEG, ROW_SEG), ROW_SEG), :]
    return pltpu.make_async_copy(src, dst, sem)


def _expert_kernel(tok_ref, dst_ref, te_ref, nv_ref, nu_ref, h_ref, wg_ref, wu_ref, wd_ref, y_ref,
                   xbuf, ybuf, gsem, ssem, *, tm):
    del te_ref
    i = pl.program_id(0)
    n_used = nu_ref[0]
    used = i < n_used
    cur = i % 2

    def for_rows(n_rows, body):
        def group(g, carry):
            for u in range(DMA_UNROLL):
                body(g * DMA_UNROLL + u)
            return carry

        def single(j, carry):
            body(j)
            return carry

        n_groups = n_rows // DMA_UNROLL
        lax.fori_loop(0, n_groups, group, 0)
        if not isinstance(n_rows, int) or n_rows % DMA_UNROLL:
            lax.fori_loop(n_groups * DMA_UNROLL, n_rows, single, 0)

    def wait_rows(buf_ref, n_rows, sem):
        part = buf_ref.at[pl.ds(0, n_rows * ROW_SEG), :]
        pltpu.make_async_copy(part, part, sem).wait()

    def gather(tile, buf, wait):
        if wait:
            wait_rows(xbuf.at[buf], tm, gsem.at[buf])
        else:
            for_rows(tm, lambda j: _row_copy(h_ref, tok_ref[tile * tm + j], xbuf.at[buf], j,
                                             gsem.at[buf]).start(priority=1))

    def scatter(tile, buf, wait):
        if wait:
            wait_rows(ybuf.at[buf], nv_ref[tile], ssem.at[buf])
        else:
            for_rows(nv_ref[tile], lambda j: _row_copy(ybuf.at[buf], j, y_ref, dst_ref[tile * tm + j],
                                                       ssem.at[buf]).start())

    @pl.when(i == 0)
    def _():
        gather(0, 0, False)

    @pl.when(i + 1 < n_used)
    def _():
        gather(i + 1, 1 - cur, False)

    @pl.when(used)
    def _():
        gather(i, cur, True)
        x = _from_tall(xbuf.at[cur]).astype(BF16)
        a = _dot(x, wg_ref[0])
        he = a * _sigmoid(a) * _dot(x, wu_ref[0])
        _to_tall(ybuf.at[cur], _dot(he.astype(BF16), wd_ref[0]))

    @pl.when(jnp.logical_and(i >= 1, i - 1 < n_used))
    def _():
        scatter(i - 1, 1 - cur, True)

    @pl.when(used)
    def _():
        scatter(i, cur, False)

    @pl.when(jnp.logical_and(i == pl.num_programs(0) - 1, used))
    def _():
        scatter(i, cur, True)


def _experts(h, tok_of_slot, dst_of_slot, tile_e, n_valid, n_used, wg, wu, wd, tm):
    d, f = wg.shape[1:]
    wspec = lambda shape: pl.BlockSpec(shape, lambda i, t, s, e, v, u: (e[i], 0, 0))
    buf = pltpu.VMEM((2, tm * ROW_SEG, LANE), F32)
    return pl.pallas_call(
        functools.partial(_expert_kernel, tm=tm),
        grid_spec=pltpu.PrefetchScalarGridSpec(
            num_scalar_prefetch=5,
            grid=(tile_e.shape[0],),
            in_specs=[pl.BlockSpec(memory_space=pl.ANY), wspec((1, d, f)), wspec((1, d, f)), wspec((1, f, d))],
            out_specs=pl.BlockSpec(memory_space=pl.ANY),
            scratch_shapes=[buf, buf, pltpu.SemaphoreType.DMA((2,)), pltpu.SemaphoreType.DMA((2,))]),
        out_shape=jax.ShapeDtypeStruct((2 * h.shape[0], LANE), F32),
        compiler_params=_cparams(("arbitrary",)),
        name="moe_experts",
    )(tok_of_slot, dst_of_slot, tile_e, n_valid, n_used, h, wg, wu, wd)


def _combine_kernel(x_ref, w_ref, gt_ref, gf_ref, y1_ref, y2_ref, o_ref, *, final):
    w = w_ref[...]
    f = w[:, 0:1] * _from_tall(y1_ref) + w[:, 1:2] * _from_tall(y2_ref)
    xo = x_ref[...] + gt_ref[0] * f
    if final:
        xo = _rms(xo, gf_ref[...])
    o_ref[...] = xo


def _combine(x, wts, mod3, g_final, y, dims, n_tok, tm, final):
    nl, t, b = dims
    d = x.shape[1]
    grp = functools.partial(_group_of, tm=tm, nl=nl, t=t, b=b)
    return pl.pallas_call(
        functools.partial(_combine_kernel, final=final),
        grid=(n_tok // tm,),
        in_specs=[pl.BlockSpec((tm, d), lambda i: (i, 0)),
                  pl.BlockSpec((tm, 2), lambda i: (i, 0)),
                  pl.BlockSpec((1, 1, d), lambda i: (grp(i) * 6 + 5, 0, 0)),
                  pl.BlockSpec((1, d), lambda i: (0, 0)),
                  pl.BlockSpec((tm * ROW_SEG, LANE), lambda i: (i, 0)),
                  pl.BlockSpec((tm * ROW_SEG, LANE), lambda i: (n_tok // tm + i, 0))],
        out_specs=pl.BlockSpec((tm, d), lambda i: (i, 0)),
        out_shape=jax.ShapeDtypeStruct((n_tok, d), F32),
        compiler_params=_cparams(("parallel",)),
        name="moe_combine",
    )(x, wts, mod3, g_final, y, y)


def _moe(h2, route, xn, mod3, g_final, wg, wu, wd, dims, final):
    n_tok = xn.shape[0]
    max_tiles = (2 * n_tok) // MOE_TM + N_EXPERTS
    tok_of_slot, dst_of_slot, tile_e, n_valid, n_used, wts = _slot_plan(route, MOE_TM, max_tiles)
    y = _experts(h2, tok_of_slot, dst_of_slot, tile_e, n_valid, n_used, wg, wu, wd, MOE_TM)
    nl, t, _ = dims
    tm = 2 * ROW_TM if t % (2 * ROW_TM) == 0 and (n_tok - nl) % (2 * ROW_TM) == 0 else ROW_TM
    return _combine(xn, wts, mod3, g_final, y, dims, n_tok, tm, final)


def _gate_weight(w_cg):
    d = w_cg.shape[0]
    per_head = w_cg.reshape(d, 4, M_HEADS).transpose(0, 2, 1)
    return jnp.pad(per_head, ((0, 0), (0, 0), (0, LANE - 4))).reshape(d, M_HEADS * LANE)


def _layer(x, mod3, dims, ctx_len, last, g1, g2, w_in, g_sgu, w_sp, b_sp, rpb, b_mgate, g_mnorm, w_branch, w_out,
           w_rt, b_r, wg, wu, wd, g_final):
    nl, t, b = dims
    nt, d = x.shape
    fits = lambda rows: t % rows == 0 and (nt - nl) % rows == 0
    tm = 512 if fits(512) else 256
    tm_proj = 1024 if fits(1024) else tm

    w_main = jnp.concatenate([w_in[:, :OFF_CG], w_in[:, OFF_CG + M_NGATE:]], axis=1).astype(BF16)
    w_gate = _gate_weight(w_in[:, OFF_CG:OFF_CG + M_NGATE]).astype(BF16)
    p = _inproj(x, mod3, g1, w_main, dims, tm_proj, 1024, BF16)
    g = _inproj(x, mod3, g1, w_gate, dims, tm_proj, M_HEADS * LANE, F32)

    n_rows = nl if last else nt
    b_full = jnp.repeat(b_sp.T, A_GDIM, axis=1)
    ya = _sgu(p, g_sgu[None, :], w_sp.astype(BF16), b_full, n_rows, tm)

    yb_l, yb_c = _natten(p, _na_bias_table(rpb, t // GRID_W), dims, ctx_len)

    bm = jnp.pad(b_mgate.T, ((0, 0), (0, LANE - 4))).reshape(1, M_HEADS * LANE)
    yc_l, yc_c = _mlstm(p, g, bm, g_mnorm[None, :], dims, ctx_len)

    if last:
        yb, yc = yb_l, yc_l
    else:
        yb = jnp.concatenate([yb_l, yb_c], axis=0)
        yc = jnp.concatenate([yc_l, yc_c], axis=0)

    merged = _merge(ya, yb, yc, p, w_branch.astype(BF16), n_rows, 256, d)
    xn, h2, route = _outproj(merged, x, mod3, g2[None, :], w_out.astype(BF16), w_rt, b_r, dims, n_rows, 256)
    return _moe(h2, route, xn, mod3, g_final, wg.astype(BF16), wu.astype(BF16), wd.astype(BF16), dims, last)


def kernel(x, c, ctx, c_ctx, w_ada, b_ada, g_norm1, g_norm2, w_in, g_sgu, w_spatial, b_spatial, na_rpb, b_mgate,
           g_mnorm, w_branch, w_out, w_router, b_router, w_e_gate, w_e_up, w_e_down, g_final):
    b, t, d = x.shape
    ctx_len = ctx.shape[1]
    nl = b * t
    dims = (nl, t, b)
    depth = w_ada.shape[0]
    xs = jnp.concatenate([x.reshape(nl, d), ctx.reshape(b * ctx_len, d)], axis=0)
    n_mod = -(-(b + 1) // 8) * 8
    cc = jnp.zeros((n_mod, d), F32).at[:b].set(c).at[b].set(c_ctx)
    w_rt = w_router.T.astype(BF16)
    b_r = b_router[:, None].astype(F32)
    for layer in range(depth):
        mod = _ada(cc, w_ada[layer], b_ada[layer][None, :])
        mod3 = mod.reshape(n_mod * 6, 1, d)
        xs = _layer(xs, mod3, dims, ctx_len, layer == depth - 1, g_norm1[layer][None, :], g_norm2[layer],
                    w_in[layer], g_sgu[layer], w_spatial[layer], b_spatial[layer], na_rpb[layer], b_mgate[layer],
                    g_mnorm[layer], w_branch[layer], w_out[layer], w_rt, b_r, w_e_gate[layer], w_e_up[layer],
                    w_e_down[layer], g_final[None, :])
    return xs.reshape(b, t, d)
```

```python
import functools

import numpy as np
import jax
import jax.numpy as jnp
from jax import lax
from jax.experimental import pallas as pl
from jax.experimental.pallas import tpu as pltpu

F32 = jnp.float32
BF16 = jnp.bfloat16

GRID_W = 64
EPS = 1e-6
NEG_INF = -1e30
ROPE_THETA = 10000.0

A_WIDTH = 1024
A_CHUNK = 128
A_GROUPS = 8
A_GDIM = A_WIDTH // A_GROUPS

NA_HEADS = 8
NA_HDIM = 128
NA_WIDTH = NA_HEADS * NA_HDIM
NA_KH = 8
NA_KW = 16

M_HEADS = 4
M_DK = 128
M_DV = 256
M_QK_W = M_HEADS * M_DK
M_V_W = M_HEADS * M_DV
M_CHUNK = 128
M_NGATE = 4 * M_HEADS

N_BRANCH = 3
BRANCH_W = 1024

N_EXPERTS = 16
N_GROUPS = 4
EXP_PER_GROUP = N_EXPERTS // N_GROUPS
D_FF_EXPERT = 1024

OFF_AU = 0
OFF_AV = OFF_AU + A_WIDTH
OFF_BQ = OFF_AV + A_WIDTH
OFF_BK = OFF_BQ + NA_WIDTH
OFF_BV = OFF_BK + NA_WIDTH
OFF_CQ = OFF_BV + NA_WIDTH
OFF_CK = OFF_CQ + M_QK_W
OFF_CV = OFF_CK + M_QK_W
OFF_CO = OFF_CV + M_V_W
OFF_CG = OFF_CO + M_V_W
OFF_GATE = OFF_CO + M_V_W

LANE = 128
VMEM_LIMIT = 56 * 1024 * 1024

ROW_SEG = 16
MOE_TM = 256
ROW_TM = 256
DMA_UNROLL = 8


def _cparams(sem):
    return pltpu.CompilerParams(dimension_semantics=sem, vmem_limit_bytes=VMEM_LIMIT)


def _dot(a, b):
    return jnp.dot(a, b, preferred_element_type=F32)


def _dot_nt(a, b):
    return lax.dot_general(a, b, (((1,), (1,)), ((), ())), preferred_element_type=F32)


def _sigmoid(x):
    return 1.0 / (1.0 + jnp.exp(-x))


def _gelu_tanh(x):
    return 0.5 * x * (1.0 + jnp.tanh(np.sqrt(2.0 / np.pi).astype(np.float32) * (x + 0.044715 * (x * x * x))))


def _rms(x, g):
    return x * lax.rsqrt(jnp.mean(x * x, axis=-1, keepdims=True) + EPS) * g


def _ada_kernel(c_ref, w_ref, b_ref, o_ref):
    cc = c_ref[...]
    s = cc * _sigmoid(cc)
    o_ref[...] = _dot(s.astype(BF16), w_ref[...].astype(BF16)) + b_ref[...]


def _ada(cc, w, b):
    rows, d = cc.shape
    n = w.shape[1]
    tn = 1024
    return pl.pallas_call(
        _ada_kernel,
        grid=(n // tn,),
        in_specs=[pl.BlockSpec((rows, d), lambda j: (0, 0)),
                  pl.BlockSpec((d, tn), lambda j: (0, j)),
                  pl.BlockSpec((1, tn), lambda j: (0, j))],
        out_specs=pl.BlockSpec((rows, tn), lambda j: (0, j)),
        out_shape=jax.ShapeDtypeStruct((rows, n), F32),
        compiler_params=_cparams(("arbitrary",)),
        name="ada_mod",
    )(cc, w, b)


def _inproj_kernel(x_ref, sh_ref, sc_ref, g_ref, w_ref, o_ref, h_ref):
    @pl.when(pl.program_id(1) == 0)
    def _():
        h = _rms(x_ref[...], g_ref[...]) * (1.0 + sc_ref[0]) + sh_ref[0]
        h_ref[...] = h.astype(BF16)

    o_ref[...] = _dot(h_ref[...], w_ref[...]).astype(o_ref.dtype)


def _group_of(i, tm, nl, t, b):
    return jnp.where(i * tm < nl, (i * tm) // t, b)


def _inproj(x, mod3, g, w, dims, tm, tn, out_dtype):
    nl, t, b = dims
    nt, d = x.shape
    n = w.shape[1]
    grp = functools.partial(_group_of, tm=tm, nl=nl, t=t, b=b)
    return pl.pallas_call(
        _inproj_kernel,
        grid=(nt // tm, n // tn),
        in_specs=[pl.BlockSpec((tm, d), lambda i, j: (i, 0)),
                  pl.BlockSpec((1, 1, d), lambda i, j: (grp(i) * 6 + 0, 0, 0)),
                  pl.BlockSpec((1, 1, d), lambda i, j: (grp(i) * 6 + 1, 0, 0)),
                  pl.BlockSpec((1, d), lambda i, j: (0, 0)),
                  pl.BlockSpec((d, tn), lambda i, j: (0, j))],
        out_specs=pl.BlockSpec((tm, tn), lambda i, j: (i, j)),
        out_shape=jax.ShapeDtypeStruct((nt, n), out_dtype),
        scratch_shapes=[pltpu.VMEM((tm, d), BF16)],
        compiler_params=_cparams(("parallel", "arbitrary")),
        name="inproj",
    )(x, mod3, mod3, g, w)


def _sgu_kernel(u_ref, v_ref, g_ref, w_ref, b_ref, o_ref):
    n_chunk = u_ref.shape[0] // A_CHUNK
    for c in range(n_chunk):
        rows = slice(c * A_CHUNK, (c + 1) * A_CHUNK)
        u = _gelu_tanh(u_ref[rows, :].astype(F32))
        v = _rms(_gelu_tanh(v_ref[rows, :].astype(F32)), g_ref[...]).astype(BF16)
        for gi in range(A_GROUPS):
            cols = slice(gi * A_GDIM, (gi + 1) * A_GDIM)
            mixed = _dot(w_ref[gi], v[:, cols]) + b_ref[:, cols]
            o_ref[rows, cols] = (u[:, cols] * mixed).astype(o_ref.dtype)


def _sgu(p, g_sgu, w_sp, b_full, n_rows, tm):
    return pl.pallas_call(
        _sgu_kernel,
        grid=(n_rows // tm,),
        in_specs=[pl.BlockSpec((tm, A_WIDTH), lambda i: (i, OFF_AU // A_WIDTH)),
                  pl.BlockSpec((tm, A_WIDTH), lambda i: (i, OFF_AV // A_WIDTH)),
                  pl.BlockSpec((1, A_WIDTH), lambda i: (0, 0)),
                  pl.BlockSpec((A_GROUPS, A_CHUNK, A_CHUNK), lambda i: (0, 0, 0)),
                  pl.BlockSpec((A_CHUNK, A_WIDTH), lambda i: (0, 0))],
        out_specs=pl.BlockSpec((tm, A_WIDTH), lambda i: (i, 0)),
        out_shape=jax.ShapeDtypeStruct((n_rows, A_WIDTH), BF16),
        compiler_params=_cparams(("parallel",)),
        name="sgu",
    )(p, p, g_sgu, w_sp, b_full)


NA_GQ = 4


def _na_geometry(rows):
    kh = min(NA_KH, rows)
    gq = min(NA_GQ, rows)
    wr = min(gq + kh - 1, rows)
    starts = [int(np.clip(gq * g - kh // 2, 0, rows - wr)) for g in range(rows // gq)]
    return kh, gq, wr, starts


def _na_row_select(rows):
    kh, gq, wr, starts = _na_geometry(rows)
    n_rel = 2 * NA_KH - 1
    row_sel = np.zeros((len(starts), gq, wr, n_rel + 1), np.float32)
    for g, ws in enumerate(starts):
        for qi in range(gq):
            r = gq * g + qi
            rs = int(np.clip(r - kh // 2, 0, rows - kh))
            assert ws <= rs and rs + kh <= ws + wr
            for a in range(wr):
                inside = rs <= ws + a < rs + kh
                row_sel[g, qi, a, ws + a - r + NA_KH - 1 if inside else n_rel] = 1.0
    kinds, kind_of = [], []
    for g in range(len(starts)):
        same = [k for k, rep in enumerate(kinds) if np.array_equal(row_sel[rep], row_sel[g])]
        if not same:
            kinds.append(g)
        kind_of.append(same[0] if same else len(kinds) - 1)
    return row_sel[kinds], kind_of


def _na_bias_table(rpb, rows):
    kh, gq, wr, starts = _na_geometry(rows)
    n_rel = 2 * NA_KH - 1
    cq = np.arange(GRID_W)[:, None]
    kc = np.arange(GRID_W)[None, :]
    cs = np.clip(cq - NA_KW // 2, 0, GRID_W - NA_KW)
    valid = (kc >= cs) & (kc < cs + NA_KW)
    coff = np.clip(kc - cq, -(NA_KW - 1), NA_KW - 1) + NA_KW - 1
    col_sel = (coff[None, :, :] == np.arange(2 * NA_KW - 1)[:, None, None]).astype(np.float32)
    by_col = jnp.einsum('hrc,cqk->hrqk', rpb.astype(F32), col_sel, precision=lax.Precision.HIGHEST)
    by_col = jnp.where(valid[None, None], by_col, NEG_INF)
    by_col = jnp.concatenate([by_col, jnp.full_like(by_col[:, :1], NEG_INF)], axis=1)
    row_sel, _ = _na_row_select(rows)
    tab = jnp.einsum('gqar,hrck->hgqcak', row_sel, by_col, precision=lax.Precision.HIGHEST)
    return tab.reshape(rpb.shape[0], row_sel.shape[0], gq * GRID_W, wr * GRID_W)


def _softmax_pv(s_parts, v_parts):
    m = functools.reduce(jnp.maximum, [jnp.max(s, axis=-1, keepdims=True) for s in s_parts])
    p_parts = [jnp.exp(s - m) for s in s_parts]
    l = functools.reduce(jnp.add, [jnp.sum(p, axis=-1, keepdims=True) for p in p_parts])
    o = functools.reduce(jnp.add, [_dot(p.astype(BF16), v) for p, v in zip(p_parts, v_parts)])
    return o / l


def _natten_kernel(q_ref, k_ref, v_ref, qc_ref, kc_ref, vc_ref, tab_ref, o_ref, oc_ref, *, rows):
    scale = NA_HDIM ** -0.5
    _, gq, wr, starts = _na_geometry(rows)
    _, kind_of = _na_row_select(rows)
    kc = kc_ref[...]
    vc = vc_ref[...]
    for g, ws in enumerate(starts):
        qs = slice(g * gq * GRID_W, (g + 1) * gq * GRID_W)
        ks = slice(ws * GRID_W, (ws + wr) * GRID_W)
        q = q_ref[qs, :]
        s_w = _dot_nt(q, k_ref[ks, :]) * scale + tab_ref[0, kind_of[g]]
        s_c = _dot_nt(q, kc) * scale
        o_ref[qs, :] = _softmax_pv([s_w, s_c], [v_ref[ks, :], vc]).astype(o_ref.dtype)
    s = _dot_nt(qc_ref[...], kc) * scale
    oc_ref[...] = _softmax_pv([s], [vc]).astype(oc_ref.dtype)


def _natten(p, tab, dims, ctx_len):
    nl, t, b = dims
    rows = t // GRID_W
    cb = nl // ctx_len
    hd = NA_HDIM
    lat = lambda off: pl.BlockSpec((t, hd), lambda h, bi: (bi, off // hd + h))
    ctx = lambda off: pl.BlockSpec((ctx_len, hd), lambda h, bi: (cb + bi, off // hd + h))
    return pl.pallas_call(
        functools.partial(_natten_kernel, rows=rows),
        grid=(NA_HEADS, b),
        in_specs=[lat(OFF_BQ), lat(OFF_BK), lat(OFF_BV), ctx(OFF_BQ), ctx(OFF_BK), ctx(OFF_BV),
                  pl.BlockSpec((1,) + tab.shape[1:], lambda h, bi: (h, 0, 0, 0))],
        out_specs=[pl.BlockSpec((t, hd), lambda h, bi: (bi, h)),
                   pl.BlockSpec((ctx_len, hd), lambda h, bi: (bi, h))],
        out_shape=[jax.ShapeDtypeStruct((nl, NA_WIDTH), BF16),
                   jax.ShapeDtypeStruct((b * ctx_len, NA_WIDTH), BF16)],
        compiler_params=_cparams(("parallel", "arbitrary")),
        name="natten",
    )(p, p, p, p, p, p, tab)


def _scan_rows(x, reverse, op, fill):
    n = x.shape[0]
    idx = lax.broadcasted_iota(jnp.int32, x.shape, 0)
    step = 1
    while step < n:
        if reverse:
            x = op(x, jnp.where(idx < n - step, pltpu.roll(x, n - step, 0), fill))
        else:
            x = op(x, jnp.where(idx >= step, pltpu.roll(x, step, 0), fill))
        step *= 2
    return x


def _log_sigmoid(x):
    return jnp.minimum(x, 0.0) - jnp.log(1.0 + jnp.exp(-jnp.abs(x)))


def _rope_swap(x):
    lane = lax.broadcasted_iota(jnp.int32, x.shape, 1)
    return jnp.where(lane % 64 < 32, pltpu.roll(x, LANE - 32, 1), pltpu.roll(x, 32, 1))


def _chunk_rows(c):
    return pl.ds(pl.multiple_of(c * M_CHUNK, M_CHUNK), M_CHUNK)


def _gate_rows(c):
    return pl.ds(pl.multiple_of(c * 8, 8), 4)


def _mlstm_prepare(seq, bm_ref, rope):
    shape = (M_CHUNK, LANE)

    def body(c, carry):
        rows = _chunk_rows(c)
        gc = seq["g"][rows, :] + bm_ref[...]
        lf = _log_sigmoid(gc)
        xs = []
        for d in range(2):
            reverse = d == 1
            b = _scan_rows(lf, reverse, jnp.add, 0.0)
            a_rep = jnp.broadcast_to(b[:, 2 * d + 1:2 * d + 2], shape)
            x_rep = jnp.broadcast_to(gc[:, 2 * d:2 * d + 1], shape) - a_rep
            seq["a"][d, rows, :] = a_rep
            seq["x"][d, rows, :] = x_rep
            seq["mi"][d, rows, :] = a_rep + _scan_rows(x_rep, reverse, jnp.maximum, -jnp.inf)
            seq["row"][pl.ds(c * 8 + 2 * d + 1, 1), :] = jnp.max(x_rep, axis=0, keepdims=True)
            xs.append(x_rep)
        lane = lax.broadcasted_iota(jnp.int32, shape, 1)
        xt = jnp.where(lane < LANE // 2, xs[0], xs[1]).T
        seq["row"][pl.ds(c * 8, 1), :] = xt[0:1, :]
        seq["row"][pl.ds(c * 8 + 2, 1), :] = xt[LANE // 2:LANE // 2 + 1, :]
        q = seq["q"][rows, :].astype(F32)
        k = seq["k"][rows, :].astype(F32)
        if rope is not None:
            cos = rope[0][rows, :]
            sin = rope[1][rows, :]
            q = q * cos + _rope_swap(q) * sin
            k = k * cos + _rope_swap(k) * sin
        seq["qs"][rows, :] = (q * (M_DK ** -0.5)).astype(BF16)
        seq["ks"][rows, :] = k
        return carry

    lax.fori_loop(0, seq["n"], body, 0)


def _mlstm_step(seq, c, reverse, state, order):
    c_ref, m_ref = state
    d = 1 if reverse else 0
    ln = M_CHUNK
    rows = _chunk_rows(c)
    a = seq["a"][d, rows, :]
    x = seq["x"][d, rows, :]
    gate_rows = seq["row"][_gate_rows(c), :]
    x_row, x_max = gate_rows[2 * d:2 * d + 1, :], gate_rows[2 * d + 1:2 * d + 2, :]
    q = seq["qs"][rows, :]
    k = seq["ks"][rows, :]
    vx = jnp.concatenate([seq["v"][rows, :], jnp.ones((ln, LANE), BF16)], axis=1)
    m_prev = m_ref[d]
    c_prev = c_ref[d]

    inter = a + m_prev
    m_row = jnp.maximum(inter, seq["mi"][d, rows, :])
    s = _dot_nt(q, k.astype(BF16)) * jnp.exp(jnp.where(order, a + x_row, -jnp.inf) - m_row)
    ew = jnp.exp(inter - m_row)
    tot = _dot(s.astype(BF16), vx) + jnp.concatenate([ew] * 3, axis=1) * _dot(q, c_prev.astype(BF16))
    inv = 1.0 / jnp.maximum(jnp.abs(tot[:, M_DV:]), jnp.exp(-m_row))
    seq["hb" if reverse else "hf"][rows, :] = tot[:, :M_DV] * jnp.concatenate([inv] * 2, axis=1)

    b_last = a[0:1, :] if reverse else a[ln - 1:ln, :]
    m_new = jnp.maximum(b_last + m_prev, b_last + x_max)
    decay = jnp.exp(b_last + m_prev - m_new)
    kw = k * jnp.exp(b_last + x - m_new)
    c_ref[d] = jnp.concatenate([decay] * 3, axis=1) * c_prev + _dot(kw.T.astype(BF16), vx)
    m_ref[d] = m_new


def _mlstm_scan(seq, state):
    n = seq["n"]
    t_idx = lax.broadcasted_iota(jnp.int32, (M_CHUNK, M_CHUNK), 0)
    s_idx = lax.broadcasted_iota(jnp.int32, (M_CHUNK, M_CHUNK), 1)

    def body(ci, carry):
        _mlstm_step(seq, ci, False, state, t_idx >= s_idx)
        _mlstm_step(seq, n - 1 - ci, True, state, s_idx >= t_idx)
        return carry

    lax.fori_loop(0, n, body, 0, unroll=min(4, n))


def _mlstm_finish(seq, gn_ref):
    def body(c, carry):
        rows = _chunk_rows(c)
        y = _rms(seq["hf"][rows, :] + seq["hb"][rows, :], gn_ref[...])
        seq["y"][rows, :] = (y * _sigmoid(seq["o"][rows, :].astype(F32))).astype(seq["y"].dtype)
        return carry

    lax.fori_loop(0, seq["n"], body, 0)


_SEQ_IN = ("q", "k", "v", "o", "g")
_SEQ_SCRATCH = ("qs", "ks", "a", "x", "mi", "row", "hf", "hb")


def _mlstm_kernel(*refs):
    lat = dict(zip(_SEQ_IN, refs[0:5]))
    ctx = dict(zip(_SEQ_IN, refs[5:10]))
    bm_ref, gn_ref, cos_ref, sin_ref = refs[10:14]
    lat["y"], ctx["y"] = refs[14:16]
    lat.update(zip(_SEQ_SCRATCH, refs[16:24]))
    ctx.update(zip(_SEQ_SCRATCH, refs[24:32]))
    state = refs[32:34]
    for seq in (lat, ctx):
        seq["n"] = seq["q"].shape[0] // M_CHUNK

    _mlstm_prepare(ctx, bm_ref, None)
    _mlstm_prepare(lat, bm_ref, (cos_ref, sin_ref))
    for ref in state:
        ref[...] = jnp.zeros_like(ref)
    _mlstm_scan(ctx, state)
    _mlstm_scan(lat, state)
    _mlstm_finish(ctx, gn_ref)
    _mlstm_finish(lat, gn_ref)


def _rope_tables(t):
    pos = np.arange(t)
    half = M_DK // 4
    inv = ROPE_THETA ** (-np.arange(half, dtype=np.float64) / half)
    ang_r = (pos // GRID_W)[:, None] * inv[None, :]
    ang_c = (pos % GRID_W)[:, None] * inv[None, :]
    ang = np.concatenate([ang_r, ang_r, ang_c, ang_c], axis=1)
    sign = np.tile(np.concatenate([-np.ones(half), np.ones(half)]), 2)[None, :]
    return jnp.asarray(np.cos(ang), F32), jnp.asarray(np.sin(ang) * sign, F32)


def _seq_scratch(n):
    col = pltpu.VMEM((2, n, LANE), F32)
    return [pltpu.VMEM((n, M_DK), BF16), pltpu.VMEM((n, M_DK), F32), col, col, col,
            pltpu.VMEM((n // M_CHUNK * 8, LANE), F32), pltpu.VMEM((n, M_DV), F32), pltpu.VMEM((n, M_DV), F32)]


def _mlstm(p, g, bm, gn, dims, ctx_len):
    nl, t, b = dims
    cb = nl // ctx_len
    cos, sin = _rope_tables(t)
    dk, dv = M_DK, M_DV
    lat = lambda off, w: pl.BlockSpec((t, w), lambda bi, h: (bi, off // w + h))
    ctx = lambda off, w: pl.BlockSpec((ctx_len, w), lambda bi, h: (cb + bi, off // w + h))
    const = lambda shape: pl.BlockSpec(shape, lambda bi, h: (0, 0))
    return pl.pallas_call(
        _mlstm_kernel,
        grid=(b, M_HEADS),
        in_specs=[lat(OFF_CQ, dk), lat(OFF_CK, dk), lat(OFF_CV, dv), lat(OFF_CO, dv),
                  pl.BlockSpec((t, LANE), lambda bi, h: (bi, h)),
                  ctx(OFF_CQ, dk), ctx(OFF_CK, dk), ctx(OFF_CV, dv), ctx(OFF_CO, dv),
                  pl.BlockSpec((ctx_len, LANE), lambda bi, h: (cb + bi, h)),
                  pl.BlockSpec((1, LANE), lambda bi, h: (0, h)),
                  pl.BlockSpec((1, dv), lambda bi, h: (0, h)),
                  const((t, LANE)), const((t, LANE))],
        out_specs=[pl.BlockSpec((t, dv), lambda bi, h: (bi, h)),
                   pl.BlockSpec((ctx_len, dv), lambda bi, h: (bi, h))],
        out_shape=[jax.ShapeDtypeStruct((nl, M_V_W), BF16),
                   jax.ShapeDtypeStruct((b * ctx_len, M_V_W), BF16)],
        scratch_shapes=_seq_scratch(t) + _seq_scratch(ctx_len) + [
            pltpu.VMEM((2, dk, dv + LANE), F32), pltpu.VMEM((2, 1, LANE), F32)],
        compiler_params=_cparams(("parallel", "arbitrary")),
        name="mlstm",
    )(p, p, p, p, g, p, p, p, p, g, bm, gn, cos, sin)


def _merge_kernel(ya_ref, yb_ref, yc_ref, g0_ref, g1_ref, g2_ref, w_ref, o_ref):
    acc = _sigmoid(g0_ref[...].astype(F32)) * _dot(ya_ref[...], w_ref[0])
    acc = acc + _sigmoid(g1_ref[...].astype(F32)) * _dot(yb_ref[...], w_ref[1])
    acc = acc + _sigmoid(g2_ref[...].astype(F32)) * _dot(yc_ref[...], w_ref[2])
    o_ref[...] = acc.astype(o_ref.dtype)


def _merge(ya, yb, yc, p, w_branch, n_rows, tm, tn):
    d = w_branch.shape[2]
    br = lambda: pl.BlockSpec((tm, BRANCH_W), lambda i, j: (i, 0))
    gate = lambda k: pl.BlockSpec((tm, tn), lambda i, j: (i, (OFF_GATE + k * d) // tn + j))
    return pl.pallas_call(
        _merge_kernel,
        grid=(n_rows // tm, d // tn),
        in_specs=[br(), br(), br(), gate(0), gate(1), gate(2),
                  pl.BlockSpec((N_BRANCH, BRANCH_W, tn), lambda i, j: (0, 0, j),
                               pipeline_mode=pl.Buffered(1) if tn == d else None)],
        out_specs=pl.BlockSpec((tm, tn), lambda i, j: (i, j)),
        out_shape=jax.ShapeDtypeStruct((n_rows, d), BF16),
        compiler_params=_cparams(("parallel", "arbitrary")),
        name="merge",
    )(ya, yb, yc, p, p, p, w_branch)


def _route(scores, sel):
    row = lambda a, e: a[e:e + 1, :]
    best = None
    for gi in range(N_GROUPS):
        a, b, c, d = (row(sel, gi * EXP_PER_GROUP + j) for j in range(EXP_PER_GROUP))
        hi1, lo1 = jnp.maximum(a, b), jnp.minimum(a, b)
        hi2, lo2 = jnp.maximum(c, d), jnp.minimum(c, d)
        gs = jnp.maximum(hi1, hi2) + jnp.maximum(jnp.minimum(hi1, hi2), jnp.maximum(lo1, lo2))
        if best is None:
            best, g_idx = gs, jnp.zeros_like(gs, dtype=jnp.int32)
        else:
            take = gs > best
            best = jnp.where(take, gs, best)
            g_idx = jnp.where(take, gi, g_idx)

    def in_group(a, j):
        out = row(a, j)
        for gi in range(1, N_GROUPS):
            out = jnp.where(g_idx == gi, row(a, gi * EXP_PER_GROUP + j), out)
        return out

    v = [in_group(sel, j) for j in range(EXP_PER_GROUP)]
    u = [in_group(scores, j) for j in range(EXP_PER_GROUP)]

    def argmax_first(vals):
        bv, bi, bu = vals[0], jnp.zeros_like(g_idx), u[0]
        for j in range(1, EXP_PER_GROUP):
            take = vals[j] > bv
            bv = jnp.where(take, vals[j], bv)
            bi = jnp.where(take, j, bi)
            bu = jnp.where(take, u[j], bu)
        return bi, bu

    i1, s1 = argmax_first(v)
    i2, s2 = argmax_first([jnp.where(i1 == j, -jnp.inf, v[j]) for j in range(EXP_PER_GROUP)])
    tot = s1 + s2
    return g_idx * EXP_PER_GROUP + i1, g_idx * EXP_PER_GROUP + i2, s1 / tot, s2 / tot


def _to_tall(ref, val):
    rows = val.shape[0]
    for j in range(ROW_SEG):
        ref[pl.ds(j, rows, stride=ROW_SEG), :] = val[:, j * LANE:(j + 1) * LANE]


def _from_tall(ref):
    rows = ref.shape[0] // ROW_SEG
    return jnp.concatenate([ref[pl.ds(j, rows, stride=ROW_SEG), :] for j in range(ROW_SEG)], axis=1)


def _outproj_kernel(mg_ref, x_ref, gt_ref, sh_ref, sc_ref, g_ref, w_ref, wr_ref, br_ref, xo_ref, h_ref, r_ref):
    y = _dot(mg_ref[...], w_ref[...])
    xn = x_ref[...] + gt_ref[0] * y
    xo_ref[...] = xn
    h = _rms(xn, g_ref[...]) * (1.0 + sc_ref[0]) + sh_ref[0]
    _to_tall(h_ref, h)
    scores = _sigmoid(_dot_nt(wr_ref[...], h.astype(BF16)))
    e1, e2, w1, w2 = _route(scores, scores + br_ref[...])
    r_ref[...] = jnp.zeros_like(r_ref)
    r_ref[0:1, :] = e1.astype(F32)
    r_ref[1:2, :] = e2.astype(F32)
    r_ref[2:3, :] = w1
    r_ref[3:4, :] = w2


def _outproj(merged, x, mod3, g2, w_out, w_rt, b_r, dims, n_rows, tm):
    nl, t, b = dims
    d = x.shape[1]
    grp = functools.partial(_group_of, tm=tm, nl=nl, t=t, b=b)
    mrow = lambda k: pl.BlockSpec((1, 1, d), lambda i: (grp(i) * 6 + k, 0, 0))
    return pl.pallas_call(
        _outproj_kernel,
        grid=(n_rows // tm,),
        in_specs=[pl.BlockSpec((tm, d), lambda i: (i, 0)),
                  pl.BlockSpec((tm, d), lambda i: (i, 0)),
                  mrow(2), mrow(3), mrow(4),
                  pl.BlockSpec((1, d), lambda i: (0, 0)),
                  pl.BlockSpec((d, d), lambda i: (0, 0), pipeline_mode=pl.Buffered(1)),
                  pl.BlockSpec((N_EXPERTS, d), lambda i: (0, 0)),
                  pl.BlockSpec((N_EXPERTS, 1), lambda i: (0, 0))],
        out_specs=[pl.BlockSpec((tm, d), lambda i: (i, 0)),
                   pl.BlockSpec((tm * ROW_SEG, LANE), lambda i: (i, 0)),
                   pl.BlockSpec((8, tm), lambda i: (0, i))],
        out_shape=[jax.ShapeDtypeStruct((n_rows, d), F32),
                   jax.ShapeDtypeStruct((n_rows * ROW_SEG, LANE), F32),
                   jax.ShapeDtypeStruct((8, n_rows), F32)],
        compiler_params=_cparams(("parallel",)),
        name="outproj_route",
    )(merged, x, mod3, mod3, mod3, g2, w_out, w_rt, b_r)


def _slot_plan(route, tm, max_tiles):
    n = route.shape[1]
    e = route[0:2, :].astype(jnp.int32).reshape(-1)
    oh = (e[:, None] == jnp.arange(N_EXPERTS, dtype=jnp.int32)[None, :]).astype(jnp.int32)
    cs = jnp.cumsum(oh, axis=0)
    rank = jnp.sum(cs * oh, axis=1) - 1
    counts = cs[-1]
    padded = ((counts + tm - 1) // tm) * tm
    ends = jnp.cumsum(padded)
    slot = jnp.sum(oh * (ends - padded)[None, :], axis=1) + rank
    tile_ends = ends // tm
    n_used = tile_ends[-1]
    tiles = jnp.arange(max_tiles, dtype=jnp.int32)
    src = jnp.minimum(tiles, n_used - 1)
    tile_e = jnp.minimum(jnp.sum((src[:, None] >= tile_ends[None, :]).astype(jnp.int32), axis=1), N_EXPERTS - 1)
    left = jnp.take(counts, tile_e) - (tiles - jnp.take(tile_ends - padded // tm, tile_e)) * tm
    n_valid = jnp.where(tiles < n_used, jnp.clip(left, 0, tm), 0)
    tok_of_slot, dst_of_slot = _invert_slots(slot.astype(jnp.int32), n, max_tiles * tm)
    wts = route[2:4, :].T
    i32 = lambda a: a.astype(jnp.int32)
    return tok_of_slot, dst_of_slot, i32(tile_e), i32(n_valid), i32(n_used.reshape(1)), wts


def _invert_kernel(groups_ref, slot_ref, tok_ref, dst_ref, *, n):
    def clear(g, carry):
        for u in range(DMA_UNROLL):
            tok_ref[g * DMA_UNROLL + u] = 0
            dst_ref[g * DMA_UNROLL + u] = 0
        return carry

    def put(g, carry):
        for u in range(DMA_UNROLL):
            f = g * DMA_UNROLL + u
            s = slot_ref[f]
            tok_ref[s] = jnp.where(f >= n, f - n, f)
            dst_ref[s] = f
        return carry

    lax.fori_loop(0, groups_ref[0], clear, 0)
    lax.fori_loop(0, groups_ref[1], put, 0)


def _invert_slots(slot, n, n_slots):
    assert n_slots % DMA_UNROLL == 0 and (2 * n) % DMA_UNROLL == 0
    smem = pl.BlockSpec(memory_space=pltpu.SMEM)
    out = jax.ShapeDtypeStruct((n_slots,), jnp.int32)
    groups = jnp.array([n_slots // DMA_UNROLL, 2 * n // DMA_UNROLL], jnp.int32)
    return pl.pallas_call(
        functools.partial(_invert_kernel, n=n),
        in_specs=[smem, smem], out_specs=[smem, smem], out_shape=[out, out],
        name="moe_invert_slots",
    )(groups, slot)


def _row_copy(src_ref, src_row, dst_ref, dst_row, sem):
    src = src_ref.at[pl.ds(pl.multiple_of(src_row * ROW_SEG, ROW_SEG), ROW_SEG), :]
    dst = dst_ref.at[pl.ds(pl.multiple_of(dst_row * ROW_SEG, ROW_SEG), ROW_SEG), :]
    return pltpu.make_async_copy(src, dst, sem)


def _expert_kernel(tok_ref, dst_ref, te_ref, nv_ref, nu_ref, h_ref, wg_ref, wu_ref, wd_ref, y_ref,
                   xbuf, ybuf, gsem, ssem, *, tm):
    del te_ref
    i = pl.program_id(0)
    n_used = nu_ref[0]
    used = i < n_used
    cur = i % 2

    def for_rows(n_rows, body):
        def group(g, carry):
            for u in range(DMA_UNROLL):
                body(g * DMA_UNROLL + u)
            return carry

        def single(j, carry):
            body(j)
            return carry

        n_groups = n_rows // DMA_UNROLL
        lax.fori_loop(0, n_groups, group, 0)
        if not isinstance(n_rows, int) or n_rows % DMA_UNROLL:
            lax.fori_loop(n_groups * DMA_UNROLL, n_rows, single, 0)

    def wait_rows(buf_ref, n_rows, sem):
        part = buf_ref.at[pl.ds(0, n_rows * ROW_SEG), :]
        pltpu.make_async_copy(part, part, sem).wait()

    def gather(tile, buf, wait):
        if wait:
            wait_rows(xbuf.at[buf], tm, gsem.at[buf])
        else:
            for_rows(tm, lambda j: _row_copy(h_ref, tok_ref[tile * tm + j], xbuf.at[buf], j,
                                             gsem.at[buf]).start(priority=1))

    def scatter(tile, buf, wait):
        if wait:
            wait_rows(ybuf.at[buf], nv_ref[tile], ssem.at[buf])
        else:
            for_rows(nv_ref[tile], lambda j: _row_copy(ybuf.at[buf], j, y_ref, dst_ref[tile * tm + j],
                                                       ssem.at[buf]).start())

    @pl.when(i == 0)
    def _():
        gather(0, 0, False)

    @pl.when(i + 1 < n_used)
    def _():
        gather(i + 1, 1 - cur, False)

    @pl.when(used)
    def _():
        gather(i, cur, True)
        x = _from_tall(xbuf.at[cur]).astype(BF16)
        a = _dot(x, wg_ref[0])
        he = a * _sigmoid(a) * _dot(x, wu_ref[0])
        _to_tall(ybuf.at[cur], _dot(he.astype(BF16), wd_ref[0]))

    @pl.when(jnp.logical_and(i >= 1, i - 1 < n_used))
    def _():
        scatter(i - 1, 1 - cur, True)

    @pl.when(used)
    def _():
        scatter(i, cur, False)

    @pl.when(jnp.logical_and(i == pl.num_programs(0) - 1, used))
    def _():
        scatter(i, cur, True)


def _experts(h, tok_of_slot, dst_of_slot, tile_e, n_valid, n_used, wg, wu, wd, tm):
    d, f = wg.shape[1:]
    wspec = lambda shape: pl.BlockSpec(shape, lambda i, t, s, e, v, u: (e[i], 0, 0))
    buf = pltpu.VMEM((2, tm * ROW_SEG, LANE), F32)
    return pl.pallas_call(
        functools.partial(_expert_kernel, tm=tm),
        grid_spec=pltpu.PrefetchScalarGridSpec(
            num_scalar_prefetch=5,
            grid=(tile_e.shape[0],),
            in_specs=[pl.BlockSpec(memory_space=pl.ANY), wspec((1, d, f)), wspec((1, d, f)), wspec((1, f, d))],
            out_specs=pl.BlockSpec(memory_space=pl.ANY),
            scratch_shapes=[buf, buf, pltpu.SemaphoreType.DMA((2,)), pltpu.SemaphoreType.DMA((2,))]),
        out_shape=jax.ShapeDtypeStruct((2 * h.shape[0], LANE), F32),
        compiler_params=_cparams(("arbitrary",)),
        name="moe_experts",
    )(tok_of_slot, dst_of_slot, tile_e, n_valid, n_used, h, wg, wu, wd)


def _combine_kernel(x_ref, w_ref, gt_ref, gf_ref, y1_ref, y2_ref, o_ref, *, final):
    w = w_ref[...]
    f = w[:, 0:1] * _from_tall(y1_ref) + w[:, 1:2] * _from_tall(y2_ref)
    xo = x_ref[...] + gt_ref[0] * f
    if final:
        xo = _rms(xo, gf_ref[...])
    o_ref[...] = xo


def _combine(x, wts, mod3, g_final, y, dims, n_tok, tm, final):
    nl, t, b = dims
    d = x.shape[1]
    grp = functools.partial(_group_of, tm=tm, nl=nl, t=t, b=b)
    return pl.pallas_call(
        functools.partial(_combine_kernel, final=final),
        grid=(n_tok // tm,),
        in_specs=[pl.BlockSpec((tm, d), lambda i: (i, 0)),
                  pl.BlockSpec((tm, 2), lambda i: (i, 0)),
                  pl.BlockSpec((1, 1, d), lambda i: (grp(i) * 6 + 5, 0, 0)),
                  pl.BlockSpec((1, d), lambda i: (0, 0)),
                  pl.BlockSpec((tm * ROW_SEG, LANE), lambda i: (i, 0)),
                  pl.BlockSpec((tm * ROW_SEG, LANE), lambda i: (n_tok // tm + i, 0))],
        out_specs=pl.BlockSpec((tm, d), lambda i: (i, 0)),
        out_shape=jax.ShapeDtypeStruct((n_tok, d), F32),
        compiler_params=_cparams(("parallel",)),
        name="moe_combine",
    )(x, wts, mod3, g_final, y, y)


def _moe(h2, route, xn, mod3, g_final, wg, wu, wd, dims, final):
    n_tok = xn.shape[0]
    max_tiles = (2 * n_tok) // MOE_TM + N_EXPERTS
    tok_of_slot, dst_of_slot, tile_e, n_valid, n_used, wts = _slot_plan(route, MOE_TM, max_tiles)
    y = _experts(h2, tok_of_slot, dst_of_slot, tile_e, n_valid, n_used, wg, wu, wd, MOE_TM)
    nl, t, _ = dims
    tm = 2 * ROW_TM if t % (2 * ROW_TM) == 0 and (n_tok - nl) % (2 * ROW_TM) == 0 else ROW_TM
    return _combine(xn, wts, mod3, g_final, y, dims, n_tok, tm, final)


def _gate_weight(w_cg):
    d = w_cg.shape[0]
    per_head = w_cg.reshape(d, 4, M_HEADS).transpose(0, 2, 1)
    return jnp.pad(per_head, ((0, 0), (0, 0), (0, LANE - 4))).reshape(d, M_HEADS * LANE)


def _layer(x, mod3, dims, ctx_len, last, g1, g2, w_in, g_sgu, w_sp, b_sp, rpb, b_mgate, g_mnorm, w_branch, w_out,
           w_rt, b_r, wg, wu, wd, g_final):
    nl, t, b = dims
    nt, d = x.shape
    fits = lambda rows: t % rows == 0 and (nt - nl) % rows == 0
    tm = 512 if fits(512) else 256
    tm_proj = 1024 if fits(1024) else tm

    w_main = jnp.concatenate([w_in[:, :OFF_CG], w_in[:, OFF_CG + M_NGATE:]], axis=1).astype(BF16)
    w_gate = _gate_weight(w_in[:, OFF_CG:OFF_CG + M_NGATE]).astype(BF16)
    p = _inproj(x, mod3, g1, w_main, dims, tm_proj, 1024, BF16)
    g = _inproj(x, mod3, g1, w_gate, dims, tm_proj, M_HEADS * LANE, F32)

    n_rows = nl if last else nt
    b_full = jnp.repeat(b_sp.T, A_GDIM, axis=1)
    ya = _sgu(p, g_sgu[None, :], w_sp.astype(BF16), b_full, n_rows, tm)

    yb_l, yb_c = _natten(p, _na_bias_table(rpb, t // GRID_W), dims, ctx_len)

    bm = jnp.pad(b_mgate.T, ((0, 0), (0, LANE - 4))).reshape(1, M_HEADS * LANE)
    yc_l, yc_c = _mlstm(p, g, bm, g_mnorm[None, :], dims, ctx_len)

    if last:
        yb, yc = yb_l, yc_l
    else:
        yb = jnp.concatenate([yb_l, yb_c], axis=0)
        yc = jnp.concatenate([yc_l, yc_c], axis=0)

    merged = _merge(ya, yb, yc, p, w_branch.astype(BF16), n_rows, tm, d)
    xn, h2, route = _outproj(merged, x, mod3, g2[None, :], w_out.astype(BF16), w_rt, b_r, dims, n_rows, tm)
    return _moe(h2, route, xn, mod3, g_final, wg.astype(BF16), wu.astype(BF16), wd.astype(BF16), dims, last)


def kernel(x, c, ctx, c_ctx, w_ada, b_ada, g_norm1, g_norm2, w_in, g_sgu, w_spatial, b_spatial, na_rpb, b_mgate,
           g_mnorm, w_branch, w_out, w_router, b_router, w_e_gate, w_e_up, w_e_down, g_final):
    b, t, d = x.shape
    ctx_len = ctx.shape[1]
    nl = b * t
    dims = (nl, t, b)
    depth = w_ada.shape[0]
    xs = jnp.concatenate([x.reshape(nl, d), ctx.reshape(b * ctx_len, d)], axis=0)
    n_mod = -(-(b + 1) // 8) * 8
    cc = jnp.zeros((n_mod, d), F32).at[:b].set(c).at[b].set(c_ctx)
    w_rt = w_router.T.astype(BF16)
    b_r = b_router[:, None].astype(F32)
    for layer in range(depth):
        mod = _ada(cc, w_ada[layer], b_ada[layer][None, :])
        mod3 = mod.reshape(n_mod * 6, 1, d)
        xs = _layer(xs, mod3, dims, ctx_len, layer == depth - 1, g_norm1[layer][None, :], g_norm2[layer],
                    w_in[layer], g_sgu[layer], w_spatial[layer], b_spatial[layer], na_rpb[layer], b_mgate[layer],
                    g_mnorm[layer], w_branch[layer], w_out[layer], w_rt, b_r, w_e_gate[layer], w_e_up[layer],
                    w_e_down[layer], g_final[None, :])
    return xs.reshape(b, t, d)
```

```python
import functools

import numpy as np
import jax
import jax.numpy as jnp
from jax import lax
from jax.experimental import pallas as pl
from jax.experimental.pallas import tpu as pltpu

F32 = jnp.float32
BF16 = jnp.bfloat16

GRID_W = 64
EPS = 1e-6
NEG_INF = -1e30
ROPE_THETA = 10000.0

A_WIDTH = 1024
A_CHUNK = 128
A_GROUPS = 8
A_GDIM = A_WIDTH // A_GROUPS

NA_HEADS = 8
NA_HDIM = 128
NA_WIDTH = NA_HEADS * NA_HDIM
NA_KH = 8
NA_KW = 16

M_HEADS = 4
M_DK = 128
M_DV = 256
M_QK_W = M_HEADS * M_DK
M_V_W = M_HEADS * M_DV
M_CHUNK = 128
M_NGATE = 4 * M_HEADS

N_BRANCH = 3
BRANCH_W = 1024

N_EXPERTS = 16
N_GROUPS = 4
EXP_PER_GROUP = N_EXPERTS // N_GROUPS
D_FF_EXPERT = 1024

OFF_AU = 0
OFF_AV = OFF_AU + A_WIDTH
OFF_BQ = OFF_AV + A_WIDTH
OFF_BK = OFF_BQ + NA_WIDTH
OFF_BV = OFF_BK + NA_WIDTH
OFF_CQ = OFF_BV + NA_WIDTH
OFF_CK = OFF_CQ + M_QK_W
OFF_CV = OFF_CK + M_QK_W
OFF_CO = OFF_CV + M_V_W
OFF_CG = OFF_CO + M_V_W
OFF_GATE = OFF_CO + M_V_W

LANE = 128
VMEM_LIMIT = 56 * 1024 * 1024

ROW_SEG = 16
MOE_TM = 256
ROW_TM = 256
DMA_UNROLL = 8


def _cparams(sem):
    return pltpu.CompilerParams(dimension_semantics=sem, vmem_limit_bytes=VMEM_LIMIT)


def _dot(a, b):
    return jnp.dot(a, b, preferred_element_type=F32)


def _dot_nt(a, b):
    return lax.dot_general(a, b, (((1,), (1,)), ((), ())), preferred_element_type=F32)


def _sigmoid(x):
    return 1.0 / (1.0 + jnp.exp(-x))


def _gelu_tanh(x):
    return 0.5 * x * (1.0 + jnp.tanh(np.sqrt(2.0 / np.pi).astype(np.float32) * (x + 0.044715 * (x * x * x))))


def _rms(x, g):
    return x * lax.rsqrt(jnp.mean(x * x, axis=-1, keepdims=True) + EPS) * g


def _ada_kernel(c_ref, w_ref, b_ref, o_ref):
    cc = c_ref[...]
    s = cc * _sigmoid(cc)
    o_ref[...] = _dot(s.astype(BF16), w_ref[...].astype(BF16)) + b_ref[...]


def _ada(cc, w, b):
    rows, d = cc.shape
    n = w.shape[1]
    tn = 1024
    return pl.pallas_call(
        _ada_kernel,
        grid=(n // tn,),
        in_specs=[pl.BlockSpec((rows, d), lambda j: (0, 0)),
                  pl.BlockSpec((d, tn), lambda j: (0, j)),
                  pl.BlockSpec((1, tn), lambda j: (0, j))],
        out_specs=pl.BlockSpec((rows, tn), lambda j: (0, j)),
        out_shape=jax.ShapeDtypeStruct((rows, n), F32),
        compiler_params=_cparams(("arbitrary",)),
        name="ada_mod",
    )(cc, w, b)


def _inproj_kernel(x_ref, sh_ref, sc_ref, g_ref, w_ref, o_ref, h_ref):
    @pl.when(pl.program_id(1) == 0)
    def _():
        h = _rms(x_ref[...], g_ref[...]) * (1.0 + sc_ref[0]) + sh_ref[0]
        h_ref[...] = h.astype(BF16)

    o_ref[...] = _dot(h_ref[...], w_ref[...]).astype(o_ref.dtype)


def _group_of(i, tm, nl, t, b):
    return jnp.where(i * tm < nl, (i * tm) // t, b)


def _inproj(x, mod3, g, w, dims, tm, tn, out_dtype, row0=0, n_rows=None):
    nl, t, b = dims
    d = x.shape[1]
    n = w.shape[1]
    n_rows = x.shape[0] - row0 if n_rows is None else n_rows
    first = row0 // tm
    grp = lambda i: _group_of(i + first, tm=tm, nl=nl, t=t, b=b)
    return pl.pallas_call(
        _inproj_kernel,
        grid=(n_rows // tm, n // tn),
        in_specs=[pl.BlockSpec((tm, d), lambda i, j: (i + first, 0)),
                  pl.BlockSpec((1, 1, d), lambda i, j: (grp(i) * 6 + 0, 0, 0)),
                  pl.BlockSpec((1, 1, d), lambda i, j: (grp(i) * 6 + 1, 0, 0)),
                  pl.BlockSpec((1, d), lambda i, j: (0, 0)),
                  pl.BlockSpec((d, tn), lambda i, j: (0, j))],
        out_specs=pl.BlockSpec((tm, tn), lambda i, j: (i, j)),
        out_shape=jax.ShapeDtypeStruct((n_rows, n), out_dtype),
        scratch_shapes=[pltpu.VMEM((tm, d), BF16)],
        compiler_params=_cparams(("parallel", "arbitrary")),
        name="inproj",
    )(x, mod3, mod3, g, w)


def _sgu_kernel(u_ref, v_ref, g_ref, w_ref, b_ref, o_ref):
    n_chunk = u_ref.shape[0] // A_CHUNK
    for c in range(n_chunk):
        rows = slice(c * A_CHUNK, (c + 1) * A_CHUNK)
        u = _gelu_tanh(u_ref[rows, :].astype(F32))
        v = _rms(_gelu_tanh(v_ref[rows, :].astype(F32)), g_ref[...]).astype(BF16)
        for gi in range(A_GROUPS):
            cols = slice(gi * A_GDIM, (gi + 1) * A_GDIM)
            mixed = _dot(w_ref[gi], v[:, cols]) + b_ref[:, cols]
            o_ref[rows, cols] = (u[:, cols] * mixed).astype(o_ref.dtype)


def _sgu(p, g_sgu, w_sp, b_full, n_rows, tm):
    return pl.pallas_call(
        _sgu_kernel,
        grid=(n_rows // tm,),
        in_specs=[pl.BlockSpec((tm, A_WIDTH), lambda i: (i, OFF_AU // A_WIDTH)),
                  pl.BlockSpec((tm, A_WIDTH), lambda i: (i, OFF_AV // A_WIDTH)),
                  pl.BlockSpec((1, A_WIDTH), lambda i: (0, 0)),
                  pl.BlockSpec((A_GROUPS, A_CHUNK, A_CHUNK), lambda i: (0, 0, 0)),
                  pl.BlockSpec((A_CHUNK, A_WIDTH), lambda i: (0, 0))],
        out_specs=pl.BlockSpec((tm, A_WIDTH), lambda i: (i, 0)),
        out_shape=jax.ShapeDtypeStruct((n_rows, A_WIDTH), BF16),
        compiler_params=_cparams(("parallel",)),
        name="sgu",
    )(p, p, g_sgu, w_sp, b_full)


NA_GQ = 4


def _na_geometry(rows):
    kh = min(NA_KH, rows)
    gq = min(NA_GQ, rows)
    wr = min(gq + kh - 1, rows)
    starts = [int(np.clip(gq * g - kh // 2, 0, rows - wr)) for g in range(rows // gq)]
    return kh, gq, wr, starts


def _na_row_select(rows):
    kh, gq, wr, starts = _na_geometry(rows)
    n_rel = 2 * NA_KH - 1
    row_sel = np.zeros((len(starts), gq, wr, n_rel + 1), np.float32)
    for g, ws in enumerate(starts):
        for qi in range(gq):
            r = gq * g + qi
            rs = int(np.clip(r - kh // 2, 0, rows - kh))
            assert ws <= rs and rs + kh <= ws + wr
            for a in range(wr):
                inside = rs <= ws + a < rs + kh
                row_sel[g, qi, a, ws + a - r + NA_KH - 1 if inside else n_rel] = 1.0
    kinds, kind_of = [], []
    for g in range(len(starts)):
        same = [k for k, rep in enumerate(kinds) if np.array_equal(row_sel[rep], row_sel[g])]
        if not same:
            kinds.append(g)
        kind_of.append(same[0] if same else len(kinds) - 1)
    return row_sel[kinds], kind_of


def _na_bias_table(rpb, rows):
    kh, gq, wr, starts = _na_geometry(rows)
    n_rel = 2 * NA_KH - 1
    cq = np.arange(GRID_W)[:, None]
    kc = np.arange(GRID_W)[None, :]
    cs = np.clip(cq - NA_KW // 2, 0, GRID_W - NA_KW)
    valid = (kc >= cs) & (kc < cs + NA_KW)
    coff = np.clip(kc - cq, -(NA_KW - 1), NA_KW - 1) + NA_KW - 1
    col_sel = (coff[None, :, :] == np.arange(2 * NA_KW - 1)[:, None, None]).astype(np.float32)
    by_col = jnp.einsum('hrc,cqk->hrqk', rpb.astype(F32), col_sel, precision=lax.Precision.HIGHEST)
    by_col = jnp.where(valid[None, None], by_col, NEG_INF)
    by_col = jnp.concatenate([by_col, jnp.full_like(by_col[:, :1], NEG_INF)], axis=1)
    row_sel, _ = _na_row_select(rows)
    tab = jnp.einsum('gqar,hrck->hgqcak', row_sel, by_col, precision=lax.Precision.HIGHEST)
    return tab.reshape(rpb.shape[0], row_sel.shape[0], gq * GRID_W, wr * GRID_W)


def _softmax_pv(s_parts, v_parts):
    m = functools.reduce(jnp.maximum, [jnp.max(s, axis=-1, keepdims=True) for s in s_parts])
    p_parts = [jnp.exp(s - m) for s in s_parts]
    l = functools.reduce(jnp.add, [jnp.sum(p, axis=-1, keepdims=True) for p in p_parts])
    o = functools.reduce(jnp.add, [_dot(p.astype(BF16), v) for p, v in zip(p_parts, v_parts)])
    return o / l


def _natten_kernel(q_ref, k_ref, v_ref, qc_ref, kc_ref, vc_ref, tab_ref, o_ref, oc_ref, *, rows):
    scale = NA_HDIM ** -0.5
    _, gq, wr, starts = _na_geometry(rows)
    _, kind_of = _na_row_select(rows)
    kc = kc_ref[...]
    vc = vc_ref[...]
    for g, ws in enumerate(starts):
        qs = slice(g * gq * GRID_W, (g + 1) * gq * GRID_W)
        ks = slice(ws * GRID_W, (ws + wr) * GRID_W)
        q = q_ref[qs, :]
        s_w = _dot_nt(q, k_ref[ks, :]) * scale + tab_ref[0, kind_of[g]]
        s_c = _dot_nt(q, kc) * scale
        o_ref[qs, :] = _softmax_pv([s_w, s_c], [v_ref[ks, :], vc]).astype(o_ref.dtype)
    s = _dot_nt(qc_ref[...], kc) * scale
    oc_ref[...] = _softmax_pv([s], [vc]).astype(oc_ref.dtype)


def _natten(p, pc, tab, dims, ctx_len):
    nl, t, b = dims
    rows = t // GRID_W
    pc_arr, pc_row0, pc_col0 = pc
    cb = pc_row0 // ctx_len
    hd = NA_HDIM
    ctx_q = OFF_BQ if pc_col0 <= OFF_BQ else OFF_BK
    lat = lambda off: pl.BlockSpec((t, hd), lambda h, bi: (bi, off // hd + h))
    ctx = lambda off: pl.BlockSpec((ctx_len, hd), lambda h, bi: (cb + bi, (off - pc_col0) // hd + h))
    return pl.pallas_call(
        functools.partial(_natten_kernel, rows=rows),
        grid=(NA_HEADS, b),
        in_specs=[lat(OFF_BQ), lat(OFF_BK), lat(OFF_BV), ctx(ctx_q), ctx(OFF_BK), ctx(OFF_BV),
                  pl.BlockSpec((1,) + tab.shape[1:], lambda h, bi: (h, 0, 0, 0))],
        out_specs=[pl.BlockSpec((t, hd), lambda h, bi: (bi, h)),
                   pl.BlockSpec((ctx_len, hd), lambda h, bi: (bi, h))],
        out_shape=[jax.ShapeDtypeStruct((nl, NA_WIDTH), BF16),
                   jax.ShapeDtypeStruct((b * ctx_len, NA_WIDTH), BF16)],
        compiler_params=_cparams(("parallel", "arbitrary")),
        name="natten",
    )(p, p, p, pc_arr, pc_arr, pc_arr, tab)


def _scan_rows(x, reverse, op, fill):
    n = x.shape[0]
    idx = lax.broadcasted_iota(jnp.int32, x.shape, 0)
    step = 1
    while step < n:
        if reverse:
            x = op(x, jnp.where(idx < n - step, pltpu.roll(x, n - step, 0), fill))
        else:
            x = op(x, jnp.where(idx >= step, pltpu.roll(x, step, 0), fill))
        step *= 2
    return x


def _log_sigmoid(x):
    return jnp.minimum(x, 0.0) - jnp.log(1.0 + jnp.exp(-jnp.abs(x)))


def _rope_swap(x):
    lane = lax.broadcasted_iota(jnp.int32, x.shape, 1)
    return jnp.where(lane % 64 < 32, pltpu.roll(x, LANE - 32, 1), pltpu.roll(x, 32, 1))


def _chunk_rows(c):
    return pl.ds(pl.multiple_of(c * M_CHUNK, M_CHUNK), M_CHUNK)


def _gate_rows(c):
    return pl.ds(pl.multiple_of(c * 8, 8), 4)


def _mlstm_prepare(seq, bm_ref, rope):
    shape = (M_CHUNK, LANE)

    def body(c, carry):
        rows = _chunk_rows(c)
        gc = seq["g"][rows, :] + bm_ref[...]
        lf = _log_sigmoid(gc)
        xs = []
        for d in range(2):
            reverse = d == 1
            b = _scan_rows(lf, reverse, jnp.add, 0.0)
            a_rep = jnp.broadcast_to(b[:, 2 * d + 1:2 * d + 2], shape)
            x_rep = jnp.broadcast_to(gc[:, 2 * d:2 * d + 1], shape) - a_rep
            seq["a"][d, rows, :] = a_rep
            seq["x"][d, rows, :] = x_rep
            seq["mi"][d, rows, :] = a_rep + _scan_rows(x_rep, reverse, jnp.maximum, -jnp.inf)
            seq["row"][pl.ds(c * 8 + 2 * d + 1, 1), :] = jnp.max(x_rep, axis=0, keepdims=True)
            xs.append(x_rep)
        lane = lax.broadcasted_iota(jnp.int32, shape, 1)
        xt = jnp.where(lane < LANE // 2, xs[0], xs[1]).T
        seq["row"][pl.ds(c * 8, 1), :] = xt[0:1, :]
        seq["row"][pl.ds(c * 8 + 2, 1), :] = xt[LANE // 2:LANE // 2 + 1, :]
        q = seq["q"][rows, :].astype(F32)
        k = seq["k"][rows, :].astype(F32)
        if rope is not None:
            cos = rope[0][rows, :]
            sin = rope[1][rows, :]
            q = q * cos + _rope_swap(q) * sin
            k = k * cos + _rope_swap(k) * sin
        seq["qs"][rows, :] = (q * (M_DK ** -0.5)).astype(BF16)
        seq["ks"][rows, :] = k
        return carry

    lax.fori_loop(0, seq["n"], body, 0)


def _mlstm_step(seq, c, reverse, state, order):
    c_ref, m_ref = state
    d = 1 if reverse else 0
    ln = M_CHUNK
    rows = _chunk_rows(c)
    a = seq["a"][d, rows, :]
    x = seq["x"][d, rows, :]
    gate_rows = seq["row"][_gate_rows(c), :]
    x_row, x_max = gate_rows[2 * d:2 * d + 1, :], gate_rows[2 * d + 1:2 * d + 2, :]
    q = seq["qs"][rows, :]
    k = seq["ks"][rows, :]
    vx = jnp.concatenate([seq["v"][rows, :], jnp.ones((ln, LANE), BF16)], axis=1)
    m_prev = m_ref[d]
    c_prev = c_ref[d]

    inter = a + m_prev
    m_row = jnp.maximum(inter, seq["mi"][d, rows, :])
    s = _dot_nt(q, k.astype(BF16)) * jnp.exp(jnp.where(order, a + x_row, -jnp.inf) - m_row)
    ew = jnp.exp(inter - m_row)
    tot = _dot(s.astype(BF16), vx) + jnp.concatenate([ew] * 3, axis=1) * _dot(q, c_prev.astype(BF16))
    inv = 1.0 / jnp.maximum(jnp.abs(tot[:, M_DV:]), jnp.exp(-m_row))
    seq["hb" if reverse else "hf"][rows, :] = tot[:, :M_DV] * jnp.concatenate([inv] * 2, axis=1)

    b_last = a[0:1, :] if reverse else a[ln - 1:ln, :]
    m_new = jnp.maximum(b_last + m_prev, b_last + x_max)
    decay = jnp.exp(b_last + m_prev - m_new)
    kw = k * jnp.exp(b_last + x - m_new)
    c_ref[d] = jnp.concatenate([decay] * 3, axis=1) * c_prev + _dot(kw.T.astype(BF16), vx)
    m_ref[d] = m_new


def _mlstm_scan(seq, state):
    n = seq["n"]
    t_idx = lax.broadcasted_iota(jnp.int32, (M_CHUNK, M_CHUNK), 0)
    s_idx = lax.broadcasted_iota(jnp.int32, (M_CHUNK, M_CHUNK), 1)

    def body(ci, carry):
        _mlstm_step(seq, ci, False, state, t_idx >= s_idx)
        _mlstm_step(seq, n - 1 - ci, True, state, s_idx >= t_idx)
        return carry

    lax.fori_loop(0, n, body, 0, unroll=min(4, n))


def _mlstm_finish(seq, gn_ref):
    def body(c, carry):
        rows = _chunk_rows(c)
        y = _rms(seq["hf"][rows, :] + seq["hb"][rows, :], gn_ref[...])
        seq["y"][rows, :] = (y * _sigmoid(seq["o"][rows, :].astype(F32))).astype(seq["y"].dtype)
        return carry

    lax.fori_loop(0, seq["n"], body, 0)


_SEQ_IN = ("q", "k", "v", "o", "g")
_SEQ_SCRATCH = ("qs", "ks", "a", "x", "mi", "row", "hf", "hb")


def _mlstm_kernel(*refs):
    lat = dict(zip(_SEQ_IN, refs[0:5]))
    ctx = dict(zip(_SEQ_IN, refs[5:10]))
    bm_ref, gn_ref, cos_ref, sin_ref = refs[10:14]
    lat["y"], ctx["y"] = refs[14:16]
    lat.update(zip(_SEQ_SCRATCH, refs[16:24]))
    ctx.update(zip(_SEQ_SCRATCH, refs[24:32]))
    state = refs[32:34]
    for seq in (lat, ctx):
        seq["n"] = seq["q"].shape[0] // M_CHUNK

    _mlstm_prepare(ctx, bm_ref, None)
    _mlstm_prepare(lat, bm_ref, (cos_ref, sin_ref))
    for ref in state:
        ref[...] = jnp.zeros_like(ref)
    _mlstm_scan(ctx, state)
    _mlstm_scan(lat, state)
    _mlstm_finish(ctx, gn_ref)
    _mlstm_finish(lat, gn_ref)


def _rope_tables(t):
    pos = np.arange(t)
    half = M_DK // 4
    inv = ROPE_THETA ** (-np.arange(half, dtype=np.float64) / half)
    ang_r = (pos // GRID_W)[:, None] * inv[None, :]
    ang_c = (pos % GRID_W)[:, None] * inv[None, :]
    ang = np.concatenate([ang_r, ang_r, ang_c, ang_c], axis=1)
    sign = np.tile(np.concatenate([-np.ones(half), np.ones(half)]), 2)[None, :]
    return jnp.asarray(np.cos(ang), F32), jnp.asarray(np.sin(ang) * sign, F32)


def _seq_scratch(n):
    col = pltpu.VMEM((2, n, LANE), F32)
    return [pltpu.VMEM((n, M_DK), BF16), pltpu.VMEM((n, M_DK), F32), col, col, col,
            pltpu.VMEM((n // M_CHUNK * 8, LANE), F32), pltpu.VMEM((n, M_DV), F32), pltpu.VMEM((n, M_DV), F32)]


def _mlstm(p, pc, g, bm, gn, dims, ctx_len):
    nl, t, b = dims
    pc_arr, pc_row0, pc_col0 = pc
    cb = pc_row0 // ctx_len
    gb = nl // ctx_len
    ctx_o = OFF_CO if pc_arr.shape[1] + pc_col0 >= OFF_CO + M_V_W else OFF_CV
    cos, sin = _rope_tables(t)
    dk, dv = M_DK, M_DV
    lat = lambda off, w: pl.BlockSpec((t, w), lambda bi, h: (bi, off // w + h))
    ctx = lambda off, w: pl.BlockSpec((ctx_len, w), lambda bi, h: (cb + bi, (off - pc_col0) // w + h))
    const = lambda shape: pl.BlockSpec(shape, lambda bi, h: (0, 0))
    return pl.pallas_call(
        _mlstm_kernel,
        grid=(b, M_HEADS),
        in_specs=[lat(OFF_CQ, dk), lat(OFF_CK, dk), lat(OFF_CV, dv), lat(OFF_CO, dv),
                  pl.BlockSpec((t, LANE), lambda bi, h: (bi, h)),
                  ctx(OFF_CQ, dk), ctx(OFF_CK, dk), ctx(OFF_CV, dv), ctx(ctx_o, dv),
                  pl.BlockSpec((ctx_len, LANE), lambda bi, h: (gb + bi, h)),
                  pl.BlockSpec((1, LANE), lambda bi, h: (0, h)),
                  pl.BlockSpec((1, dv), lambda bi, h: (0, h)),
                  const((t, LANE)), const((t, LANE))],
        out_specs=[pl.BlockSpec((t, dv), lambda bi, h: (bi, h)),
                   pl.BlockSpec((ctx_len, dv), lambda bi, h: (bi, h))],
        out_shape=[jax.ShapeDtypeStruct((nl, M_V_W), BF16),
                   jax.ShapeDtypeStruct((b * ctx_len, M_V_W), BF16)],
        scratch_shapes=_seq_scratch(t) + _seq_scratch(ctx_len) + [
            pltpu.VMEM((2, dk, dv + LANE), F32), pltpu.VMEM((2, 1, LANE), F32)],
        compiler_params=_cparams(("parallel", "arbitrary")),
        name="mlstm",
    )(p, p, p, p, g, pc_arr, pc_arr, pc_arr, pc_arr, g, bm, gn, cos, sin)


def _merge_kernel(ya_ref, yb_ref, yc_ref, g0_ref, g1_ref, g2_ref, w_ref, o_ref):
    acc = _sigmoid(g0_ref[...].astype(F32)) * _dot(ya_ref[...], w_ref[0])
    acc = acc + _sigmoid(g1_ref[...].astype(F32)) * _dot(yb_ref[...], w_ref[1])
    acc = acc + _sigmoid(g2_ref[...].astype(F32)) * _dot(yc_ref[...], w_ref[2])
    o_ref[...] = acc.astype(o_ref.dtype)


def _merge(ya, yb, yc, p, w_branch, n_rows, tm, tn):
    d = w_branch.shape[2]
    br = lambda: pl.BlockSpec((tm, BRANCH_W), lambda i, j: (i, 0))
    gate = lambda k: pl.BlockSpec((tm, tn), lambda i, j: (i, (OFF_GATE + k * d) // tn + j))
    return pl.pallas_call(
        _merge_kernel,
        grid=(n_rows // tm, d // tn),
        in_specs=[br(), br(), br(), gate(0), gate(1), gate(2),
                  pl.BlockSpec((N_BRANCH, BRANCH_W, tn), lambda i, j: (0, 0, j),
                               pipeline_mode=pl.Buffered(1) if tn == d else None)],
        out_specs=pl.BlockSpec((tm, tn), lambda i, j: (i, j)),
        out_shape=jax.ShapeDtypeStruct((n_rows, d), BF16),
        compiler_params=_cparams(("parallel", "arbitrary")),
        name="merge",
    )(ya, yb, yc, p, p, p, w_branch)


def _route(scores, sel):
    row = lambda a, e: a[e:e + 1, :]
    best = None
    for gi in range(N_GROUPS):
        a, b, c, d = (row(sel, gi * EXP_PER_GROUP + j) for j in range(EXP_PER_GROUP))
        hi1, lo1 = jnp.maximum(a, b), jnp.minimum(a, b)
        hi2, lo2 = jnp.maximum(c, d), jnp.minimum(c, d)
        gs = jnp.maximum(hi1, hi2) + jnp.maximum(jnp.minimum(hi1, hi2), jnp.maximum(lo1, lo2))
        if best is None:
            best, g_idx = gs, jnp.zeros_like(gs, dtype=jnp.int32)
        else:
            take = gs > best
            best = jnp.where(take, gs, best)
            g_idx = jnp.where(take, gi, g_idx)

    def in_group(a, j):
        out = row(a, j)
        for gi in range(1, N_GROUPS):
            out = jnp.where(g_idx == gi, row(a, gi * EXP_PER_GROUP + j), out)
        return out

    v = [in_group(sel, j) for j in range(EXP_PER_GROUP)]
    u = [in_group(scores, j) for j in range(EXP_PER_GROUP)]

    def argmax_first(vals):
        bv, bi, bu = vals[0], jnp.zeros_like(g_idx), u[0]
        for j in range(1, EXP_PER_GROUP):
            take = vals[j] > bv
            bv = jnp.where(take, vals[j], bv)
            bi = jnp.where(take, j, bi)
            bu = jnp.where(take, u[j], bu)
        return bi, bu

    i1, s1 = argmax_first(v)
    i2, s2 = argmax_first([jnp.where(i1 == j, -jnp.inf, v[j]) for j in range(EXP_PER_GROUP)])
    tot = s1 + s2
    return g_idx * EXP_PER_GROUP + i1, g_idx * EXP_PER_GROUP + i2, s1 / tot, s2 / tot


def _to_tall(ref, val):
    rows = val.shape[0]
    for j in range(ROW_SEG):
        ref[pl.ds(j, rows, stride=ROW_SEG), :] = val[:, j * LANE:(j + 1) * LANE]


def _from_tall(ref):
    rows = ref.shape[0] // ROW_SEG
    return jnp.concatenate([ref[pl.ds(j, rows, stride=ROW_SEG), :] for j in range(ROW_SEG)], axis=1)


def _outproj_kernel(mg_ref, x_ref, gt_ref, sh_ref, sc_ref, g_ref, w_ref, wr_ref, br_ref, xo_ref, h_ref, r_ref):
    y = _dot(mg_ref[...], w_ref[...])
    xn = x_ref[...] + gt_ref[0] * y
    xo_ref[...] = xn
    h = _rms(xn, g_ref[...]) * (1.0 + sc_ref[0]) + sh_ref[0]
    _to_tall(h_ref, h)
    scores = _sigmoid(_dot_nt(wr_ref[...], h.astype(BF16)))
    e1, e2, w1, w2 = _route(scores, scores + br_ref[...])
    r_ref[...] = jnp.zeros_like(r_ref)
    r_ref[0:1, :] = e1.astype(F32)
    r_ref[1:2, :] = e2.astype(F32)
    r_ref[2:3, :] = w1
    r_ref[3:4, :] = w2


def _outproj(merged, x, mod3, g2, w_out, w_rt, b_r, dims, n_rows, tm):
    nl, t, b = dims
    d = x.shape[1]
    grp = functools.partial(_group_of, tm=tm, nl=nl, t=t, b=b)
    mrow = lambda k: pl.BlockSpec((1, 1, d), lambda i: (grp(i) * 6 + k, 0, 0))
    return pl.pallas_call(
        _outproj_kernel,
        grid=(n_rows // tm,),
        in_specs=[pl.BlockSpec((tm, d), lambda i: (i, 0)),
                  pl.BlockSpec((tm, d), lambda i: (i, 0)),
                  mrow(2), mrow(3), mrow(4),
                  pl.BlockSpec((1, d), lambda i: (0, 0)),
                  pl.BlockSpec((d, d), lambda i: (0, 0), pipeline_mode=pl.Buffered(1)),
                  pl.BlockSpec((N_EXPERTS, d), lambda i: (0, 0)),
                  pl.BlockSpec((N_EXPERTS, 1), lambda i: (0, 0))],
        out_specs=[pl.BlockSpec((tm, d), lambda i: (i, 0)),
                   pl.BlockSpec((tm * ROW_SEG, LANE), lambda i: (i, 0)),
                   pl.BlockSpec((8, tm), lambda i: (0, i))],
        out_shape=[jax.ShapeDtypeStruct((n_rows, d), F32),
                   jax.ShapeDtypeStruct((n_rows * ROW_SEG, LANE), F32),
                   jax.ShapeDtypeStruct((8, n_rows), F32)],
        compiler_params=_cparams(("parallel",)),
        name="outproj_route",
    )(merged, x, mod3, mod3, mod3, g2, w_out, w_rt, b_r)


def _slot_plan(route, tm, max_tiles):
    n = route.shape[1]
    e = route[0:2, :].astype(jnp.int32).reshape(-1)
    counts = jnp.sum((e[:, None] == jnp.arange(N_EXPERTS, dtype=jnp.int32)[None, :]).astype(jnp.int32), axis=0)
    padded = ((counts + tm - 1) // tm) * tm
    ends = jnp.cumsum(padded)
    tile_ends = ends // tm
    n_used = tile_ends[-1]
    tiles = jnp.arange(max_tiles, dtype=jnp.int32)
    src = jnp.minimum(tiles, n_used - 1)
    tile_e = jnp.minimum(jnp.sum((src[:, None] >= tile_ends[None, :]).astype(jnp.int32), axis=1), N_EXPERTS - 1)
    left = jnp.take(counts, tile_e) - (tiles - jnp.take(tile_ends - padded // tm, tile_e)) * tm
    n_valid = jnp.where(tiles < n_used, jnp.clip(left, 0, tm), 0)
    i32 = lambda a: a.astype(jnp.int32)
    tok_of_slot, dst_of_slot = _assign_slots(e, i32(ends - padded), i32(counts), i32(ends), n, max_tiles * tm)
    wts = route[2:4, :].T
    return tok_of_slot, dst_of_slot, i32(tile_e), i32(n_valid), i32(n_used.reshape(1)), wts


def _assign_kernel(groups_ref, e_ref, start_ref, count_ref, end_ref, tok_ref, dst_ref, next_ref, *, n):
    def clear(s, carry):
        tok_ref[s] = 0
        dst_ref[s] = 0
        return carry

    def per_expert(x, carry):
        next_ref[x] = start_ref[x]
        lax.fori_loop(start_ref[x] + count_ref[x], end_ref[x], clear, 0)
        return carry

    def put(g, carry):
        for u in range(DMA_UNROLL):
            f = g * DMA_UNROLL + u
            x = e_ref[f]
            s = next_ref[x]
            next_ref[x] = s + 1
            tok_ref[s] = jnp.where(f >= n, f - n, f)
            dst_ref[s] = f
        return carry

    lax.fori_loop(0, N_EXPERTS, per_expert, 0)
    lax.fori_loop(end_ref[N_EXPERTS - 1], tok_ref.shape[0], clear, 0)
    lax.fori_loop(0, groups_ref[0], put, 0)


def _assign_slots(e, starts, counts, ends, n, n_slots):
    assert (2 * n) % DMA_UNROLL == 0
    smem = pl.BlockSpec(memory_space=pltpu.SMEM)
    out = jax.ShapeDtypeStruct((n_slots,), jnp.int32)
    groups = jnp.array([2 * n // DMA_UNROLL], jnp.int32)
    return pl.pallas_call(
        functools.partial(_assign_kernel, n=n),
        in_specs=[smem] * 5, out_specs=[smem, smem], out_shape=[out, out],
        scratch_shapes=[pltpu.SMEM((N_EXPERTS,), jnp.int32)],
        name="moe_assign_slots",
    )(groups, e, starts, counts, ends)


def _row_copy(src_ref, src_row, dst_ref, dst_row, sem):
    src = src_ref.at[pl.ds(pl.multiple_of(src_row * ROW_SEG, ROW_SEG), ROW_SEG), :]
    dst = dst_ref.at[pl.ds(pl.multiple_of(dst_row * ROW_SEG, ROW_SEG), ROW_SEG), :]
    return pltpu.make_async_copy(src, dst, sem)


def _expert_kernel(tok_ref, dst_ref, te_ref, nv_ref, nu_ref, h_ref, wg_ref, wu_ref, wd_ref, y_ref,
                   xbuf, ybuf, gsem, ssem, *, tm):
    del te_ref
    i = pl.program_id(0)
    n_used = nu_ref[0]
    used = i < n_used
    cur = i % 2

    def for_rows(n_rows, body):
        def group(g, carry):
            for u in range(DMA_UNROLL):
                body(g * DMA_UNROLL + u)
            return carry

        def single(j, carry):
            body(j)
            return carry

        n_groups = n_rows // DMA_UNROLL
        lax.fori_loop(0, n_groups, group, 0)
        if not isinstance(n_rows, int) or n_rows % DMA_UNROLL:
            lax.fori_loop(n_groups * DMA_UNROLL, n_rows, single, 0)

    def wait_rows(buf_ref, n_rows, sem):
        part = buf_ref.at[pl.ds(0, n_rows * ROW_SEG), :]
        pltpu.make_async_copy(part, part, sem).wait()

    def gather(tile, buf, wait):
        if wait:
            wait_rows(xbuf.at[buf], tm, gsem.at[buf])
        else:
            for_rows(tm, lambda j: _row_copy(h_ref, tok_ref[tile * tm + j], xbuf.at[buf], j,
                                             gsem.at[buf]).start(priority=1))

    def scatter(tile, buf, wait):
        if wait:
            wait_rows(ybuf.at[buf], nv_ref[tile], ssem.at[buf])
        else:
            for_rows(nv_ref[tile], lambda j: _row_copy(ybuf.at[buf], j, y_ref, dst_ref[tile * tm + j],
                                                       ssem.at[buf]).start())

    @pl.when(i == 0)
    def _():
        gather(0, 0, False)

    @pl.when(i + 1 < n_used)
    def _():
        gather(i + 1, 1 - cur, False)

    @pl.when(used)
    def _():
        gather(i, cur, True)
        x = _from_tall(xbuf.at[cur]).astype(BF16)
        a = _dot(x, wg_ref[0])
        he = a * _sigmoid(a) * _dot(x, wu_ref[0])
        _to_tall(ybuf.at[cur], _dot(he.astype(BF16), wd_ref[0]))

    @pl.when(jnp.logical_and(i >= 1, i - 1 < n_used))
    def _():
        scatter(i - 1, 1 - cur, True)

    @pl.when(used)
    def _():
        scatter(i, cur, False)

    @pl.when(jnp.logical_and(i == pl.num_programs(0) - 1, used))
    def _():
        scatter(i, cur, True)


def _experts(h, tok_of_slot, dst_of_slot, tile_e, n_valid, n_used, wg, wu, wd, tm):
    d, f = wg.shape[1:]
    wspec = lambda shape: pl.BlockSpec(shape, lambda i, t, s, e, v, u: (e[i], 0, 0))
    buf = pltpu.VMEM((2, tm * ROW_SEG, LANE), F32)
    return pl.pallas_call(
        functools.partial(_expert_kernel, tm=tm),
        grid_spec=pltpu.PrefetchScalarGridSpec(
            num_scalar_prefetch=5,
            grid=(tile_e.shape[0],),
            in_specs=[pl.BlockSpec(memory_space=pl.ANY), wspec((1, d, f)), wspec((1, d, f)), wspec((1, f, d))],
            out_specs=pl.BlockSpec(memory_space=pl.ANY),
            scratch_shapes=[buf, buf, pltpu.SemaphoreType.DMA((2,)), pltpu.SemaphoreType.DMA((2,))]),
        out_shape=jax.ShapeDtypeStruct((2 * h.shape[0], LANE), F32),
        compiler_params=_cparams(("arbitrary",)),
        name="moe_experts",
    )(tok_of_slot, dst_of_slot, tile_e, n_valid, n_used, h, wg, wu, wd)


def _combine_kernel(x_ref, w_ref, gt_ref, gf_ref, y1_ref, y2_ref, o_ref, *, final):
    w = w_ref[...]
    f = w[:, 0:1] * _from_tall(y1_ref) + w[:, 1:2] * _from_tall(y2_ref)
    xo = x_ref[...] + gt_ref[0] * f
    if final:
        xo = _rms(xo, gf_ref[...])
    o_ref[...] = xo


def _combine(x, wts, mod3, g_final, y, dims, n_tok, tm, final):
    nl, t, b = dims
    d = x.shape[1]
    grp = functools.partial(_group_of, tm=tm, nl=nl, t=t, b=b)
    return pl.pallas_call(
        functools.partial(_combine_kernel, final=final),
        grid=(n_tok // tm,),
        in_specs=[pl.BlockSpec((tm, d), lambda i: (i, 0)),
                  pl.BlockSpec((tm, 2), lambda i: (i, 0)),
                  pl.BlockSpec((1, 1, d), lambda i: (grp(i) * 6 + 5, 0, 0)),
                  pl.BlockSpec((1, d), lambda i: (0, 0)),
                  pl.BlockSpec((tm * ROW_SEG, LANE), lambda i: (i, 0)),
                  pl.BlockSpec((tm * ROW_SEG, LANE), lambda i: (n_tok // tm + i, 0))],
        out_specs=pl.BlockSpec((tm, d), lambda i: (i, 0)),
        out_shape=jax.ShapeDtypeStruct((n_tok, d), F32),
        compiler_params=_cparams(("parallel",)),
        name="moe_combine",
    )(x, wts, mod3, g_final, y, y)


def _moe(h2, route, xn, mod3, g_final, wg, wu, wd, dims, final):
    n_tok = xn.shape[0]
    max_tiles = (2 * n_tok) // MOE_TM + N_EXPERTS
    tok_of_slot, dst_of_slot, tile_e, n_valid, n_used, wts = _slot_plan(route, MOE_TM, max_tiles)
    y = _experts(h2, tok_of_slot, dst_of_slot, tile_e, n_valid, n_used, wg, wu, wd, MOE_TM)
    nl, t, _ = dims
    tm = 2 * ROW_TM if t % (2 * ROW_TM) == 0 and (n_tok - nl) % (2 * ROW_TM) == 0 else ROW_TM
    return _combine(xn, wts, mod3, g_final, y, dims, n_tok, tm, final)


def _gate_weight(w_cg):
    d = w_cg.shape[0]
    per_head = w_cg.reshape(d, 4, M_HEADS).transpose(0, 2, 1)
    return jnp.pad(per_head, ((0, 0), (0, 0), (0, LANE - 4))).reshape(d, M_HEADS * LANE)


def _layer(x, mod3, dims, ctx_len, last, g1, g2, w_in, g_sgu, w_sp, b_sp, rpb, b_mgate, g_mnorm, w_branch, w_out,
           w_rt, b_r, wg, wu, wd, g_final):
    nl, t, b = dims
    nt, d = x.shape
    fits = lambda rows: t % rows == 0 and (nt - nl) % rows == 0
    tm = 512 if fits(512) else 256
    tm_proj = 1024 if fits(1024) else tm

    w_main = jnp.concatenate([w_in[:, :OFF_CG], w_in[:, OFF_CG + M_NGATE:]], axis=1).astype(BF16)
    w_gate = _gate_weight(w_in[:, OFF_CG:OFF_CG + M_NGATE]).astype(BF16)
    g = _inproj(x, mod3, g1, w_gate, dims, tm_proj, M_HEADS * LANE, F32)
    if last:
        p = _inproj(x, mod3, g1, w_main, dims, tm_proj, 1024, BF16, 0, nl)
        pc = (_inproj(x, mod3, g1, w_main[:, OFF_BK:OFF_CO], dims, tm_proj, 1024, BF16, nl, nt - nl), 0, OFF_BK)
    else:
        p = _inproj(x, mod3, g1, w_main, dims, tm_proj, 1024, BF16)
        pc = (p, nl, 0)

    n_rows = nl if last else nt
    b_full = jnp.repeat(b_sp.T, A_GDIM, axis=1)
    ya = _sgu(p, g_sgu[None, :], w_sp.astype(BF16), b_full, n_rows, tm)

    yb_l, yb_c = _natten(p, pc, _na_bias_table(rpb, t // GRID_W), dims, ctx_len)

    bm = jnp.pad(b_mgate.T, ((0, 0), (0, LANE - 4))).reshape(1, M_HEADS * LANE)
    yc_l, yc_c = _mlstm(p, pc, g, bm, g_mnorm[None, :], dims, ctx_len)

    if last:
        yb, yc = yb_l, yc_l
    else:
        yb = jnp.concatenate([yb_l, yb_c], axis=0)
        yc = jnp.concatenate([yc_l, yc_c], axis=0)

    merged = _merge(ya, yb, yc, p, w_branch.astype(BF16), n_rows, tm, d)
    xn, h2, route = _outproj(merged, x, mod3, g2[None, :], w_out.astype(BF16), w_rt, b_r, dims, n_rows, tm)
    return _moe(h2, route, xn, mod3, g_final, wg.astype(BF16), wu.astype(BF16), wd.astype(BF16), dims, last)


def kernel(x, c, ctx, c_ctx, w_ada, b_ada, g_norm1, g_norm2, w_in, g_sgu, w_spatial, b_spatial, na_rpb, b_mgate,
           g_mnorm, w_branch, w_out, w_router, b_router, w_e_gate, w_e_up, w_e_down, g_final):
    b, t, d = x.shape
    ctx_len = ctx.shape[1]
    nl = b * t
    dims = (nl, t, b)
    depth = w_ada.shape[0]
    xs = jnp.concatenate([x.reshape(nl, d), ctx.reshape(b * ctx_len, d)], axis=0)
    n_mod = -(-(b + 1) // 8) * 8
    cc = jnp.zeros((n_mod, d), F32).at[:b].set(c).at[b].set(c_ctx)
    w_rt = w_router.T.astype(BF16)
    b_r = b_router[:, None].astype(F32)
    for layer in range(depth):
        mod = _ada(cc, w_ada[layer], b_ada[layer][None, :])
        mod3 = mod.reshape(n_mod * 6, 1, d)
        xs = _layer(xs, mod3, dims, ctx_len, layer == depth - 1, g_norm1[layer][None, :], g_norm2[layer],
                    w_in[layer], g_sgu[layer], w_spatial[layer], b_spatial[layer], na_rpb[layer], b_mgate[layer],
                    g_mnorm[layer], w_branch[layer], w_out[layer], w_rt, b_r, w_e_gate[layer], w_e_up[layer],
                    w_e_down[layer], g_final[None, :])
    return xs.reshape(b, t, d)
```

```python
import functools

import numpy as np
import jax
import jax.numpy as jnp
from jax import lax
from jax.experimental import pallas as pl
from jax.experimental.pallas import tpu as pltpu

F32 = jnp.float32
BF16 = jnp.bfloat16

GRID_W = 64
EPS = 1e-6
NEG_INF = -1e30
ROPE_THETA = 10000.0

A_WIDTH = 1024
A_CHUNK = 128
A_GROUPS = 8
A_GDIM = A_WIDTH // A_GROUPS

NA_HEADS = 8
NA_HDIM = 128
NA_WIDTH = NA_HEADS * NA_HDIM
NA_KH = 8
NA_KW = 16

M_HEADS = 4
M_DK = 128
M_DV = 256
M_QK_W = M_HEADS * M_DK
M_V_W = M_HEADS * M_DV
M_CHUNK = 128
M_NGATE = 4 * M_HEADS

N_BRANCH = 3
BRANCH_W = 1024

N_EXPERTS = 16
N_GROUPS = 4
EXP_PER_GROUP = N_EXPERTS // N_GROUPS
D_FF_EXPERT = 1024

OFF_AU = 0
OFF_AV = OFF_AU + A_WIDTH
OFF_BQ = OFF_AV + A_WIDTH
OFF_BK = OFF_BQ + NA_WIDTH
OFF_BV = OFF_BK + NA_WIDTH
OFF_CQ = OFF_BV + NA_WIDTH
OFF_CK = OFF_CQ + M_QK_W
OFF_CV = OFF_CK + M_QK_W
OFF_CO = OFF_CV + M_V_W
OFF_CG = OFF_CO + M_V_W
OFF_GATE = OFF_CO + M_V_W

LANE = 128
VMEM_LIMIT = 56 * 1024 * 1024

ROW_SEG = 16
MOE_TM = 256
ROW_TM = 256
DMA_UNROLL = 8


def _cparams(sem):
    return pltpu.CompilerParams(dimension_semantics=sem, vmem_limit_bytes=VMEM_LIMIT)


def _dot(a, b):
    return jnp.dot(a, b, preferred_element_type=F32)


def _dot_nt(a, b):
    return lax.dot_general(a, b, (((1,), (1,)), ((), ())), preferred_element_type=F32)


def _sigmoid(x):
    return 1.0 / (1.0 + jnp.exp(-x))


def _gelu_tanh(x):
    return 0.5 * x * (1.0 + jnp.tanh(np.sqrt(2.0 / np.pi).astype(np.float32) * (x + 0.044715 * (x * x * x))))


def _rms(x, g):
    return x * lax.rsqrt(jnp.mean(x * x, axis=-1, keepdims=True) + EPS) * g


def _ada_kernel(c_ref, w_ref, b_ref, o_ref):
    cc = c_ref[...]
    s = cc * _sigmoid(cc)
    o_ref[...] = _dot(s.astype(BF16), w_ref[...].astype(BF16)) + b_ref[...]


def _ada(cc, w, b):
    rows, d = cc.shape
    n = w.shape[1]
    tn = 1024
    return pl.pallas_call(
        _ada_kernel,
        grid=(n // tn,),
        in_specs=[pl.BlockSpec((rows, d), lambda j: (0, 0)),
                  pl.BlockSpec((d, tn), lambda j: (0, j)),
                  pl.BlockSpec((1, tn), lambda j: (0, j))],
        out_specs=pl.BlockSpec((rows, tn), lambda j: (0, j)),
        out_shape=jax.ShapeDtypeStruct((rows, n), F32),
        compiler_params=_cparams(("arbitrary",)),
        name="ada_mod",
    )(cc, w, b)


def _inproj_kernel(x_ref, sh_ref, sc_ref, g_ref, w_ref, o_ref, h_ref):
    @pl.when(pl.program_id(1) == 0)
    def _():
        h = _rms(x_ref[...], g_ref[...]) * (1.0 + sc_ref[0]) + sh_ref[0]
        h_ref[...] = h.astype(BF16)

    o_ref[...] = _dot(h_ref[...], w_ref[...]).astype(o_ref.dtype)


def _group_of(i, tm, nl, t, b):
    return jnp.where(i * tm < nl, (i * tm) // t, b)


def _inproj(x, mod3, g, w, dims, tm, tn, out_dtype, row0=0, n_rows=None):
    nl, t, b = dims
    d = x.shape[1]
    n = w.shape[1]
    n_rows = x.shape[0] - row0 if n_rows is None else n_rows
    first = row0 // tm
    grp = lambda i: _group_of(i + first, tm=tm, nl=nl, t=t, b=b)
    return pl.pallas_call(
        _inproj_kernel,
        grid=(n_rows // tm, n // tn),
        in_specs=[pl.BlockSpec((tm, d), lambda i, j: (i + first, 0)),
                  pl.BlockSpec((1, 1, d), lambda i, j: (grp(i) * 6 + 0, 0, 0)),
                  pl.BlockSpec((1, 1, d), lambda i, j: (grp(i) * 6 + 1, 0, 0)),
                  pl.BlockSpec((1, d), lambda i, j: (0, 0)),
                  pl.BlockSpec((d, tn), lambda i, j: (0, j))],
        out_specs=pl.BlockSpec((tm, tn), lambda i, j: (i, j)),
        out_shape=jax.ShapeDtypeStruct((n_rows, n), out_dtype),
        scratch_shapes=[pltpu.VMEM((tm, d), BF16)],
        compiler_params=_cparams(("parallel", "arbitrary")),
        name="inproj",
    )(x, mod3, mod3, g, w)


def _sgu_kernel(u_ref, v_ref, g_ref, w_ref, b_ref, o_ref):
    n_chunk = u_ref.shape[0] // A_CHUNK
    for c in range(n_chunk):
        rows = slice(c * A_CHUNK, (c + 1) * A_CHUNK)
        u = _gelu_tanh(u_ref[rows, :].astype(F32))
        v = _rms(_gelu_tanh(v_ref[rows, :].astype(F32)), g_ref[...]).astype(BF16)
        for gi in range(A_GROUPS):
            cols = slice(gi * A_GDIM, (gi + 1) * A_GDIM)
            mixed = _dot(w_ref[gi], v[:, cols]) + b_ref[:, cols]
            o_ref[rows, cols] = (u[:, cols] * mixed).astype(o_ref.dtype)


def _sgu(p, g_sgu, w_sp, b_full, n_rows, tm):
    return pl.pallas_call(
        _sgu_kernel,
        grid=(n_rows // tm,),
        in_specs=[pl.BlockSpec((tm, A_WIDTH), lambda i: (i, OFF_AU // A_WIDTH)),
                  pl.BlockSpec((tm, A_WIDTH), lambda i: (i, OFF_AV // A_WIDTH)),
                  pl.BlockSpec((1, A_WIDTH), lambda i: (0, 0)),
                  pl.BlockSpec((A_GROUPS, A_CHUNK, A_CHUNK), lambda i: (0, 0, 0)),
                  pl.BlockSpec((A_CHUNK, A_WIDTH), lambda i: (0, 0))],
        out_specs=pl.BlockSpec((tm, A_WIDTH), lambda i: (i, 0)),
        out_shape=jax.ShapeDtypeStruct((n_rows, A_WIDTH), BF16),
        compiler_params=_cparams(("parallel",)),
        name="sgu",
    )(p, p, g_sgu, w_sp, b_full)


NA_GQ = 4


def _na_geometry(rows):
    kh = min(NA_KH, rows)
    gq = min(NA_GQ, rows)
    wr = min(gq + kh - 1, rows)
    starts = [int(np.clip(gq * g - kh // 2, 0, rows - wr)) for g in range(rows // gq)]
    return kh, gq, wr, starts


def _na_row_select(rows):
    kh, gq, wr, starts = _na_geometry(rows)
    n_rel = 2 * NA_KH - 1
    row_sel = np.zeros((len(starts), gq, wr, n_rel + 1), np.float32)
    for g, ws in enumerate(starts):
        for qi in range(gq):
            r = gq * g + qi
            rs = int(np.clip(r - kh // 2, 0, rows - kh))
            assert ws <= rs and rs + kh <= ws + wr
            for a in range(wr):
                inside = rs <= ws + a < rs + kh
                row_sel[g, qi, a, ws + a - r + NA_KH - 1 if inside else n_rel] = 1.0
    kinds, kind_of = [], []
    for g in range(len(starts)):
        same = [k for k, rep in enumerate(kinds) if np.array_equal(row_sel[rep], row_sel[g])]
        if not same:
            kinds.append(g)
        kind_of.append(same[0] if same else len(kinds) - 1)
    return row_sel[kinds], kind_of


def _na_bias_table(rpb, rows):
    kh, gq, wr, starts = _na_geometry(rows)
    n_rel = 2 * NA_KH - 1
    cq = np.arange(GRID_W)[:, None]
    kc = np.arange(GRID_W)[None, :]
    cs = np.clip(cq - NA_KW // 2, 0, GRID_W - NA_KW)
    valid = (kc >= cs) & (kc < cs + NA_KW)
    coff = np.clip(kc - cq, -(NA_KW - 1), NA_KW - 1) + NA_KW - 1
    col_sel = (coff[None, :, :] == np.arange(2 * NA_KW - 1)[:, None, None]).astype(np.float32)
    by_col = jnp.einsum('hrc,cqk->hrqk', rpb.astype(F32), col_sel, precision=lax.Precision.HIGHEST)
    by_col = jnp.where(valid[None, None], by_col, NEG_INF)
    by_col = jnp.concatenate([by_col, jnp.full_like(by_col[:, :1], NEG_INF)], axis=1)
    row_sel, _ = _na_row_select(rows)
    tab = jnp.einsum('gqar,hrck->hgqcak', row_sel, by_col, precision=lax.Precision.HIGHEST)
    return tab.reshape(rpb.shape[0], row_sel.shape[0], gq * GRID_W, wr * GRID_W)


def _softmax_pv(s_parts, v_parts):
    m = functools.reduce(jnp.maximum, [jnp.max(s, axis=-1, keepdims=True) for s in s_parts])
    p_parts = [jnp.exp(s - m) for s in s_parts]
    l = functools.reduce(jnp.add, [jnp.sum(p, axis=-1, keepdims=True) for p in p_parts])
    o = functools.reduce(jnp.add, [_dot(p.astype(BF16), v) for p, v in zip(p_parts, v_parts)])
    return o / l


def _natten_kernel(q_ref, k_ref, v_ref, qc_ref, kc_ref, vc_ref, tab_ref, o_ref, oc_ref, *, rows):
    scale = NA_HDIM ** -0.5
    _, gq, wr, starts = _na_geometry(rows)
    _, kind_of = _na_row_select(rows)
    kc = kc_ref[...]
    vc = vc_ref[...]
    for g, ws in enumerate(starts):
        qs = slice(g * gq * GRID_W, (g + 1) * gq * GRID_W)
        ks = slice(ws * GRID_W, (ws + wr) * GRID_W)
        q = q_ref[qs, :]
        s_w = _dot_nt(q, k_ref[ks, :]) * scale + tab_ref[0, kind_of[g]]
        s_c = _dot_nt(q, kc) * scale
        o_ref[qs, :] = _softmax_pv([s_w, s_c], [v_ref[ks, :], vc]).astype(o_ref.dtype)
    s = _dot_nt(qc_ref[...], kc) * scale
    oc_ref[...] = _softmax_pv([s], [vc]).astype(oc_ref.dtype)


def _natten(p, pc, tab, dims, ctx_len):
    nl, t, b = dims
    rows = t // GRID_W
    pc_arr, pc_row0, pc_col0 = pc
    cb = pc_row0 // ctx_len
    hd = NA_HDIM
    ctx_q = OFF_BQ if pc_col0 <= OFF_BQ else OFF_BK
    lat = lambda off: pl.BlockSpec((t, hd), lambda h, bi: (bi, off // hd + h))
    ctx = lambda off: pl.BlockSpec((ctx_len, hd), lambda h, bi: (cb + bi, (off - pc_col0) // hd + h))
    return pl.pallas_call(
        functools.partial(_natten_kernel, rows=rows),
        grid=(NA_HEADS, b),
        in_specs=[lat(OFF_BQ), lat(OFF_BK), lat(OFF_BV), ctx(ctx_q), ctx(OFF_BK), ctx(OFF_BV),
                  pl.BlockSpec((1,) + tab.shape[1:], lambda h, bi: (h, 0, 0, 0))],
        out_specs=[pl.BlockSpec((t, hd), lambda h, bi: (bi, h)),
                   pl.BlockSpec((ctx_len, hd), lambda h, bi: (bi, h))],
        out_shape=[jax.ShapeDtypeStruct((nl, NA_WIDTH), BF16),
                   jax.ShapeDtypeStruct((b * ctx_len, NA_WIDTH), BF16)],
        compiler_params=_cparams(("parallel", "arbitrary")),
        name="natten",
    )(p, p, p, pc_arr, pc_arr, pc_arr, tab)


def _scan_rows(x, reverse, op, fill):
    n = x.shape[0]
    idx = lax.broadcasted_iota(jnp.int32, x.shape, 0)
    step = 1
    while step < n:
        if reverse:
            x = op(x, jnp.where(idx < n - step, pltpu.roll(x, n - step, 0), fill))
        else:
            x = op(x, jnp.where(idx >= step, pltpu.roll(x, step, 0), fill))
        step *= 2
    return x


def _log_sigmoid(x):
    return jnp.minimum(x, 0.0) - jnp.log(1.0 + jnp.exp(-jnp.abs(x)))


def _rope_swap(x):
    lane = lax.broadcasted_iota(jnp.int32, x.shape, 1)
    return jnp.where(lane % 64 < 32, pltpu.roll(x, LANE - 32, 1), pltpu.roll(x, 32, 1))


def _chunk_rows(c):
    return pl.ds(pl.multiple_of(c * M_CHUNK, M_CHUNK), M_CHUNK)


def _gate_rows(c):
    return pl.ds(pl.multiple_of(c * 8, 8), 4)


def _mlstm_prepare(seq, bm_ref, rope):
    shape = (M_CHUNK, LANE)

    def body(c, carry):
        rows = _chunk_rows(c)
        gc = seq["g"][rows, :] + bm_ref[...]
        lf = _log_sigmoid(gc)
        xs = []
        for d in range(2):
            reverse = d == 1
            b = _scan_rows(lf, reverse, jnp.add, 0.0)
            a_rep = jnp.broadcast_to(b[:, 2 * d + 1:2 * d + 2], shape)
            x_rep = jnp.broadcast_to(gc[:, 2 * d:2 * d + 1], shape) - a_rep
            seq["a"][d, rows, :] = a_rep
            seq["x"][d, rows, :] = x_rep
            seq["mi"][d, rows, :] = a_rep + _scan_rows(x_rep, reverse, jnp.maximum, -jnp.inf)
            seq["row"][pl.ds(c * 8 + 2 * d + 1, 1), :] = jnp.max(x_rep, axis=0, keepdims=True)
            xs.append(x_rep)
        lane = lax.broadcasted_iota(jnp.int32, shape, 1)
        xt = jnp.where(lane < LANE // 2, xs[0], xs[1]).T
        seq["row"][pl.ds(c * 8, 1), :] = xt[0:1, :]
        seq["row"][pl.ds(c * 8 + 2, 1), :] = xt[LANE // 2:LANE // 2 + 1, :]
        q = seq["q"][rows, :].astype(F32)
        k = seq["k"][rows, :].astype(F32)
        if rope is not None:
            cos = rope[0][rows, :]
            sin = rope[1][rows, :]
            q = q * cos + _rope_swap(q) * sin
            k = k * cos + _rope_swap(k) * sin
        seq["qs"][rows, :] = (q * (M_DK ** -0.5)).astype(BF16)
        seq["ks"][rows, :] = k
        return carry

    lax.fori_loop(0, seq["n"], body, 0)


def _mlstm_step(seq, c, reverse, state, order):
    c_ref, m_ref = state
    d = 1 if reverse else 0
    ln = M_CHUNK
    rows = _chunk_rows(c)
    a = seq["a"][d, rows, :]
    x = seq["x"][d, rows, :]
    gate_rows = seq["row"][_gate_rows(c), :]
    x_row, x_max = gate_rows[2 * d:2 * d + 1, :], gate_rows[2 * d + 1:2 * d + 2, :]
    q = seq["qs"][rows, :]
    k = seq["ks"][rows, :]
    vx = jnp.concatenate([seq["v"][rows, :], jnp.ones((ln, LANE), BF16)], axis=1)
    m_prev = m_ref[d]
    c_prev = c_ref[d]

    inter = a + m_prev
    m_row = jnp.maximum(inter, seq["mi"][d, rows, :])
    s = _dot_nt(q, k.astype(BF16)) * jnp.exp(jnp.where(order, a + x_row, -jnp.inf) - m_row)
    ew = jnp.exp(inter - m_row)
    tot = _dot(s.astype(BF16), vx) + jnp.concatenate([ew] * 3, axis=1) * _dot(q, c_prev.astype(BF16))
    inv = 1.0 / jnp.maximum(jnp.abs(tot[:, M_DV:]), jnp.exp(-m_row))
    seq["hb" if reverse else "hf"][rows, :] = tot[:, :M_DV] * jnp.concatenate([inv] * 2, axis=1)

    b_last = a[0:1, :] if reverse else a[ln - 1:ln, :]
    m_new = jnp.maximum(b_last + m_prev, b_last + x_max)
    decay = jnp.exp(b_last + m_prev - m_new)
    kw = k * jnp.exp(b_last + x - m_new)
    c_ref[d] = jnp.concatenate([decay] * 3, axis=1) * c_prev + _dot(kw.T.astype(BF16), vx)
    m_ref[d] = m_new


def _mlstm_scan(seq, state):
    n = seq["n"]
    t_idx = lax.broadcasted_iota(jnp.int32, (M_CHUNK, M_CHUNK), 0)
    s_idx = lax.broadcasted_iota(jnp.int32, (M_CHUNK, M_CHUNK), 1)

    def body(ci, carry):
        _mlstm_step(seq, ci, False, state, t_idx >= s_idx)
        _mlstm_step(seq, n - 1 - ci, True, state, s_idx >= t_idx)
        return carry

    lax.fori_loop(0, n, body, 0, unroll=min(4, n))


def _mlstm_finish(seq, gn_ref):
    def body(c, carry):
        rows = _chunk_rows(c)
        y = _rms(seq["hf"][rows, :] + seq["hb"][rows, :], gn_ref[...])
        seq["y"][rows, :] = (y * _sigmoid(seq["o"][rows, :].astype(F32))).astype(seq["y"].dtype)
        return carry

    lax.fori_loop(0, seq["n"], body, 0)


_SEQ_IN = ("q", "k", "v", "o", "g")
_SEQ_SCRATCH = ("qs", "ks", "a", "x", "mi", "row", "hf", "hb")


def _mlstm_kernel(*refs):
    lat = dict(zip(_SEQ_IN, refs[0:5]))
    ctx = dict(zip(_SEQ_IN, refs[5:10]))
    bm_ref, gn_ref, cos_ref, sin_ref = refs[10:14]
    lat["y"], ctx["y"] = refs[14:16]
    lat.update(zip(_SEQ_SCRATCH, refs[16:24]))
    ctx.update(zip(_SEQ_SCRATCH, refs[24:32]))
    state = refs[32:34]
    for seq in (lat, ctx):
        seq["n"] = seq["q"].shape[0] // M_CHUNK

    _mlstm_prepare(ctx, bm_ref, None)
    _mlstm_prepare(lat, bm_ref, (cos_ref, sin_ref))
    for ref in state:
        ref[...] = jnp.zeros_like(ref)
    _mlstm_scan(ctx, state)
    _mlstm_scan(lat, state)
    _mlstm_finish(ctx, gn_ref)
    _mlstm_finish(lat, gn_ref)


def _rope_tables(t):
    pos = np.arange(t)
    half = M_DK // 4
    inv = ROPE_THETA ** (-np.arange(half, dtype=np.float64) / half)
    ang_r = (pos // GRID_W)[:, None] * inv[None, :]
    ang_c = (pos % GRID_W)[:, None] * inv[None, :]
    ang = np.concatenate([ang_r, ang_r, ang_c, ang_c], axis=1)
    sign = np.tile(np.concatenate([-np.ones(half), np.ones(half)]), 2)[None, :]
    return jnp.asarray(np.cos(ang), F32), jnp.asarray(np.sin(ang) * sign, F32)


def _seq_scratch(n):
    col = pltpu.VMEM((2, n, LANE), F32)
    return [pltpu.VMEM((n, M_DK), BF16), pltpu.VMEM((n, M_DK), F32), col, col, col,
            pltpu.VMEM((n // M_CHUNK * 8, LANE), F32), pltpu.VMEM((n, M_DV), F32), pltpu.VMEM((n, M_DV), F32)]


def _mlstm(p, pc, g, bm, gn, dims, ctx_len):
    nl, t, b = dims
    pc_arr, pc_row0, pc_col0 = pc
    cb = pc_row0 // ctx_len
    gb = nl // ctx_len
    ctx_o = OFF_CO if pc_arr.shape[1] + pc_col0 >= OFF_CO + M_V_W else OFF_CV
    cos, sin = _rope_tables(t)
    dk, dv = M_DK, M_DV
    lat = lambda off, w: pl.BlockSpec((t, w), lambda bi, h: (bi, off // w + h))
    ctx = lambda off, w: pl.BlockSpec((ctx_len, w), lambda bi, h: (cb + bi, (off - pc_col0) // w + h))
    const = lambda shape: pl.BlockSpec(shape, lambda bi, h: (0, 0))
    return pl.pallas_call(
        _mlstm_kernel,
        grid=(b, M_HEADS),
        in_specs=[lat(OFF_CQ, dk), lat(OFF_CK, dk), lat(OFF_CV, dv), lat(OFF_CO, dv),
                  pl.BlockSpec((t, LANE), lambda bi, h: (bi, h)),
                  ctx(OFF_CQ, dk), ctx(OFF_CK, dk), ctx(OFF_CV, dv), ctx(ctx_o, dv),
                  pl.BlockSpec((ctx_len, LANE), lambda bi, h: (gb + bi, h)),
                  pl.BlockSpec((1, LANE), lambda bi, h: (0, h)),
                  pl.BlockSpec((1, dv), lambda bi, h: (0, h)),
                  const((t, LANE)), const((t, LANE))],
        out_specs=[pl.BlockSpec((t, dv), lambda bi, h: (bi, h)),
                   pl.BlockSpec((ctx_len, dv), lambda bi, h: (bi, h))],
        out_shape=[jax.ShapeDtypeStruct((nl, M_V_W), BF16),
                   jax.ShapeDtypeStruct((b * ctx_len, M_V_W), BF16)],
        scratch_shapes=_seq_scratch(t) + _seq_scratch(ctx_len) + [
            pltpu.VMEM((2, dk, dv + LANE), F32), pltpu.VMEM((2, 1, LANE), F32)],
        compiler_params=_cparams(("parallel", "arbitrary")),
        name="mlstm",
    )(p, p, p, p, g, pc_arr, pc_arr, pc_arr, pc_arr, g, bm, gn, cos, sin)


def _merge_kernel(ya_ref, yb_ref, yc_ref, g0_ref, g1_ref, g2_ref, w_ref, o_ref):
    acc = _sigmoid(g0_ref[...].astype(F32)) * _dot(ya_ref[...], w_ref[0])
    acc = acc + _sigmoid(g1_ref[...].astype(F32)) * _dot(yb_ref[...], w_ref[1])
    acc = acc + _sigmoid(g2_ref[...].astype(F32)) * _dot(yc_ref[...], w_ref[2])
    o_ref[...] = acc.astype(o_ref.dtype)


def _merge(ya, yb, yc, p, w_branch, n_rows, tm, tn):
    d = w_branch.shape[2]
    br = lambda: pl.BlockSpec((tm, BRANCH_W), lambda i, j: (i, 0))
    gate = lambda k: pl.BlockSpec((tm, tn), lambda i, j: (i, (OFF_GATE + k * d) // tn + j))
    return pl.pallas_call(
        _merge_kernel,
        grid=(n_rows // tm, d // tn),
        in_specs=[br(), br(), br(), gate(0), gate(1), gate(2),
                  pl.BlockSpec((N_BRANCH, BRANCH_W, tn), lambda i, j: (0, 0, j),
                               pipeline_mode=pl.Buffered(1) if tn == d else None)],
        out_specs=pl.BlockSpec((tm, tn), lambda i, j: (i, j)),
        out_shape=jax.ShapeDtypeStruct((n_rows, d), BF16),
        compiler_params=_cparams(("parallel", "arbitrary")),
        name="merge",
    )(ya, yb, yc, p, p, p, w_branch)


def _route(scores, sel):
    row = lambda a, e: a[e:e + 1, :]
    best = None
    for gi in range(N_GROUPS):
        a, b, c, d = (row(sel, gi * EXP_PER_GROUP + j) for j in range(EXP_PER_GROUP))
        hi1, lo1 = jnp.maximum(a, b), jnp.minimum(a, b)
        hi2, lo2 = jnp.maximum(c, d), jnp.minimum(c, d)
        gs = jnp.maximum(hi1, hi2) + jnp.maximum(jnp.minimum(hi1, hi2), jnp.maximum(lo1, lo2))
        if best is None:
            best, g_idx = gs, jnp.zeros_like(gs, dtype=jnp.int32)
        else:
            take = gs > best
            best = jnp.where(take, gs, best)
            g_idx = jnp.where(take, gi, g_idx)

    def in_group(a, j):
        out = row(a, j)
        for gi in range(1, N_GROUPS):
            out = jnp.where(g_idx == gi, row(a, gi * EXP_PER_GROUP + j), out)
        return out

    v = [in_group(sel, j) for j in range(EXP_PER_GROUP)]
    u = [in_group(scores, j) for j in range(EXP_PER_GROUP)]

    def argmax_first(vals):
        bv, bi, bu = vals[0], jnp.zeros_like(g_idx), u[0]
        for j in range(1, EXP_PER_GROUP):
            take = vals[j] > bv
            bv = jnp.where(take, vals[j], bv)
            bi = jnp.where(take, j, bi)
            bu = jnp.where(take, u[j], bu)
        return bi, bu

    i1, s1 = argmax_first(v)
    i2, s2 = argmax_first([jnp.where(i1 == j, -jnp.inf, v[j]) for j in range(EXP_PER_GROUP)])
    tot = s1 + s2
    return g_idx * EXP_PER_GROUP + i1, g_idx * EXP_PER_GROUP + i2, s1 / tot, s2 / tot


def _to_tall(ref, val):
    rows = val.shape[0]
    for j in range(ROW_SEG):
        ref[pl.ds(j, rows, stride=ROW_SEG), :] = val[:, j * LANE:(j + 1) * LANE]


def _from_tall(ref):
    rows = ref.shape[0] // ROW_SEG
    return jnp.concatenate([ref[pl.ds(j, rows, stride=ROW_SEG), :] for j in range(ROW_SEG)], axis=1)


def _outproj_kernel(mg_ref, x_ref, gt_ref, sh_ref, sc_ref, g_ref, w_ref, wr_ref, br_ref, xo_ref, h_ref, r_ref):
    y = _dot(mg_ref[...], w_ref[...])
    xn = x_ref[...] + gt_ref[0] * y
    xo_ref[...] = xn
    h = _rms(xn, g_ref[...]) * (1.0 + sc_ref[0]) + sh_ref[0]
    _to_tall(h_ref, h)
    scores = _sigmoid(_dot_nt(wr_ref[...], h.astype(BF16)))
    e1, e2, w1, w2 = _route(scores, scores + br_ref[...])
    r_ref[...] = jnp.zeros_like(r_ref)
    r_ref[0:1, :] = e1.astype(F32)
    r_ref[1:2, :] = e2.astype(F32)
    r_ref[2:3, :] = w1
    r_ref[3:4, :] = w2


def _outproj(merged, x, mod3, g2, w_out, w_rt, b_r, dims, n_rows, tm):
    nl, t, b = dims
    d = x.shape[1]
    grp = functools.partial(_group_of, tm=tm, nl=nl, t=t, b=b)
    mrow = lambda k: pl.BlockSpec((1, 1, d), lambda i: (grp(i) * 6 + k, 0, 0))
    return pl.pallas_call(
        _outproj_kernel,
        grid=(n_rows // tm,),
        in_specs=[pl.BlockSpec((tm, d), lambda i: (i, 0)),
                  pl.BlockSpec((tm, d), lambda i: (i, 0)),
                  mrow(2), mrow(3), mrow(4),
                  pl.BlockSpec((1, d), lambda i: (0, 0)),
                  pl.BlockSpec((d, d), lambda i: (0, 0), pipeline_mode=pl.Buffered(1)),
                  pl.BlockSpec((N_EXPERTS, d), lambda i: (0, 0)),
                  pl.BlockSpec((N_EXPERTS, 1), lambda i: (0, 0))],
        out_specs=[pl.BlockSpec((tm, d), lambda i: (i, 0)),
                   pl.BlockSpec((tm * ROW_SEG, LANE), lambda i: (i, 0)),
                   pl.BlockSpec((8, tm), lambda i: (0, i))],
        out_shape=[jax.ShapeDtypeStruct((n_rows, d), F32),
                   jax.ShapeDtypeStruct((n_rows * ROW_SEG, LANE), F32),
                   jax.ShapeDtypeStruct((8, n_rows), F32)],
        compiler_params=_cparams(("parallel",)),
        name="outproj_route",
    )(merged, x, mod3, mod3, mod3, g2, w_out, w_rt, b_r)


def _slot_plan(route, tm, max_tiles):
    n = route.shape[1]
    e = route[0:2, :].astype(jnp.int32).reshape(-1)
    oh = (e[:, None] == jnp.arange(N_EXPERTS, dtype=jnp.int32)[None, :]).astype(jnp.int32)
    cs = jnp.cumsum(oh, axis=0)
    rank = jnp.sum(cs * oh, axis=1) - 1
    counts = cs[-1]
    padded = ((counts + tm - 1) // tm) * tm
    ends = jnp.cumsum(padded)
    slot = jnp.sum(oh * (ends - padded)[None, :], axis=1) + rank
    tile_ends = ends // tm
    n_used = tile_ends[-1]
    tiles = jnp.arange(max_tiles, dtype=jnp.int32)
    src = jnp.minimum(tiles, n_used - 1)
    tile_e = jnp.minimum(jnp.sum((src[:, None] >= tile_ends[None, :]).astype(jnp.int32), axis=1), N_EXPERTS - 1)
    left = jnp.take(counts, tile_e) - (tiles - jnp.take(tile_ends - padded // tm, tile_e)) * tm
    n_valid = jnp.where(tiles < n_used, jnp.clip(left, 0, tm), 0)
    tok_of_slot, dst_of_slot = _invert_slots(slot.astype(jnp.int32), n, max_tiles * tm)
    wts = route[2:4, :].T
    i32 = lambda a: a.astype(jnp.int32)
    return tok_of_slot, dst_of_slot, i32(tile_e), i32(n_valid), i32(n_used.reshape(1)), wts


def _invert_kernel(groups_ref, slot_ref, tok_ref, dst_ref, *, n):
    def clear(g, carry):
        for u in range(DMA_UNROLL):
            tok_ref[g * DMA_UNROLL + u] = 0
            dst_ref[g * DMA_UNROLL + u] = 0
        return carry

    def put(g, carry):
        for u in range(DMA_UNROLL):
            f = g * DMA_UNROLL + u
            s = slot_ref[f]
            tok_ref[s] = jnp.where(f >= n, f - n, f)
            dst_ref[s] = f
        return carry

    lax.fori_loop(0, groups_ref[0], clear, 0)
    lax.fori_loop(0, groups_ref[1], put, 0)


def _invert_slots(slot, n, n_slots):
    assert n_slots % DMA_UNROLL == 0 and (2 * n) % DMA_UNROLL == 0
    smem = pl.BlockSpec(memory_space=pltpu.SMEM)
    out = jax.ShapeDtypeStruct((n_slots,), jnp.int32)
    groups = jnp.array([n_slots // DMA_UNROLL, 2 * n // DMA_UNROLL], jnp.int32)
    return pl.pallas_call(
        functools.partial(_invert_kernel, n=n),
        in_specs=[smem, smem], out_specs=[smem, smem], out_shape=[out, out],
        name="moe_invert_slots",
    )(groups, slot)


def _row_copy(src_ref, src_row, dst_ref, dst_row, sem):
    src = src_ref.at[pl.ds(pl.multiple_of(src_row * ROW_SEG, ROW_SEG), ROW_SEG), :]
    dst = dst_ref.at[pl.ds(pl.multiple_of(dst_row * ROW_SEG, ROW_SEG), ROW_SEG), :]
    return pltpu.make_async_copy(src, dst, sem)


def _expert_kernel(tok_ref, dst_ref, te_ref, nv_ref, nu_ref, h_ref, wg_ref, wu_ref, wd_ref, y_ref,
                   xbuf, ybuf, gsem, ssem, *, tm):
    del te_ref
    i = pl.program_id(0)
    n_used = nu_ref[0]
    used = i < n_used
    cur = i % 2

    def for_rows(n_rows, body):
        def group(g, carry):
            for u in range(DMA_UNROLL):
                body(g * DMA_UNROLL + u)
            return carry

        def single(j, carry):
            body(j)
            return carry

        n_groups = n_rows // DMA_UNROLL
        lax.fori_loop(0, n_groups, group, 0)
        if not isinstance(n_rows, int) or n_rows % DMA_UNROLL:
            lax.fori_loop(n_groups * DMA_UNROLL, n_rows, single, 0)

    def wait_rows(buf_ref, n_rows, sem):
        part = buf_ref.at[pl.ds(0, n_rows * ROW_SEG), :]
        pltpu.make_async_copy(part, part, sem).wait()

    def gather(tile, buf, wait):
        if wait:
            wait_rows(xbuf.at[buf], tm, gsem.at[buf])
        else:
            for_rows(tm, lambda j: _row_copy(h_ref, tok_ref[tile * tm + j], xbuf.at[buf], j,
                                             gsem.at[buf]).start(priority=1))

    def scatter(tile, buf, wait):
        if wait:
            wait_rows(ybuf.at[buf], nv_ref[tile], ssem.at[buf])
        else:
            for_rows(nv_ref[tile], lambda j: _row_copy(ybuf.at[buf], j, y_ref, dst_ref[tile * tm + j],
                                                       ssem.at[buf]).start())

    @pl.when(i == 0)
    def _():
        gather(0, 0, False)

    @pl.when(i + 1 < n_used)
    def _():
        gather(i + 1, 1 - cur, False)

    @pl.when(used)
    def _():
        gather(i, cur, True)
        x = _from_tall(xbuf.at[cur]).astype(BF16)
        a = _dot(x, wg_ref[0])
        he = a * _sigmoid(a) * _dot(x, wu_ref[0])
        _to_tall(ybuf.at[cur], _dot(he.astype(BF16), wd_ref[0]))

    @pl.when(jnp.logical_and(i >= 1, i - 1 < n_used))
    def _():
        scatter(i - 1, 1 - cur, True)

    @pl.when(used)
    def _():
        scatter(i, cur, False)

    @pl.when(jnp.logical_and(i == pl.num_programs(0) - 1, used))
    def _():
        scatter(i, cur, True)


def _experts(h, tok_of_slot, dst_of_slot, tile_e, n_valid, n_used, wg, wu, wd, tm):
    d, f = wg.shape[1:]
    wspec = lambda shape: pl.BlockSpec(shape, lambda i, t, s, e, v, u: (e[i], 0, 0))
    buf = pltpu.VMEM((2, tm * ROW_SEG, LANE), F32)
    return pl.pallas_call(
        functools.partial(_expert_kernel, tm=tm),
        grid_spec=pltpu.PrefetchScalarGridSpec(
            num_scalar_prefetch=5,
            grid=(tile_e.shape[0],),
            in_specs=[pl.BlockSpec(memory_space=pl.ANY), wspec((1, d, f)), wspec((1, d, f)), wspec((1, f, d))],
            out_specs=pl.BlockSpec(memory_space=pl.ANY),
            scratch_shapes=[buf, buf, pltpu.SemaphoreType.DMA((2,)), pltpu.SemaphoreType.DMA((2,))]),
        out_shape=jax.ShapeDtypeStruct((2 * h.shape[0], LANE), F32),
        compiler_params=_cparams(("arbitrary",)),
        name="moe_experts",
    )(tok_of_slot, dst_of_slot, tile_e, n_valid, n_used, h, wg, wu, wd)


def _combine_kernel(x_ref, w_ref, gt_ref, gf_ref, y1_ref, y2_ref, o_ref, *, final):
    w = w_ref[...]
    f = w[:, 0:1] * _from_tall(y1_ref) + w[:, 1:2] * _from_tall(y2_ref)
    xo = x_ref[...] + gt_ref[0] * f
    if final:
        xo = _rms(xo, gf_ref[...])
    o_ref[...] = xo


def _combine(x, wts, mod3, g_final, y, dims, n_tok, tm, final):
    nl, t, b = dims
    d = x.shape[1]
    grp = functools.partial(_group_of, tm=tm, nl=nl, t=t, b=b)
    return pl.pallas_call(
        functools.partial(_combine_kernel, final=final),
        grid=(n_tok // tm,),
        in_specs=[pl.BlockSpec((tm, d), lambda i: (i, 0)),
                  pl.BlockSpec((tm, 2), lambda i: (i, 0)),
                  pl.BlockSpec((1, 1, d), lambda i: (grp(i) * 6 + 5, 0, 0)),
                  pl.BlockSpec((1, d), lambda i: (0, 0)),
                  pl.BlockSpec((tm * ROW_SEG, LANE), lambda i: (i, 0)),
                  pl.BlockSpec((tm * ROW_SEG, LANE), lambda i: (n_tok // tm + i, 0))],
        out_specs=pl.BlockSpec((tm, d), lambda i: (i, 0)),
        out_shape=jax.ShapeDtypeStruct((n_tok, d), F32),
        compiler_params=_cparams(("parallel",)),
        name="moe_combine",
    )(x, wts, mod3, g_final, y, y)


def _moe(h2, route, xn, mod3, g_final, wg, wu, wd, dims, final):
    n_tok = xn.shape[0]
    max_tiles = (2 * n_tok) // MOE_TM + N_EXPERTS
    tok_of_slot, dst_of_slot, tile_e, n_valid, n_used, wts = _slot_plan(route, MOE_TM, max_tiles)
    y = _experts(h2, tok_of_slot, dst_of_slot, tile_e, n_valid, n_used, wg, wu, wd, MOE_TM)
    nl, t, _ = dims
    tm = 2 * ROW_TM if t % (2 * ROW_TM) == 0 and (n_tok - nl) % (2 * ROW_TM) == 0 else ROW_TM
    return _combine(xn, wts, mod3, g_final, y, dims, n_tok, tm, final)


def _gate_weight(w_cg):
    d = w_cg.shape[0]
    per_head = w_cg.reshape(d, 4, M_HEADS).transpose(0, 2, 1)
    return jnp.pad(per_head, ((0, 0), (0, 0), (0, LANE - 4))).reshape(d, M_HEADS * LANE)


def _layer(x, mod3, dims, ctx_len, last, g1, g2, w_in, g_sgu, w_sp, b_sp, rpb, b_mgate, g_mnorm, w_branch, w_out,
           w_rt, b_r, wg, wu, wd, g_final):
    nl, t, b = dims
    nt, d = x.shape
    fits = lambda rows: t % rows == 0 and (nt - nl) % rows == 0
    tm = 512 if fits(512) else 256
    tm_proj = 1024 if fits(1024) else tm

    w_main = jnp.concatenate([w_in[:, :OFF_CG], w_in[:, OFF_CG + M_NGATE:]], axis=1).astype(BF16)
    w_gate = _gate_weight(w_in[:, OFF_CG:OFF_CG + M_NGATE]).astype(BF16)
    g = _inproj(x, mod3, g1, w_gate, dims, tm_proj, M_HEADS * LANE, F32)
    if last:
        p = _inproj(x, mod3, g1, w_main, dims, tm_proj, 1024, BF16, 0, nl)
        pc = (_inproj(x, mod3, g1, w_main[:, OFF_BK:OFF_CO], dims, tm_proj, 1024, BF16, nl, nt - nl), 0, OFF_BK)
    else:
        p = _inproj(x, mod3, g1, w_main, dims, tm_proj, 1024, BF16)
        pc = (p, nl, 0)

    n_rows = nl if last else nt
    b_full = jnp.repeat(b_sp.T, A_GDIM, axis=1)
    ya = _sgu(p, g_sgu[None, :], w_sp.astype(BF16), b_full, n_rows, tm)

    yb_l, yb_c = _natten(p, pc, _na_bias_table(rpb, t // GRID_W), dims, ctx_len)

    bm = jnp.pad(b_mgate.T, ((0, 0), (0, LANE - 4))).reshape(1, M_HEADS * LANE)
    yc_l, yc_c = _mlstm(p, pc, g, bm, g_mnorm[None, :], dims, ctx_len)

    if last:
        yb, yc = yb_l, yc_l
    else:
        yb = jnp.concatenate([yb_l, yb_c], axis=0)
        yc = jnp.concatenate([yc_l, yc_c], axis=0)

    merged = _merge(ya, yb, yc, p, w_branch.astype(BF16), n_rows, tm, d)
    xn, h2, route = _outproj(merged, x, mod3, g2[None, :], w_out.astype(BF16), w_rt, b_r, dims, n_rows, tm)
    return _moe(h2, route, xn, mod3, g_final, wg.astype(BF16), wu.astype(BF16), wd.astype(BF16), dims, last)


def kernel(x, c, ctx, c_ctx, w_ada, b_ada, g_norm1, g_norm2, w_in, g_sgu, w_spatial, b_spatial, na_rpb, b_mgate,
           g_mnorm, w_branch, w_out, w_router, b_router, w_e_gate, w_e_up, w_e_down, g_final):
    b, t, d = x.shape
    ctx_len = ctx.shape[1]
    nl = b * t
    dims = (nl, t, b)
    depth = w_ada.shape[0]
    xs = jnp.concatenate([x.reshape(nl, d), ctx.reshape(b * ctx_len, d)], axis=0)
    n_mod = -(-(b + 1) // 8) * 8
    cc = jnp.zeros((n_mod, d), F32).at[:b].set(c).at[b].set(c_ctx)
    w_rt = w_router.T.astype(BF16)
    b_r = b_router[:, None].astype(F32)
    w_in_bf = lax.optimization_barrier(w_in.astype(BF16))
    for layer in range(depth):
        mod = _ada(cc, w_ada[layer], b_ada[layer][None, :])
        mod3 = mod.reshape(n_mod * 6, 1, d)
        xs = _layer(xs, mod3, dims, ctx_len, layer == depth - 1, g_norm1[layer][None, :], g_norm2[layer],
                    w_in_bf[layer], g_sgu[layer], w_spatial[layer], b_spatial[layer], na_rpb[layer], b_mgate[layer],
                    g_mnorm[layer], w_branch[layer], w_out[layer], w_rt, b_r, w_e_gate[layer], w_e_up[layer],
                    w_e_down[layer], g_final[None, :])
    return xs.reshape(b, t, d)
```

```python
import functools

import numpy as np
import jax
import jax.numpy as jnp
from jax import lax
from jax.experimental import pallas as pl
from jax.experimental.pallas import tpu as pltpu

F32 = jnp.float32
BF16 = jnp.bfloat16

GRID_W = 64
EPS = 1e-6
NEG_INF = -1e30
ROPE_THETA = 10000.0

A_WIDTH = 1024
A_CHUNK = 128
A_GROUPS = 8
A_GDIM = A_WIDTH // A_GROUPS

NA_HEADS = 8
NA_HDIM = 128
NA_WIDTH = NA_HEADS * NA_HDIM
NA_KH = 8
NA_KW = 16

M_HEADS = 4
M_DK = 128
M_DV = 256
M_QK_W = M_HEADS * M_DK
M_V_W = M_HEADS * M_DV
M_CHUNK = 128
M_NGATE = 4 * M_HEADS

N_BRANCH = 3
BRANCH_W = 1024

N_EXPERTS = 16
N_GROUPS = 4
EXP_PER_GROUP = N_EXPERTS // N_GROUPS
D_FF_EXPERT = 1024

OFF_AU = 0
OFF_AV = OFF_AU + A_WIDTH
OFF_BQ = OFF_AV + A_WIDTH
OFF_BK = OFF_BQ + NA_WIDTH
OFF_BV = OFF_BK + NA_WIDTH
OFF_CQ = OFF_BV + NA_WIDTH
OFF_CK = OFF_CQ + M_QK_W
OFF_CV = OFF_CK + M_QK_W
OFF_CO = OFF_CV + M_V_W
OFF_CG = OFF_CO + M_V_W
OFF_GATE = OFF_CO + M_V_W

LANE = 128
VMEM_LIMIT = 56 * 1024 * 1024

ROW_SEG = 16
MOE_TM = 256
ROW_TM = 256
DMA_UNROLL = 8


def _cparams(sem):
    return pltpu.CompilerParams(dimension_semantics=sem, vmem_limit_bytes=VMEM_LIMIT)


def _dot(a, b):
    return jnp.dot(a, b, preferred_element_type=F32)


def _dot_nt(a, b):
    return lax.dot_general(a, b, (((1,), (1,)), ((), ())), preferred_element_type=F32)


def _sigmoid(x):
    return 1.0 / (1.0 + jnp.exp(-x))


def _gelu_tanh(x):
    return 0.5 * x * (1.0 + jnp.tanh(np.sqrt(2.0 / np.pi).astype(np.float32) * (x + 0.044715 * (x * x * x))))


def _rms(x, g):
    return x * lax.rsqrt(jnp.mean(x * x, axis=-1, keepdims=True) + EPS) * g


def _ada_kernel(c_ref, w_ref, b_ref, o_ref):
    cc = c_ref[...]
    s = cc * _sigmoid(cc)
    o_ref[...] = _dot(s.astype(BF16), w_ref[0].astype(BF16)) + b_ref[...]


def _ada(cc, w, layer, b):
    rows, d = cc.shape
    n = w.shape[2]
    tn = 1024
    return pl.pallas_call(
        _ada_kernel,
        grid=(n // tn,),
        in_specs=[pl.BlockSpec((rows, d), lambda j: (0, 0)),
                  pl.BlockSpec((1, d, tn), lambda j: (layer, 0, j)),
                  pl.BlockSpec((1, tn), lambda j: (0, j))],
        out_specs=pl.BlockSpec((rows, tn), lambda j: (0, j)),
        out_shape=jax.ShapeDtypeStruct((rows, n), F32),
        compiler_params=_cparams(("arbitrary",)),
        name="ada_mod",
    )(cc, w, b)


def _inproj_kernel(x_ref, sh_ref, sc_ref, g_ref, w_ref, o_ref, h_ref):
    @pl.when(pl.program_id(1) == 0)
    def _():
        h = _rms(x_ref[...], g_ref[...]) * (1.0 + sc_ref[0]) + sh_ref[0]
        h_ref[...] = h.astype(BF16)

    o_ref[...] = _dot(h_ref[...], w_ref[...]).astype(o_ref.dtype)


def _group_of(i, tm, nl, t, b):
    return jnp.where(i * tm < nl, (i * tm) // t, b)


def _inproj(x, mod3, g, w, dims, tm, tn, out_dtype, row0=0, n_rows=None):
    nl, t, b = dims
    d = x.shape[1]
    n = w.shape[1]
    n_rows = x.shape[0] - row0 if n_rows is None else n_rows
    first = row0 // tm
    grp = lambda i: _group_of(i + first, tm=tm, nl=nl, t=t, b=b)
    return pl.pallas_call(
        _inproj_kernel,
        grid=(n_rows // tm, n // tn),
        in_specs=[pl.BlockSpec((tm, d), lambda i, j: (i + first, 0)),
                  pl.BlockSpec((1, 1, d), lambda i, j: (grp(i) * 6 + 0, 0, 0)),
                  pl.BlockSpec((1, 1, d), lambda i, j: (grp(i) * 6 + 1, 0, 0)),
                  pl.BlockSpec((1, d), lambda i, j: (0, 0)),
                  pl.BlockSpec((d, tn), lambda i, j: (0, j))],
        out_specs=pl.BlockSpec((tm, tn), lambda i, j: (i, j)),
        out_shape=jax.ShapeDtypeStruct((n_rows, n), out_dtype),
        scratch_shapes=[pltpu.VMEM((tm, d), BF16)],
        compiler_params=_cparams(("parallel", "arbitrary")),
        name="inproj",
    )(x, mod3, mod3, g, w)


def _sgu_kernel(u_ref, v_ref, g_ref, w_ref, b_ref, o_ref):
    n_chunk = u_ref.shape[0] // A_CHUNK
    for c in range(n_chunk):
        rows = slice(c * A_CHUNK, (c + 1) * A_CHUNK)
        u = _gelu_tanh(u_ref[rows, :].astype(F32))
        v = _rms(_gelu_tanh(v_ref[rows, :].astype(F32)), g_ref[...]).astype(BF16)
        for gi in range(A_GROUPS):
            cols = slice(gi * A_GDIM, (gi + 1) * A_GDIM)
            mixed = _dot(w_ref[gi], v[:, cols]) + b_ref[:, cols]
            o_ref[rows, cols] = (u[:, cols] * mixed).astype(o_ref.dtype)


def _sgu(p, g_sgu, w_sp, b_full, n_rows, tm):
    return pl.pallas_call(
        _sgu_kernel,
        grid=(n_rows // tm,),
        in_specs=[pl.BlockSpec((tm, A_WIDTH), lambda i: (i, OFF_AU // A_WIDTH)),
                  pl.BlockSpec((tm, A_WIDTH), lambda i: (i, OFF_AV // A_WIDTH)),
                  pl.BlockSpec((1, A_WIDTH), lambda i: (0, 0)),
                  pl.BlockSpec((A_GROUPS, A_CHUNK, A_CHUNK), lambda i: (0, 0, 0)),
                  pl.BlockSpec((A_CHUNK, A_WIDTH), lambda i: (0, 0))],
        out_specs=pl.BlockSpec((tm, A_WIDTH), lambda i: (i, 0)),
        out_shape=jax.ShapeDtypeStruct((n_rows, A_WIDTH), BF16),
        compiler_params=_cparams(("parallel",)),
        name="sgu",
    )(p, p, g_sgu, w_sp, b_full)


NA_GQ = 4


def _na_geometry(rows):
    kh = min(NA_KH, rows)
    gq = min(NA_GQ, rows)
    wr = min(gq + kh - 1, rows)
    starts = [int(np.clip(gq * g - kh // 2, 0, rows - wr)) for g in range(rows // gq)]
    return kh, gq, wr, starts


def _na_row_select(rows):
    kh, gq, wr, starts = _na_geometry(rows)
    n_rel = 2 * NA_KH - 1
    row_sel = np.zeros((len(starts), gq, wr, n_rel + 1), np.float32)
    for g, ws in enumerate(starts):
        for qi in range(gq):
            r = gq * g + qi
            rs = int(np.clip(r - kh // 2, 0, rows - kh))
            assert ws <= rs and rs + kh <= ws + wr
            for a in range(wr):
                inside = rs <= ws + a < rs + kh
                row_sel[g, qi, a, ws + a - r + NA_KH - 1 if inside else n_rel] = 1.0
    kinds, kind_of = [], []
    for g in range(len(starts)):
        same = [k for k, rep in enumerate(kinds) if np.array_equal(row_sel[rep], row_sel[g])]
        if not same:
            kinds.append(g)
        kind_of.append(same[0] if same else len(kinds) - 1)
    return row_sel[kinds], kind_of


def _na_bias_table(rpb, rows):
    kh, gq, wr, starts = _na_geometry(rows)
    n_rel = 2 * NA_KH - 1
    cq = np.arange(GRID_W)[:, None]
    kc = np.arange(GRID_W)[None, :]
    cs = np.clip(cq - NA_KW // 2, 0, GRID_W - NA_KW)
    valid = (kc >= cs) & (kc < cs + NA_KW)
    coff = np.clip(kc - cq, -(NA_KW - 1), NA_KW - 1) + NA_KW - 1
    col_sel = (coff[None, :, :] == np.arange(2 * NA_KW - 1)[:, None, None]).astype(np.float32)
    by_col = jnp.einsum('hrc,cqk->hrqk', rpb.astype(F32), col_sel, precision=lax.Precision.HIGHEST)
    by_col = jnp.where(valid[None, None], by_col, NEG_INF)
    by_col = jnp.concatenate([by_col, jnp.full_like(by_col[:, :1], NEG_INF)], axis=1)
    row_sel, _ = _na_row_select(rows)
    tab = jnp.einsum('gqar,hrck->hgqcak', row_sel, by_col, precision=lax.Precision.HIGHEST)
    return tab.reshape(rpb.shape[0], row_sel.shape[0], gq * GRID_W, wr * GRID_W)


def _softmax_pv(s_parts, v_parts):
    m = functools.reduce(jnp.maximum, [jnp.max(s, axis=-1, keepdims=True) for s in s_parts])
    p_parts = [jnp.exp(s - m) for s in s_parts]
    l = functools.reduce(jnp.add, [jnp.sum(p, axis=-1, keepdims=True) for p in p_parts])
    o = functools.reduce(jnp.add, [_dot(p.astype(BF16), v) for p, v in zip(p_parts, v_parts)])
    return o / l


def _natten_kernel(q_ref, k_ref, v_ref, qc_ref, kc_ref, vc_ref, tab_ref, o_ref, oc_ref, *, rows):
    scale = NA_HDIM ** -0.5
    _, gq, wr, starts = _na_geometry(rows)
    _, kind_of = _na_row_select(rows)
    kc = kc_ref[...]
    vc = vc_ref[...]
    for g, ws in enumerate(starts):
        qs = slice(g * gq * GRID_W, (g + 1) * gq * GRID_W)
        ks = slice(ws * GRID_W, (ws + wr) * GRID_W)
        q = q_ref[qs, :]
        s_w = _dot_nt(q, k_ref[ks, :]) * scale + tab_ref[0, kind_of[g]]
        s_c = _dot_nt(q, kc) * scale
        o_ref[qs, :] = _softmax_pv([s_w, s_c], [v_ref[ks, :], vc]).astype(o_ref.dtype)
    s = _dot_nt(qc_ref[...], kc) * scale
    oc_ref[...] = _softmax_pv([s], [vc]).astype(oc_ref.dtype)


def _natten(p, pc, tab, dims, ctx_len):
    nl, t, b = dims
    rows = t // GRID_W
    pc_arr, pc_row0, pc_col0 = pc
    cb = pc_row0 // ctx_len
    hd = NA_HDIM
    ctx_q = OFF_BQ if pc_col0 <= OFF_BQ else OFF_BK
    lat = lambda off: pl.BlockSpec((t, hd), lambda h, bi: (bi, off // hd + h))
    ctx = lambda off: pl.BlockSpec((ctx_len, hd), lambda h, bi: (cb + bi, (off - pc_col0) // hd + h))
    return pl.pallas_call(
        functools.partial(_natten_kernel, rows=rows),
        grid=(NA_HEADS, b),
        in_specs=[lat(OFF_BQ), lat(OFF_BK), lat(OFF_BV), ctx(ctx_q), ctx(OFF_BK), ctx(OFF_BV),
                  pl.BlockSpec((1,) + tab.shape[1:], lambda h, bi: (h, 0, 0, 0))],
        out_specs=[pl.BlockSpec((t, hd), lambda h, bi: (bi, h)),
                   pl.BlockSpec((ctx_len, hd), lambda h, bi: (bi, h))],
        out_shape=[jax.ShapeDtypeStruct((nl, NA_WIDTH), BF16),
                   jax.ShapeDtypeStruct((b * ctx_len, NA_WIDTH), BF16)],
        compiler_params=_cparams(("parallel", "arbitrary")),
        name="natten",
    )(p, p, p, pc_arr, pc_arr, pc_arr, tab)


def _scan_rows(x, reverse, op, fill):
    n = x.shape[0]
    idx = lax.broadcasted_iota(jnp.int32, x.shape, 0)
    step = 1
    while step < n:
        if reverse:
            x = op(x, jnp.where(idx < n - step, pltpu.roll(x, n - step, 0), fill))
        else:
            x = op(x, jnp.where(idx >= step, pltpu.roll(x, step, 0), fill))
        step *= 2
    return x


def _log_sigmoid(x):
    return jnp.minimum(x, 0.0) - jnp.log(1.0 + jnp.exp(-jnp.abs(x)))


def _rope_swap(x):
    lane = lax.broadcasted_iota(jnp.int32, x.shape, 1)
    return jnp.where(lane % 64 < 32, pltpu.roll(x, LANE - 32, 1), pltpu.roll(x, 32, 1))


def _chunk_rows(c):
    return pl.ds(pl.multiple_of(c * M_CHUNK, M_CHUNK), M_CHUNK)


def _gate_rows(c):
    return pl.ds(pl.multiple_of(c * 8, 8), 4)


def _mlstm_prepare(seq, bm_ref, rope):
    shape = (M_CHUNK, LANE)

    def body(c, carry):
        rows = _chunk_rows(c)
        gc = seq["g"][rows, :] + bm_ref[...]
        lf = _log_sigmoid(gc)
        xs = []
        for d in range(2):
            reverse = d == 1
            b = _scan_rows(lf, reverse, jnp.add, 0.0)
            a_rep = jnp.broadcast_to(b[:, 2 * d + 1:2 * d + 2], shape)
            x_rep = jnp.broadcast_to(gc[:, 2 * d:2 * d + 1], shape) - a_rep
            seq["a"][d, rows, :] = a_rep
            seq["x"][d, rows, :] = x_rep
            seq["mi"][d, rows, :] = a_rep + _scan_rows(x_rep, reverse, jnp.maximum, -jnp.inf)
            seq["row"][pl.ds(c * 8 + 2 * d + 1, 1), :] = jnp.max(x_rep, axis=0, keepdims=True)
            xs.append(x_rep)
        lane = lax.broadcasted_iota(jnp.int32, shape, 1)
        xt = jnp.where(lane < LANE // 2, xs[0], xs[1]).T
        seq["row"][pl.ds(c * 8, 1), :] = xt[0:1, :]
        seq["row"][pl.ds(c * 8 + 2, 1), :] = xt[LANE // 2:LANE // 2 + 1, :]
        q = seq["q"][rows, :].astype(F32)
        k = seq["k"][rows, :].astype(F32)
        if rope is not None:
            cos = rope[0][rows, :]
            sin = rope[1][rows, :]
            q = q * cos + _rope_swap(q) * sin
            k = k * cos + _rope_swap(k) * sin
        seq["qs"][rows, :] = (q * (M_DK ** -0.5)).astype(BF16)
        seq["ks"][rows, :] = k
        return carry

    lax.fori_loop(0, seq["n"], body, 0)


def _mlstm_step(seq, c, reverse, state, order):
    c_ref, m_ref = state
    d = 1 if reverse else 0
    ln = M_CHUNK
    rows = _chunk_rows(c)
    a = seq["a"][d, rows, :]
    x = seq["x"][d, rows, :]
    gate_rows = seq["row"][_gate_rows(c), :]
    x_row, x_max = gate_rows[2 * d:2 * d + 1, :], gate_rows[2 * d + 1:2 * d + 2, :]
    q = seq["qs"][rows, :]
    k = seq["ks"][rows, :]
    vx = jnp.concatenate([seq["v"][rows, :], jnp.ones((ln, LANE), BF16)], axis=1)
    m_prev = m_ref[d]
    c_prev = c_ref[d]

    inter = a + m_prev
    m_row = jnp.maximum(inter, seq["mi"][d, rows, :])
    s = _dot_nt(q, k.astype(BF16)) * jnp.exp(jnp.where(order, a + x_row, -jnp.inf) - m_row)
    ew = jnp.exp(inter - m_row)
    tot = _dot(s.astype(BF16), vx) + jnp.concatenate([ew] * 3, axis=1) * _dot(q, c_prev.astype(BF16))
    inv = 1.0 / jnp.maximum(jnp.abs(tot[:, M_DV:]), jnp.exp(-m_row))
    seq["hb" if reverse else "hf"][rows, :] = tot[:, :M_DV] * jnp.concatenate([inv] * 2, axis=1)

    b_last = a[0:1, :] if reverse else a[ln - 1:ln, :]
    m_new = jnp.maximum(b_last + m_prev, b_last + x_max)
    decay = jnp.exp(b_last + m_prev - m_new)
    kw = k * jnp.exp(b_last + x - m_new)
    c_ref[d] = jnp.concatenate([decay] * 3, axis=1) * c_prev + _dot(kw.T.astype(BF16), vx)
    m_ref[d] = m_new


def _mlstm_scan(seq, state):
    n = seq["n"]
    t_idx = lax.broadcasted_iota(jnp.int32, (M_CHUNK, M_CHUNK), 0)
    s_idx = lax.broadcasted_iota(jnp.int32, (M_CHUNK, M_CHUNK), 1)

    def body(ci, carry):
        _mlstm_step(seq, ci, False, state, t_idx >= s_idx)
        _mlstm_step(seq, n - 1 - ci, True, state, s_idx >= t_idx)
        return carry

    lax.fori_loop(0, n, body, 0, unroll=min(4, n))


def _mlstm_finish(seq, gn_ref):
    def body(c, carry):
        rows = _chunk_rows(c)
        y = _rms(seq["hf"][rows, :] + seq["hb"][rows, :], gn_ref[...])
        seq["y"][rows, :] = (y * _sigmoid(seq["o"][rows, :].astype(F32))).astype(seq["y"].dtype)
        return carry

    lax.fori_loop(0, seq["n"], body, 0)


_SEQ_IN = ("q", "k", "v", "o", "g")
_SEQ_SCRATCH = ("qs", "ks", "a", "x", "mi", "row", "hf", "hb")


def _mlstm_kernel(*refs):
    lat = dict(zip(_SEQ_IN, refs[0:5]))
    ctx = dict(zip(_SEQ_IN, refs[5:10]))
    bm_ref, gn_ref, cos_ref, sin_ref = refs[10:14]
    lat["y"], ctx["y"] = refs[14:16]
    lat.update(zip(_SEQ_SCRATCH, refs[16:24]))
    ctx.update(zip(_SEQ_SCRATCH, refs[24:32]))
    state = refs[32:34]
    for seq in (lat, ctx):
        seq["n"] = seq["q"].shape[0] // M_CHUNK

    _mlstm_prepare(ctx, bm_ref, None)
    _mlstm_prepare(lat, bm_ref, (cos_ref, sin_ref))
    for ref in state:
        ref[...] = jnp.zeros_like(ref)
    _mlstm_scan(ctx, state)
    _mlstm_scan(lat, state)
    _mlstm_finish(ctx, gn_ref)
    _mlstm_finish(lat, gn_ref)


def _rope_tables(t):
    pos = np.arange(t)
    half = M_DK // 4
    inv = ROPE_THETA ** (-np.arange(half, dtype=np.float64) / half)
    ang_r = (pos // GRID_W)[:, None] * inv[None, :]
    ang_c = (pos % GRID_W)[:, None] * inv[None, :]
    ang = np.concatenate([ang_r, ang_r, ang_c, ang_c], axis=1)
    sign = np.tile(np.concatenate([-np.ones(half), np.ones(half)]), 2)[None, :]
    return jnp.asarray(np.cos(ang), F32), jnp.asarray(np.sin(ang) * sign, F32)


def _seq_scratch(n):
    col = pltpu.VMEM((2, n, LANE), F32)
    return [pltpu.VMEM((n, M_DK), BF16), pltpu.VMEM((n, M_DK), F32), col, col, col,
            pltpu.VMEM((n // M_CHUNK * 8, LANE), F32), pltpu.VMEM((n, M_DV), F32), pltpu.VMEM((n, M_DV), F32)]


def _mlstm(p, pc, g, bm, gn, dims, ctx_len):
    nl, t, b = dims
    pc_arr, pc_row0, pc_col0 = pc
    cb = pc_row0 // ctx_len
    gb = nl // ctx_len
    ctx_o = OFF_CO if pc_arr.shape[1] + pc_col0 >= OFF_CO + M_V_W else OFF_CV
    cos, sin = _rope_tables(t)
    dk, dv = M_DK, M_DV
    lat = lambda off, w: pl.BlockSpec((t, w), lambda bi, h: (bi, off // w + h))
    ctx = lambda off, w: pl.BlockSpec((ctx_len, w), lambda bi, h: (cb + bi, (off - pc_col0) // w + h))
    const = lambda shape: pl.BlockSpec(shape, lambda bi, h: (0, 0))
    return pl.pallas_call(
        _mlstm_kernel,
        grid=(b, M_HEADS),
        in_specs=[lat(OFF_CQ, dk), lat(OFF_CK, dk), lat(OFF_CV, dv), lat(OFF_CO, dv),
                  pl.BlockSpec((t, LANE), lambda bi, h: (bi, h)),
                  ctx(OFF_CQ, dk), ctx(OFF_CK, dk), ctx(OFF_CV, dv), ctx(ctx_o, dv),
                  pl.BlockSpec((ctx_len, LANE), lambda bi, h: (gb + bi, h)),
                  pl.BlockSpec((1, LANE), lambda bi, h: (0, h)),
                  pl.BlockSpec((1, dv), lambda bi, h: (0, h)),
                  const((t, LANE)), const((t, LANE))],
        out_specs=[pl.BlockSpec((t, dv), lambda bi, h: (bi, h)),
                   pl.BlockSpec((ctx_len, dv), lambda bi, h: (bi, h))],
        out_shape=[jax.ShapeDtypeStruct((nl, M_V_W), BF16),
                   jax.ShapeDtypeStruct((b * ctx_len, M_V_W), BF16)],
        scratch_shapes=_seq_scratch(t) + _seq_scratch(ctx_len) + [
            pltpu.VMEM((2, dk, dv + LANE), F32), pltpu.VMEM((2, 1, LANE), F32)],
        compiler_params=_cparams(("parallel", "arbitrary")),
        name="mlstm",
    )(p, p, p, p, g, pc_arr, pc_arr, pc_arr, pc_arr, g, bm, gn, cos, sin)


def _merge_kernel(ya_ref, yb_ref, yc_ref, g0_ref, g1_ref, g2_ref, w_ref, o_ref):
    acc = _sigmoid(g0_ref[...].astype(F32)) * _dot(ya_ref[...], w_ref[0])
    acc = acc + _sigmoid(g1_ref[...].astype(F32)) * _dot(yb_ref[...], w_ref[1])
    acc = acc + _sigmoid(g2_ref[...].astype(F32)) * _dot(yc_ref[...], w_ref[2])
    o_ref[...] = acc.astype(o_ref.dtype)


def _merge(ya, yb, yc, p, w_branch, n_rows, tm, tn):
    d = w_branch.shape[2]
    br = lambda: pl.BlockSpec((tm, BRANCH_W), lambda i, j: (i, 0))
    gate = lambda k: pl.BlockSpec((tm, tn), lambda i, j: (i, (OFF_GATE + k * d) // tn + j))
    return pl.pallas_call(
        _merge_kernel,
        grid=(n_rows // tm, d // tn),
        in_specs=[br(), br(), br(), gate(0), gate(1), gate(2),
                  pl.BlockSpec((N_BRANCH, BRANCH_W, tn), lambda i, j: (0, 0, j),
                               pipeline_mode=pl.Buffered(1) if tn == d else None)],
        out_specs=pl.BlockSpec((tm, tn), lambda i, j: (i, j)),
        out_shape=jax.ShapeDtypeStruct((n_rows, d), BF16),
        compiler_params=_cparams(("parallel", "arbitrary")),
        name="merge",
    )(ya, yb, yc, p, p, p, w_branch)


def _route(scores, sel):
    row = lambda a, e: a[e:e + 1, :]
    best = None
    for gi in range(N_GROUPS):
        a, b, c, d = (row(sel, gi * EXP_PER_GROUP + j) for j in range(EXP_PER_GROUP))
        hi1, lo1 = jnp.maximum(a, b), jnp.minimum(a, b)
        hi2, lo2 = jnp.maximum(c, d), jnp.minimum(c, d)
        gs = jnp.maximum(hi1, hi2) + jnp.maximum(jnp.minimum(hi1, hi2), jnp.maximum(lo1, lo2))
        if best is None:
            best, g_idx = gs, jnp.zeros_like(gs, dtype=jnp.int32)
        else:
            take = gs > best
            best = jnp.where(take, gs, best)
            g_idx = jnp.where(take, gi, g_idx)

    def in_group(a, j):
        out = row(a, j)
        for gi in range(1, N_GROUPS):
            out = jnp.where(g_idx == gi, row(a, gi * EXP_PER_GROUP + j), out)
        return out

    v = [in_group(sel, j) for j in range(EXP_PER_GROUP)]
    u = [in_group(scores, j) for j in range(EXP_PER_GROUP)]

    def argmax_first(vals):
        bv, bi, bu = vals[0], jnp.zeros_like(g_idx), u[0]
        for j in range(1, EXP_PER_GROUP):
            take = vals[j] > bv
            bv = jnp.where(take, vals[j], bv)
            bi = jnp.where(take, j, bi)
            bu = jnp.where(take, u[j], bu)
        return bi, bu

    i1, s1 = argmax_first(v)
    i2, s2 = argmax_first([jnp.where(i1 == j, -jnp.inf, v[j]) for j in range(EXP_PER_GROUP)])
    tot = s1 + s2
    return g_idx * EXP_PER_GROUP + i1, g_idx * EXP_PER_GROUP + i2, s1 / tot, s2 / tot


def _to_tall(ref, val):
    rows = val.shape[0]
    for j in range(ROW_SEG):
        ref[pl.ds(j, rows, stride=ROW_SEG), :] = val[:, j * LANE:(j + 1) * LANE]


def _from_tall(ref):
    rows = ref.shape[0] // ROW_SEG
    return jnp.concatenate([ref[pl.ds(j, rows, stride=ROW_SEG), :] for j in range(ROW_SEG)], axis=1)


def _outproj_kernel(mg_ref, x_ref, gt_ref, sh_ref, sc_ref, g_ref, w_ref, wr_ref, br_ref, xo_ref, h_ref, r_ref):
    y = _dot(mg_ref[...], w_ref[...])
    xn = x_ref[...] + gt_ref[0] * y
    xo_ref[...] = xn
    h = _rms(xn, g_ref[...]) * (1.0 + sc_ref[0]) + sh_ref[0]
    _to_tall(h_ref, h)
    scores = _sigmoid(_dot_nt(wr_ref[...], h.astype(BF16)))
    e1, e2, w1, w2 = _route(scores, scores + br_ref[...])
    r_ref[...] = jnp.zeros_like(r_ref)
    r_ref[0:1, :] = e1.astype(F32)
    r_ref[1:2, :] = e2.astype(F32)
    r_ref[2:3, :] = w1
    r_ref[3:4, :] = w2


def _outproj(merged, x, mod3, g2, w_out, w_rt, b_r, dims, n_rows, tm):
    nl, t, b = dims
    d = x.shape[1]
    grp = functools.partial(_group_of, tm=tm, nl=nl, t=t, b=b)
    mrow = lambda k: pl.BlockSpec((1, 1, d), lambda i: (grp(i) * 6 + k, 0, 0))
    return pl.pallas_call(
        _outproj_kernel,
        grid=(n_rows // tm,),
        in_specs=[pl.BlockSpec((tm, d), lambda i: (i, 0)),
                  pl.BlockSpec((tm, d), lambda i: (i, 0)),
                  mrow(2), mrow(3), mrow(4),
                  pl.BlockSpec((1, d), lambda i: (0, 0)),
                  pl.BlockSpec((d, d), lambda i: (0, 0), pipeline_mode=pl.Buffered(1)),
                  pl.BlockSpec((N_EXPERTS, d), lambda i: (0, 0)),
                  pl.BlockSpec((N_EXPERTS, 1), lambda i: (0, 0))],
        out_specs=[pl.BlockSpec((tm, d), lambda i: (i, 0)),
                   pl.BlockSpec((tm * ROW_SEG, LANE), lambda i: (i, 0)),
                   pl.BlockSpec((8, tm), lambda i: (0, i))],
        out_shape=[jax.ShapeDtypeStruct((n_rows, d), F32),
                   jax.ShapeDtypeStruct((n_rows * ROW_SEG, LANE), F32),
                   jax.ShapeDtypeStruct((8, n_rows), F32)],
        compiler_params=_cparams(("parallel",)),
        name="outproj_route",
    )(merged, x, mod3, mod3, mod3, g2, w_out, w_rt, b_r)


def _slot_plan(route, tm, max_tiles):
    n = route.shape[1]
    e = route[0:2, :].astype(jnp.int32).reshape(-1)
    oh = (e[:, None] == jnp.arange(N_EXPERTS, dtype=jnp.int32)[None, :]).astype(jnp.int32)
    cs = jnp.cumsum(oh, axis=0)
    rank = jnp.sum(cs * oh, axis=1) - 1
    counts = cs[-1]
    padded = ((counts + tm - 1) // tm) * tm
    ends = jnp.cumsum(padded)
    slot = jnp.sum(oh * (ends - padded)[None, :], axis=1) + rank
    tile_ends = ends // tm
    n_used = tile_ends[-1]
    tiles = jnp.arange(max_tiles, dtype=jnp.int32)
    src = jnp.minimum(tiles, n_used - 1)
    tile_e = jnp.minimum(jnp.sum((src[:, None] >= tile_ends[None, :]).astype(jnp.int32), axis=1), N_EXPERTS - 1)
    left = jnp.take(counts, tile_e) - (tiles - jnp.take(tile_ends - padded // tm, tile_e)) * tm
    n_valid = jnp.where(tiles < n_used, jnp.clip(left, 0, tm), 0)
    tok_of_slot, dst_of_slot = _invert_slots(slot.astype(jnp.int32), n, max_tiles * tm)
    wts = route[2:4, :].T
    i32 = lambda a: a.astype(jnp.int32)
    return tok_of_slot, dst_of_slot, i32(tile_e), i32(n_valid), i32(n_used.reshape(1)), wts


def _invert_kernel(groups_ref, slot_ref, tok_ref, dst_ref, *, n):
    def clear(g, carry):
        for u in range(DMA_UNROLL):
            tok_ref[g * DMA_UNROLL + u] = 0
            dst_ref[g * DMA_UNROLL + u] = 0
        return carry

    def put(g, carry):
        for u in range(DMA_UNROLL):
            f = g * DMA_UNROLL + u
            s = slot_ref[f]
            tok_ref[s] = jnp.where(f >= n, f - n, f)
            dst_ref[s] = f
        return carry

    lax.fori_loop(0, groups_ref[0], clear, 0)
    lax.fori_loop(0, groups_ref[1], put, 0)


def _invert_slots(slot, n, n_slots):
    assert n_slots % DMA_UNROLL == 0 and (2 * n) % DMA_UNROLL == 0
    smem = pl.BlockSpec(memory_space=pltpu.SMEM)
    out = jax.ShapeDtypeStruct((n_slots,), jnp.int32)
    groups = jnp.array([n_slots // DMA_UNROLL, 2 * n // DMA_UNROLL], jnp.int32)
    return pl.pallas_call(
        functools.partial(_invert_kernel, n=n),
        in_specs=[smem, smem], out_specs=[smem, smem], out_shape=[out, out],
        name="moe_invert_slots",
    )(groups, slot)


def _row_copy(src_ref, src_row, dst_ref, dst_row, sem):
    src = src_ref.at[pl.ds(pl.multiple_of(src_row * ROW_SEG, ROW_SEG), ROW_SEG), :]
    dst = dst_ref.at[pl.ds(pl.multiple_of(dst_row * ROW_SEG, ROW_SEG), ROW_SEG), :]
    return pltpu.make_async_copy(src, dst, sem)


def _expert_kernel(tok_ref, dst_ref, te_ref, nv_ref, nu_ref, h_ref, wg_ref, wu_ref, wd_ref, y_ref,
                   xbuf, ybuf, gsem, ssem, *, tm):
    del te_ref
    i = pl.program_id(0)
    n_used = nu_ref[0]
    used = i < n_used
    cur = i % 2

    def for_rows(n_rows, body):
        def group(g, carry):
            for u in range(DMA_UNROLL):
                body(g * DMA_UNROLL + u)
            return carry

        def single(j, carry):
            body(j)
            return carry

        n_groups = n_rows // DMA_UNROLL
        lax.fori_loop(0, n_groups, group, 0)
        if not isinstance(n_rows, int) or n_rows % DMA_UNROLL:
            lax.fori_loop(n_groups * DMA_UNROLL, n_rows, single, 0)

    def wait_rows(buf_ref, n_rows, sem):
        part = buf_ref.at[pl.ds(0, n_rows * ROW_SEG), :]
        pltpu.make_async_copy(part, part, sem).wait()

    def gather(tile, buf, wait):
        if wait:
            wait_rows(xbuf.at[buf], tm, gsem.at[buf])
        else:
            for_rows(tm, lambda j: _row_copy(h_ref, tok_ref[tile * tm + j], xbuf.at[buf], j,
                                             gsem.at[buf]).start(priority=1))

    def scatter(tile, buf, wait):
        if wait:
            wait_rows(ybuf.at[buf], nv_ref[tile], ssem.at[buf])
        else:
            for_rows(nv_ref[tile], lambda j: _row_copy(ybuf.at[buf], j, y_ref, dst_ref[tile * tm + j],
                                                       ssem.at[buf]).start())

    @pl.when(i == 0)
    def _():
        gather(0, 0, False)

    @pl.when(i + 1 < n_used)
    def _():
        gather(i + 1, 1 - cur, False)

    @pl.when(used)
    def _():
        gather(i, cur, True)
        x = _from_tall(xbuf.at[cur]).astype(BF16)
        a = _dot(x, wg_ref[0, 0])
        he = a * _sigmoid(a) * _dot(x, wu_ref[0, 0])
        _to_tall(ybuf.at[cur], _dot(he.astype(BF16), wd_ref[0, 0]))

    @pl.when(jnp.logical_and(i >= 1, i - 1 < n_used))
    def _():
        scatter(i - 1, 1 - cur, True)

    @pl.when(used)
    def _():
        scatter(i, cur, False)

    @pl.when(jnp.logical_and(i == pl.num_programs(0) - 1, used))
    def _():
        scatter(i, cur, True)


def _experts(h, tok_of_slot, dst_of_slot, tile_e, n_valid, n_used, experts, layer, tm):
    wg, wu, wd = experts
    d, f = wg.shape[2:]
    wspec = lambda shape: pl.BlockSpec((1,) + shape, lambda i, t, s, e, v, u: (layer, e[i], 0, 0))
    buf = pltpu.VMEM((2, tm * ROW_SEG, LANE), F32)
    return pl.pallas_call(
        functools.partial(_expert_kernel, tm=tm),
        grid_spec=pltpu.PrefetchScalarGridSpec(
            num_scalar_prefetch=5,
            grid=(tile_e.shape[0],),
            in_specs=[pl.BlockSpec(memory_space=pl.ANY), wspec((1, d, f)), wspec((1, d, f)), wspec((1, f, d))],
            out_specs=pl.BlockSpec(memory_space=pl.ANY),
            scratch_shapes=[buf, buf, pltpu.SemaphoreType.DMA((2,)), pltpu.SemaphoreType.DMA((2,))]),
        out_shape=jax.ShapeDtypeStruct((2 * h.shape[0], LANE), F32),
        compiler_params=_cparams(("arbitrary",)),
        name="moe_experts",
    )(tok_of_slot, dst_of_slot, tile_e, n_valid, n_used, h, wg, wu, wd)


def _combine_kernel(x_ref, w_ref, gt_ref, gf_ref, y1_ref, y2_ref, o_ref, *, final):
    w = w_ref[...]
    f = w[:, 0:1] * _from_tall(y1_ref) + w[:, 1:2] * _from_tall(y2_ref)
    xo = x_ref[...] + gt_ref[0] * f
    if final:
        xo = _rms(xo, gf_ref[...])
    o_ref[...] = xo


def _combine(x, wts, mod3, g_final, y, dims, n_tok, tm, final):
    nl, t, b = dims
    d = x.shape[1]
    grp = functools.partial(_group_of, tm=tm, nl=nl, t=t, b=b)
    return pl.pallas_call(
        functools.partial(_combine_kernel, final=final),
        grid=(n_tok // tm,),
        in_specs=[pl.BlockSpec((tm, d), lambda i: (i, 0)),
                  pl.BlockSpec((tm, 2), lambda i: (i, 0)),
                  pl.BlockSpec((1, 1, d), lambda i: (grp(i) * 6 + 5, 0, 0)),
                  pl.BlockSpec((1, d), lambda i: (0, 0)),
                  pl.BlockSpec((tm * ROW_SEG, LANE), lambda i: (i, 0)),
                  pl.BlockSpec((tm * ROW_SEG, LANE), lambda i: (n_tok // tm + i, 0))],
        out_specs=pl.BlockSpec((tm, d), lambda i: (i, 0)),
        out_shape=jax.ShapeDtypeStruct((n_tok, d), F32),
        compiler_params=_cparams(("parallel",)),
        name="moe_combine",
    )(x, wts, mod3, g_final, y, y)


def _moe(h2, route, xn, mod3, g_final, experts, layer, dims, final):
    n_tok = xn.shape[0]
    max_tiles = (2 * n_tok) // MOE_TM + N_EXPERTS
    tok_of_slot, dst_of_slot, tile_e, n_valid, n_used, wts = _slot_plan(route, MOE_TM, max_tiles)
    y = _experts(h2, tok_of_slot, dst_of_slot, tile_e, n_valid, n_used, experts, layer, MOE_TM)
    nl, t, _ = dims
    tm = 2 * ROW_TM if t % (2 * ROW_TM) == 0 and (n_tok - nl) % (2 * ROW_TM) == 0 else ROW_TM
    return _combine(xn, wts, mod3, g_final, y, dims, n_tok, tm, final)


def _gate_weight(w_cg):
    d = w_cg.shape[0]
    per_head = w_cg.reshape(d, 4, M_HEADS).transpose(0, 2, 1)
    return jnp.pad(per_head, ((0, 0), (0, 0), (0, LANE - 4))).reshape(d, M_HEADS * LANE)


def _layer(x, mod3, dims, ctx_len, last, g1, g2, w_in, g_sgu, w_sp, b_sp, rpb, b_mgate, g_mnorm, w_branch, w_out,
           w_rt, b_r, experts, layer, g_final):
    nl, t, b = dims
    nt, d = x.shape
    fits = lambda rows: t % rows == 0 and (nt - nl) % rows == 0
    tm = 512 if fits(512) else 256
    tm_proj = 1024 if fits(1024) else tm

    w_main = jnp.concatenate([w_in[:, :OFF_CG], w_in[:, OFF_CG + M_NGATE:]], axis=1).astype(BF16)
    w_gate = _gate_weight(w_in[:, OFF_CG:OFF_CG + M_NGATE]).astype(BF16)
    g = _inproj(x, mod3, g1, w_gate, dims, tm_proj, M_HEADS * LANE, F32)
    if last:
        p = _inproj(x, mod3, g1, w_main, dims, tm_proj, 1024, BF16, 0, nl)
        pc = (_inproj(x, mod3, g1, w_main[:, OFF_BK:OFF_CO], dims, tm_proj, 1024, BF16, nl, nt - nl), 0, OFF_BK)
    else:
        p = _inproj(x, mod3, g1, w_main, dims, tm_proj, 1024, BF16)
        pc = (p, nl, 0)

    n_rows = nl if last else nt
    b_full = jnp.repeat(b_sp.T, A_GDIM, axis=1)
    ya = _sgu(p, g_sgu[None, :], w_sp.astype(BF16), b_full, n_rows, tm)

    yb_l, yb_c = _natten(p, pc, _na_bias_table(rpb, t // GRID_W), dims, ctx_len)

    bm = jnp.pad(b_mgate.T, ((0, 0), (0, LANE - 4))).reshape(1, M_HEADS * LANE)
    yc_l, yc_c = _mlstm(p, pc, g, bm, g_mnorm[None, :], dims, ctx_len)

    if last:
        yb, yc = yb_l, yc_l
    else:
        yb = jnp.concatenate([yb_l, yb_c], axis=0)
        yc = jnp.concatenate([yc_l, yc_c], axis=0)

    merged = _merge(ya, yb, yc, p, w_branch.astype(BF16), n_rows, tm, d)
    xn, h2, route = _outproj(merged, x, mod3, g2[None, :], w_out.astype(BF16), w_rt, b_r, dims, n_rows, tm)
    return _moe(h2, route, xn, mod3, g_final, experts, layer, dims, last)


def kernel(x, c, ctx, c_ctx, w_ada, b_ada, g_norm1, g_norm2, w_in, g_sgu, w_spatial, b_spatial, na_rpb, b_mgate,
           g_mnorm, w_branch, w_out, w_router, b_router, w_e_gate, w_e_up, w_e_down, g_final):
    b, t, d = x.shape
    ctx_len = ctx.shape[1]
    nl = b * t
    dims = (nl, t, b)
    depth = w_ada.shape[0]
    xs = jnp.concatenate([x.reshape(nl, d), ctx.reshape(b * ctx_len, d)], axis=0)
    n_mod = -(-(b + 1) // 8) * 8
    cc = jnp.zeros((n_mod, d), F32).at[:b].set(c).at[b].set(c_ctx)
    w_rt = w_router.T.astype(BF16)
    b_r = b_router[:, None].astype(F32)
    experts = tuple(w.astype(BF16) for w in (w_e_gate, w_e_up, w_e_down))
    for layer in range(depth):
        mod = _ada(cc, w_ada, layer, b_ada[layer][None, :])
        mod3 = mod.reshape(n_mod * 6, 1, d)
        xs = _layer(xs, mod3, dims, ctx_len, layer == depth - 1, g_norm1[layer][None, :], g_norm2[layer],
                    w_in[layer], g_sgu[layer], w_spatial[layer], b_spatial[layer], na_rpb[layer], b_mgate[layer],
                    g_mnorm[layer], w_branch[layer], w_out[layer], w_rt, b_r, experts, layer, g_final[None, :])
    return xs.reshape(b, t, d)
```

```python
import functools

import numpy as np
import jax
import jax.numpy as jnp
from jax import lax
from jax.experimental import pallas as pl
from jax.experimental.pallas import tpu as pltpu

F32 = jnp.float32
BF16 = jnp.bfloat16

GRID_W = 64
EPS = 1e-6
NEG_INF = -1e30
ROPE_THETA = 10000.0

A_WIDTH = 1024
A_CHUNK = 128
A_GROUPS = 8
A_GDIM = A_WIDTH // A_GROUPS

NA_HEADS = 8
NA_HDIM = 128
NA_WIDTH = NA_HEADS * NA_HDIM
NA_KH = 8
NA_KW = 16

M_HEADS = 4
M_DK = 128
M_DV = 256
M_QK_W = M_HEADS * M_DK
M_V_W = M_HEADS * M_DV
M_CHUNK = 128
M_NGATE = 4 * M_HEADS

N_BRANCH = 3
BRANCH_W = 1024

N_EXPERTS = 16
N_GROUPS = 4
EXP_PER_GROUP = N_EXPERTS // N_GROUPS
D_FF_EXPERT = 1024

OFF_AU = 0
OFF_AV = OFF_AU + A_WIDTH
OFF_BQ = OFF_AV + A_WIDTH
OFF_BK = OFF_BQ + NA_WIDTH
OFF_BV = OFF_BK + NA_WIDTH
OFF_CQ = OFF_BV + NA_WIDTH
OFF_CK = OFF_CQ + M_QK_W
OFF_CV = OFF_CK + M_QK_W
OFF_CO = OFF_CV + M_V_W
OFF_CG = OFF_CO + M_V_W
OFF_GATE = OFF_CO + M_V_W

LANE = 128
VMEM_LIMIT = 56 * 1024 * 1024

ROW_SEG = 16
MOE_TM = 256
ROW_TM = 256
DMA_UNROLL = 8


def _cparams(sem):
    return pltpu.CompilerParams(dimension_semantics=sem, vmem_limit_bytes=VMEM_LIMIT)


def _dot(a, b):
    return jnp.dot(a, b, preferred_element_type=F32)


def _dot_nt(a, b):
    return lax.dot_general(a, b, (((1,), (1,)), ((), ())), preferred_element_type=F32)


def _sigmoid(x):
    return 1.0 / (1.0 + jnp.exp(-x))


def _gelu_tanh(x):
    return 0.5 * x * (1.0 + jnp.tanh(np.sqrt(2.0 / np.pi).astype(np.float32) * (x + 0.044715 * (x * x * x))))


def _rms(x, g):
    return x * lax.rsqrt(jnp.mean(x * x, axis=-1, keepdims=True) + EPS) * g


def _ada_kernel(c_ref, w_ref, b_ref, o_ref):
    cc = c_ref[...]
    s = cc * _sigmoid(cc)
    o_ref[...] = _dot(s.astype(BF16), w_ref[0].astype(BF16)) + b_ref[...]


def _ada(cc, w, layer, b):
    rows, d = cc.shape
    n = w.shape[2]
    tn = 1024
    return pl.pallas_call(
        _ada_kernel,
        grid=(n // tn,),
        in_specs=[pl.BlockSpec((rows, d), lambda j: (0, 0)),
                  pl.BlockSpec((1, d, tn), lambda j: (layer, 0, j)),
                  pl.BlockSpec((1, tn), lambda j: (0, j))],
        out_specs=pl.BlockSpec((rows, tn), lambda j: (0, j)),
        out_shape=jax.ShapeDtypeStruct((rows, n), F32),
        compiler_params=_cparams(("arbitrary",)),
        name="ada_mod",
    )(cc, w, b)


def _inproj_kernel(x_ref, sh_ref, sc_ref, g_ref, w_ref, o_ref, h_ref):
    @pl.when(pl.program_id(1) == 0)
    def _():
        h = _rms(x_ref[...], g_ref[...]) * (1.0 + sc_ref[0]) + sh_ref[0]
        h_ref[...] = h.astype(BF16)

    o_ref[...] = _dot(h_ref[...], w_ref[...]).astype(o_ref.dtype)


def _group_of(i, tm, nl, t, b):
    return jnp.where(i * tm < nl, (i * tm) // t, b)


def _inproj(x, mod3, g, w, dims, tm, tn, out_dtype, row0=0, n_rows=None):
    nl, t, b = dims
    d = x.shape[1]
    n = w.shape[1]
    n_rows = x.shape[0] - row0 if n_rows is None else n_rows
    first = row0 // tm
    grp = lambda i: _group_of(i + first, tm=tm, nl=nl, t=t, b=b)
    return pl.pallas_call(
        _inproj_kernel,
        grid=(n_rows // tm, n // tn),
        in_specs=[pl.BlockSpec((tm, d), lambda i, j: (i + first, 0)),
                  pl.BlockSpec((1, 1, d), lambda i, j: (grp(i) * 6 + 0, 0, 0)),
                  pl.BlockSpec((1, 1, d), lambda i, j: (grp(i) * 6 + 1, 0, 0)),
                  pl.BlockSpec((1, d), lambda i, j: (0, 0)),
                  pl.BlockSpec((d, tn), lambda i, j: (0, j))],
        out_specs=pl.BlockSpec((tm, tn), lambda i, j: (i, j)),
        out_shape=jax.ShapeDtypeStruct((n_rows, n), out_dtype),
        scratch_shapes=[pltpu.VMEM((tm, d), BF16)],
        compiler_params=_cparams(("parallel", "arbitrary")),
        name="inproj",
    )(x, mod3, mod3, g, w)


def _sgu_kernel(u_ref, v_ref, g_ref, w_ref, b_ref, o_ref):
    n_chunk = u_ref.shape[0] // A_CHUNK
    for c in range(n_chunk):
        rows = slice(c * A_CHUNK, (c + 1) * A_CHUNK)
        u = _gelu_tanh(u_ref[rows, :].astype(F32))
        v = _rms(_gelu_tanh(v_ref[rows, :].astype(F32)), g_ref[...]).astype(BF16)
        for gi in range(A_GROUPS):
            cols = slice(gi * A_GDIM, (gi + 1) * A_GDIM)
            mixed = _dot(w_ref[gi], v[:, cols]) + b_ref[:, cols]
            o_ref[rows, cols] = (u[:, cols] * mixed).astype(o_ref.dtype)


def _sgu(p, g_sgu, w_sp, b_full, n_rows, tm):
    return pl.pallas_call(
        _sgu_kernel,
        grid=(n_rows // tm,),
        in_specs=[pl.BlockSpec((tm, A_WIDTH), lambda i: (i, OFF_AU // A_WIDTH)),
                  pl.BlockSpec((tm, A_WIDTH), lambda i: (i, OFF_AV // A_WIDTH)),
                  pl.BlockSpec((1, A_WIDTH), lambda i: (0, 0)),
                  pl.BlockSpec((A_GROUPS, A_CHUNK, A_CHUNK), lambda i: (0, 0, 0)),
                  pl.BlockSpec((A_CHUNK, A_WIDTH), lambda i: (0, 0))],
        out_specs=pl.BlockSpec((tm, A_WIDTH), lambda i: (i, 0)),
        out_shape=jax.ShapeDtypeStruct((n_rows, A_WIDTH), BF16),
        compiler_params=_cparams(("parallel",)),
        name="sgu",
    )(p, p, g_sgu, w_sp, b_full)


NA_GQ = 4


def _na_geometry(rows):
    kh = min(NA_KH, rows)
    gq = min(NA_GQ, rows)
    wr = min(gq + kh - 1, rows)
    starts = [int(np.clip(gq * g - kh // 2, 0, rows - wr)) for g in range(rows // gq)]
    return kh, gq, wr, starts


def _na_row_select(rows):
    kh, gq, wr, starts = _na_geometry(rows)
    n_rel = 2 * NA_KH - 1
    row_sel = np.zeros((len(starts), gq, wr, n_rel + 1), np.float32)
    for g, ws in enumerate(starts):
        for qi in range(gq):
            r = gq * g + qi
            rs = int(np.clip(r - kh // 2, 0, rows - kh))
            assert ws <= rs and rs + kh <= ws + wr
            for a in range(wr):
                inside = rs <= ws + a < rs + kh
                row_sel[g, qi, a, ws + a - r + NA_KH - 1 if inside else n_rel] = 1.0
    kinds, kind_of = [], []
    for g in range(len(starts)):
        same = [k for k, rep in enumerate(kinds) if np.array_equal(row_sel[rep], row_sel[g])]
        if not same:
            kinds.append(g)
        kind_of.append(same[0] if same else len(kinds) - 1)
    return row_sel[kinds], kind_of


def _na_bias_table(rpb, rows):
    kh, gq, wr, starts = _na_geometry(rows)
    n_rel = 2 * NA_KH - 1
    cq = np.arange(GRID_W)[:, None]
    kc = np.arange(GRID_W)[None, :]
    cs = np.clip(cq - NA_KW // 2, 0, GRID_W - NA_KW)
    valid = (kc >= cs) & (kc < cs + NA_KW)
    coff = np.clip(kc - cq, -(NA_KW - 1), NA_KW - 1) + NA_KW - 1
    col_sel = (coff[None, :, :] == np.arange(2 * NA_KW - 1)[:, None, None]).astype(np.float32)
    by_col = jnp.einsum('hrc,cqk->hrqk', rpb.astype(F32), col_sel, precision=lax.Precision.HIGHEST)
    by_col = jnp.where(valid[None, None], by_col, NEG_INF)
    by_col = jnp.concatenate([by_col, jnp.full_like(by_col[:, :1], NEG_INF)], axis=1)
    row_sel, _ = _na_row_select(rows)
    tab = jnp.einsum('gqar,hrck->hgqcak', row_sel, by_col, precision=lax.Precision.HIGHEST)
    return tab.reshape(rpb.shape[0], row_sel.shape[0], gq * GRID_W, wr * GRID_W)


def _softmax_pv(s_parts, v_parts):
    m = functools.reduce(jnp.maximum, [jnp.max(s, axis=-1, keepdims=True) for s in s_parts])
    p_parts = [jnp.exp(s - m) for s in s_parts]
    l = functools.reduce(jnp.add, [jnp.sum(p, axis=-1, keepdims=True) for p in p_parts])
    o = functools.reduce(jnp.add, [_dot(p.astype(BF16), v) for p, v in zip(p_parts, v_parts)])
    return o / l


def _natten_kernel(q_ref, k_ref, v_ref, qc_ref, kc_ref, vc_ref, tab_ref, o_ref, oc_ref, *, rows):
    scale = NA_HDIM ** -0.5
    _, gq, wr, starts = _na_geometry(rows)
    _, kind_of = _na_row_select(rows)
    kc = kc_ref[...]
    vc = vc_ref[...]
    for g, ws in enumerate(starts):
        qs = slice(g * gq * GRID_W, (g + 1) * gq * GRID_W)
        ks = slice(ws * GRID_W, (ws + wr) * GRID_W)
        q = q_ref[qs, :]
        s_w = _dot_nt(q, k_ref[ks, :]) * scale + tab_ref[0, kind_of[g]]
        s_c = _dot_nt(q, kc) * scale
        o_ref[qs, :] = _softmax_pv([s_w, s_c], [v_ref[ks, :], vc]).astype(o_ref.dtype)
    s = _dot_nt(qc_ref[...], kc) * scale
    oc_ref[...] = _softmax_pv([s], [vc]).astype(oc_ref.dtype)


def _natten(p, pc, tab, dims, ctx_len):
    nl, t, b = dims
    rows = t // GRID_W
    pc_arr, pc_row0, pc_col0 = pc
    cb = pc_row0 // ctx_len
    hd = NA_HDIM
    ctx_q = OFF_BQ if pc_col0 <= OFF_BQ else OFF_BK
    lat = lambda off: pl.BlockSpec((t, hd), lambda h, bi: (bi, off // hd + h))
    ctx = lambda off: pl.BlockSpec((ctx_len, hd), lambda h, bi: (cb + bi, (off - pc_col0) // hd + h))
    return pl.pallas_call(
        functools.partial(_natten_kernel, rows=rows),
        grid=(NA_HEADS, b),
        in_specs=[lat(OFF_BQ), lat(OFF_BK), lat(OFF_BV), ctx(ctx_q), ctx(OFF_BK), ctx(OFF_BV),
                  pl.BlockSpec((1,) + tab.shape[1:], lambda h, bi: (h, 0, 0, 0))],
        out_specs=[pl.BlockSpec((t, hd), lambda h, bi: (bi, h)),
                   pl.BlockSpec((ctx_len, hd), lambda h, bi: (bi, h))],
        out_shape=[jax.ShapeDtypeStruct((nl, NA_WIDTH), BF16),
                   jax.ShapeDtypeStruct((b * ctx_len, NA_WIDTH), BF16)],
        compiler_params=_cparams(("parallel", "arbitrary")),
        name="natten",
    )(p, p, p, pc_arr, pc_arr, pc_arr, tab)


def _scan_rows(x, reverse, op, fill):
    n = x.shape[0]
    idx = lax.broadcasted_iota(jnp.int32, x.shape, 0)
    step = 1
    while step < n:
        if reverse:
            x = op(x, jnp.where(idx < n - step, pltpu.roll(x, n - step, 0), fill))
        else:
            x = op(x, jnp.where(idx >= step, pltpu.roll(x, step, 0), fill))
        step *= 2
    return x


def _log_sigmoid(x):
    return jnp.minimum(x, 0.0) - jnp.log(1.0 + jnp.exp(-jnp.abs(x)))


def _rope_swap(x):
    lane = lax.broadcasted_iota(jnp.int32, x.shape, 1)
    return jnp.where(lane % 64 < 32, pltpu.roll(x, LANE - 32, 1), pltpu.roll(x, 32, 1))


def _chunk_rows(c):
    return pl.ds(pl.multiple_of(c * M_CHUNK, M_CHUNK), M_CHUNK)


def _gate_rows(c):
    return pl.ds(pl.multiple_of(c * 8, 8), 4)


def _mlstm_prepare(seq, bm_ref, rope):
    shape = (M_CHUNK, LANE)

    def body(c, carry):
        rows = _chunk_rows(c)
        gc = seq["g"][rows, :] + bm_ref[...]
        lf = _log_sigmoid(gc)
        xs = []
        for d in range(2):
            reverse = d == 1
            b = _scan_rows(lf, reverse, jnp.add, 0.0)
            a_rep = jnp.broadcast_to(b[:, 2 * d + 1:2 * d + 2], shape)
            x_rep = jnp.broadcast_to(gc[:, 2 * d:2 * d + 1], shape) - a_rep
            seq["a"][d, rows, :] = a_rep
            seq["x"][d, rows, :] = x_rep
            seq["mi"][d, rows, :] = a_rep + _scan_rows(x_rep, reverse, jnp.maximum, -jnp.inf)
            seq["row"][pl.ds(c * 8 + 2 * d + 1, 1), :] = jnp.max(x_rep, axis=0, keepdims=True)
            xs.append(x_rep)
        lane = lax.broadcasted_iota(jnp.int32, shape, 1)
        xt = jnp.where(lane < LANE // 2, xs[0], xs[1]).T
        seq["row"][pl.ds(c * 8, 1), :] = xt[0:1, :]
        seq["row"][pl.ds(c * 8 + 2, 1), :] = xt[LANE // 2:LANE // 2 + 1, :]
        q = seq["q"][rows, :].astype(F32)
        k = seq["k"][rows, :].astype(F32)
        if rope is not None:
            cos = rope[0][rows, :]
            sin = rope[1][rows, :]
            q = q * cos + _rope_swap(q) * sin
            k = k * cos + _rope_swap(k) * sin
        seq["qs"][rows, :] = (q * (M_DK ** -0.5)).astype(BF16)
        seq["ks"][rows, :] = k
        return carry

    lax.fori_loop(0, seq["n"], body, 0)


def _mlstm_step(seq, c, reverse, state, order):
    c_ref, m_ref = state
    d = 1 if reverse else 0
    ln = M_CHUNK
    rows = _chunk_rows(c)
    a = seq["a"][d, rows, :]
    x = seq["x"][d, rows, :]
    gate_rows = seq["row"][_gate_rows(c), :]
    x_row, x_max = gate_rows[2 * d:2 * d + 1, :], gate_rows[2 * d + 1:2 * d + 2, :]
    q = seq["qs"][rows, :]
    k = seq["ks"][rows, :]
    vx = jnp.concatenate([seq["v"][rows, :], jnp.ones((ln, LANE), BF16)], axis=1)
    m_prev = m_ref[d]
    c_prev = c_ref[d]

    inter = a + m_prev
    m_row = jnp.maximum(inter, seq["mi"][d, rows, :])
    s = _dot_nt(q, k.astype(BF16)) * jnp.exp(jnp.where(order, a + x_row, -jnp.inf) - m_row)
    ew = jnp.exp(inter - m_row)
    tot = _dot(s.astype(BF16), vx) + jnp.concatenate([ew] * 3, axis=1) * _dot(q, c_prev.astype(BF16))
    inv = 1.0 / jnp.maximum(jnp.abs(tot[:, M_DV:]), jnp.exp(-m_row))
    seq["hb" if reverse else "hf"][rows, :] = tot[:, :M_DV] * jnp.concatenate([inv] * 2, axis=1)

    b_last = a[0:1, :] if reverse else a[ln - 1:ln, :]
    m_new = jnp.maximum(b_last + m_prev, b_last + x_max)
    decay = jnp.exp(b_last + m_prev - m_new)
    kw = k * jnp.exp(b_last + x - m_new)
    c_ref[d] = jnp.concatenate([decay] * 3, axis=1) * c_prev + _dot(kw.T.astype(BF16), vx)
    m_ref[d] = m_new


def _mlstm_scan(seq, state):
    n = seq["n"]
    t_idx = lax.broadcasted_iota(jnp.int32, (M_CHUNK, M_CHUNK), 0)
    s_idx = lax.broadcasted_iota(jnp.int32, (M_CHUNK, M_CHUNK), 1)

    def body(ci, carry):
        _mlstm_step(seq, ci, False, state, t_idx >= s_idx)
        _mlstm_step(seq, n - 1 - ci, True, state, s_idx >= t_idx)
        return carry

    lax.fori_loop(0, n, body, 0, unroll=min(4, n))


def _mlstm_finish(seq, gn_ref):
    def body(c, carry):
        rows = _chunk_rows(c)
        y = _rms(seq["hf"][rows, :] + seq["hb"][rows, :], gn_ref[...])
        seq["y"][rows, :] = (y * _sigmoid(seq["o"][rows, :].astype(F32))).astype(seq["y"].dtype)
        return carry

    lax.fori_loop(0, seq["n"], body, 0)


_SEQ_IN = ("q", "k", "v", "o", "g")
_SEQ_SCRATCH = ("qs", "ks", "a", "x", "mi", "row", "hf", "hb")


def _mlstm_kernel(*refs):
    lat = dict(zip(_SEQ_IN, refs[0:5]))
    ctx = dict(zip(_SEQ_IN, refs[5:10]))
    bm_ref, gn_ref, cos_ref, sin_ref = refs[10:14]
    lat["y"], ctx["y"] = refs[14:16]
    lat.update(zip(_SEQ_SCRATCH, refs[16:24]))
    ctx.update(zip(_SEQ_SCRATCH, refs[24:32]))
    state = refs[32:34]
    for seq in (lat, ctx):
        seq["n"] = seq["q"].shape[0] // M_CHUNK

    _mlstm_prepare(ctx, bm_ref, None)
    _mlstm_prepare(lat, bm_ref, (cos_ref, sin_ref))
    for ref in state:
        ref[...] = jnp.zeros_like(ref)
    _mlstm_scan(ctx, state)
    _mlstm_scan(lat, state)
    _mlstm_finish(ctx, gn_ref)
    _mlstm_finish(lat, gn_ref)


def _rope_tables(t):
    pos = np.arange(t)
    half = M_DK // 4
    inv = ROPE_THETA ** (-np.arange(half, dtype=np.float64) / half)
    ang_r = (pos // GRID_W)[:, None] * inv[None, :]
    ang_c = (pos % GRID_W)[:, None] * inv[None, :]
    ang = np.concatenate([ang_r, ang_r, ang_c, ang_c], axis=1)
    sign = np.tile(np.concatenate([-np.ones(half), np.ones(half)]), 2)[None, :]
    return jnp.asarray(np.cos(ang), F32), jnp.asarray(np.sin(ang) * sign, F32)


def _seq_scratch(n):
    col = pltpu.VMEM((2, n, LANE), F32)
    return [pltpu.VMEM((n, M_DK), BF16), pltpu.VMEM((n, M_DK), F32), col, col, col,
            pltpu.VMEM((n // M_CHUNK * 8, LANE), F32), pltpu.VMEM((n, M_DV), F32), pltpu.VMEM((n, M_DV), F32)]


def _mlstm(p, pc, g, bm, gn, dims, ctx_len):
    nl, t, b = dims
    pc_arr, pc_row0, pc_col0 = pc
    cb = pc_row0 // ctx_len
    gb = nl // ctx_len
    ctx_o = OFF_CO if pc_arr.shape[1] + pc_col0 >= OFF_CO + M_V_W else OFF_CV
    cos, sin = _rope_tables(t)
    dk, dv = M_DK, M_DV
    lat = lambda off, w: pl.BlockSpec((t, w), lambda bi, h: (bi, off // w + h))
    ctx = lambda off, w: pl.BlockSpec((ctx_len, w), lambda bi, h: (cb + bi, (off - pc_col0) // w + h))
    const = lambda shape: pl.BlockSpec(shape, lambda bi, h: (0, 0))
    return pl.pallas_call(
        _mlstm_kernel,
        grid=(b, M_HEADS),
        in_specs=[lat(OFF_CQ, dk), lat(OFF_CK, dk), lat(OFF_CV, dv), lat(OFF_CO, dv),
                  pl.BlockSpec((t, LANE), lambda bi, h: (bi, h)),
                  ctx(OFF_CQ, dk), ctx(OFF_CK, dk), ctx(OFF_CV, dv), ctx(ctx_o, dv),
                  pl.BlockSpec((ctx_len, LANE), lambda bi, h: (gb + bi, h)),
                  pl.BlockSpec((1, LANE), lambda bi, h: (0, h)),
                  pl.BlockSpec((1, dv), lambda bi, h: (0, h)),
                  const((t, LANE)), const((t, LANE))],
        out_specs=[pl.BlockSpec((t, dv), lambda bi, h: (bi, h)),
                   pl.BlockSpec((ctx_len, dv), lambda bi, h: (bi, h))],
        out_shape=[jax.ShapeDtypeStruct((nl, M_V_W), BF16),
                   jax.ShapeDtypeStruct((b * ctx_len, M_V_W), BF16)],
        scratch_shapes=_seq_scratch(t) + _seq_scratch(ctx_len) + [
            pltpu.VMEM((2, dk, dv + LANE), F32), pltpu.VMEM((2, 1, LANE), F32)],
        compiler_params=_cparams(("parallel", "arbitrary")),
        name="mlstm",
    )(p, p, p, p, g, pc_arr, pc_arr, pc_arr, pc_arr, g, bm, gn, cos, sin)


def _merge_kernel(ya_ref, yb_ref, yc_ref, g0_ref, g1_ref, g2_ref, w_ref, o_ref):
    acc = _sigmoid(g0_ref[...].astype(F32)) * _dot(ya_ref[...], w_ref[0])
    acc = acc + _sigmoid(g1_ref[...].astype(F32)) * _dot(yb_ref[...], w_ref[1])
    acc = acc + _sigmoid(g2_ref[...].astype(F32)) * _dot(yc_ref[...], w_ref[2])
    o_ref[...] = acc.astype(o_ref.dtype)


def _merge(ya, yb, yc, p, w_branch, n_rows, tm, tn):
    d = w_branch.shape[2]
    br = lambda: pl.BlockSpec((tm, BRANCH_W), lambda i, j: (i, 0))
    gate = lambda k: pl.BlockSpec((tm, tn), lambda i, j: (i, (OFF_GATE + k * d) // tn + j))
    return pl.pallas_call(
        _merge_kernel,
        grid=(n_rows // tm, d // tn),
        in_specs=[br(), br(), br(), gate(0), gate(1), gate(2),
                  pl.BlockSpec((N_BRANCH, BRANCH_W, tn), lambda i, j: (0, 0, j),
                               pipeline_mode=pl.Buffered(1) if tn == d else None)],
        out_specs=pl.BlockSpec((tm, tn), lambda i, j: (i, j)),
        out_shape=jax.ShapeDtypeStruct((n_rows, d), BF16),
        compiler_params=_cparams(("parallel", "arbitrary")),
        name="merge",
    )(ya, yb, yc, p, p, p, w_branch)


def _route(scores, sel):
    row = lambda a, e: a[e:e + 1, :]
    best = None
    for gi in range(N_GROUPS):
        a, b, c, d = (row(sel, gi * EXP_PER_GROUP + j) for j in range(EXP_PER_GROUP))
        hi1, lo1 = jnp.maximum(a, b), jnp.minimum(a, b)
        hi2, lo2 = jnp.maximum(c, d), jnp.minimum(c, d)
        gs = jnp.maximum(hi1, hi2) + jnp.maximum(jnp.minimum(hi1, hi2), jnp.maximum(lo1, lo2))
        if best is None:
            best, g_idx = gs, jnp.zeros_like(gs, dtype=jnp.int32)
        else:
            take = gs > best
            best = jnp.where(take, gs, best)
            g_idx = jnp.where(take, gi, g_idx)

    def in_group(a, j):
        out = row(a, j)
        for gi in range(1, N_GROUPS):
            out = jnp.where(g_idx == gi, row(a, gi * EXP_PER_GROUP + j), out)
        return out

    v = [in_group(sel, j) for j in range(EXP_PER_GROUP)]
    u = [in_group(scores, j) for j in range(EXP_PER_GROUP)]

    def argmax_first(vals):
        bv, bi, bu = vals[0], jnp.zeros_like(g_idx), u[0]
        for j in range(1, EXP_PER_GROUP):
            take = vals[j] > bv
            bv = jnp.where(take, vals[j], bv)
            bi = jnp.where(take, j, bi)
            bu = jnp.where(take, u[j], bu)
        return bi, bu

    i1, s1 = argmax_first(v)
    i2, s2 = argmax_first([jnp.where(i1 == j, -jnp.inf, v[j]) for j in range(EXP_PER_GROUP)])
    tot = s1 + s2
    return g_idx * EXP_PER_GROUP + i1, g_idx * EXP_PER_GROUP + i2, s1 / tot, s2 / tot


def _to_tall(ref, val):
    rows = val.shape[0]
    for j in range(ROW_SEG):
        ref[pl.ds(j, rows, stride=ROW_SEG), :] = val[:, j * LANE:(j + 1) * LANE]


def _from_tall(ref):
    rows = ref.shape[0] // ROW_SEG
    return jnp.concatenate([ref[pl.ds(j, rows, stride=ROW_SEG), :] for j in range(ROW_SEG)], axis=1)


def _outproj_kernel(mg_ref, x_ref, gt_ref, sh_ref, sc_ref, g_ref, w_ref, wr_ref, br_ref, xo_ref, h_ref, r_ref):
    y = _dot(mg_ref[...], w_ref[...])
    xn = x_ref[...] + gt_ref[0] * y
    xo_ref[...] = xn
    h = _rms(xn, g_ref[...]) * (1.0 + sc_ref[0]) + sh_ref[0]
    _to_tall(h_ref, h)
    scores = _sigmoid(_dot_nt(wr_ref[...], h.astype(BF16)))
    e1, e2, w1, w2 = _route(scores, scores + br_ref[...])
    r_ref[...] = jnp.zeros_like(r_ref)
    r_ref[0:1, :] = e1.astype(F32)
    r_ref[1:2, :] = e2.astype(F32)
    r_ref[2:3, :] = w1
    r_ref[3:4, :] = w2


def _outproj(merged, x, mod3, g2, w_out, w_rt, b_r, dims, n_rows, tm):
    nl, t, b = dims
    d = x.shape[1]
    grp = functools.partial(_group_of, tm=tm, nl=nl, t=t, b=b)
    mrow = lambda k: pl.BlockSpec((1, 1, d), lambda i: (grp(i) * 6 + k, 0, 0))
    return pl.pallas_call(
        _outproj_kernel,
        grid=(n_rows // tm,),
        in_specs=[pl.BlockSpec((tm, d), lambda i: (i, 0)),
                  pl.BlockSpec((tm, d), lambda i: (i, 0)),
                  mrow(2), mrow(3), mrow(4),
                  pl.BlockSpec((1, d), lambda i: (0, 0)),
                  pl.BlockSpec((d, d), lambda i: (0, 0), pipeline_mode=pl.Buffered(1)),
                  pl.BlockSpec((N_EXPERTS, d), lambda i: (0, 0)),
                  pl.BlockSpec((N_EXPERTS, 1), lambda i: (0, 0))],
        out_specs=[pl.BlockSpec((tm, d), lambda i: (i, 0)),
                   pl.BlockSpec((tm * ROW_SEG, LANE), lambda i: (i, 0)),
                   pl.BlockSpec((8, tm), lambda i: (0, i))],
        out_shape=[jax.ShapeDtypeStruct((n_rows, d), F32),
                   jax.ShapeDtypeStruct((n_rows * ROW_SEG, LANE), F32),
                   jax.ShapeDtypeStruct((8, n_rows), F32)],
        compiler_params=_cparams(("parallel",)),
        name="outproj_route",
    )(merged, x, mod3, mod3, mod3, g2, w_out, w_rt, b_r)


def _slot_plan(route, tm, max_tiles):
    n = route.shape[1]
    e = route[0:2, :].astype(jnp.int32).reshape(-1)
    oh = (e[:, None] == jnp.arange(N_EXPERTS, dtype=jnp.int32)[None, :]).astype(jnp.int32)
    cs = jnp.cumsum(oh, axis=0)
    rank = jnp.sum(cs * oh, axis=1) - 1
    counts = cs[-1]
    padded = ((counts + tm - 1) // tm) * tm
    ends = jnp.cumsum(padded)
    slot = jnp.sum(oh * (ends - padded)[None, :], axis=1) + rank
    tile_ends = ends // tm
    n_used = tile_ends[-1]
    tiles = jnp.arange(max_tiles, dtype=jnp.int32)
    src = jnp.minimum(tiles, n_used - 1)
    tile_e = jnp.minimum(jnp.sum((src[:, None] >= tile_ends[None, :]).astype(jnp.int32), axis=1), N_EXPERTS - 1)
    left = jnp.take(counts, tile_e) - (tiles - jnp.take(tile_ends - padded // tm, tile_e)) * tm
    n_valid = jnp.where(tiles < n_used, jnp.clip(left, 0, tm), 0)
    i32 = lambda a: a.astype(jnp.int32)
    n_slots = max_tiles * tm
    pad_from = i32(jnp.concatenate([ends - padded + counts, ends[-1:]]))
    pad_to = i32(jnp.concatenate([ends, jnp.full((1,), n_slots, ends.dtype)]))
    tok_of_slot, dst_of_slot = _invert_slots(i32(slot), pad_from, pad_to, n, n_slots)
    wts = route[2:4, :].T
    return tok_of_slot, dst_of_slot, i32(tile_e), i32(n_valid), i32(n_used.reshape(1)), wts


def _invert_kernel(groups_ref, slot_ref, pad_from_ref, pad_to_ref, tok_ref, dst_ref, *, n):
    def clear(s, carry):
        tok_ref[s] = 0
        dst_ref[s] = 0
        return carry

    def clear_range(r, carry):
        lax.fori_loop(pad_from_ref[r], pad_to_ref[r], clear, 0)
        return carry

    def put(choice):
        def body(g, carry):
            for u in range(DMA_UNROLL):
                f = g * DMA_UNROLL + u
                s = slot_ref[f]
                tok_ref[s] = f - choice * n
                dst_ref[s] = f
            return carry
        return body

    lax.fori_loop(0, pad_from_ref.shape[0], clear_range, 0)
    lax.fori_loop(0, groups_ref[0], put(0), 0)
    lax.fori_loop(groups_ref[0], groups_ref[1], put(1), 0)


def _invert_slots(slot, pad_from, pad_to, n, n_slots):
    assert n % DMA_UNROLL == 0
    smem = pl.BlockSpec(memory_space=pltpu.SMEM)
    out = jax.ShapeDtypeStruct((n_slots,), jnp.int32)
    groups = jnp.array([n // DMA_UNROLL, 2 * n // DMA_UNROLL], jnp.int32)
    return pl.pallas_call(
        functools.partial(_invert_kernel, n=n),
        in_specs=[smem] * 4, out_specs=[smem, smem], out_shape=[out, out],
        name="moe_invert_slots",
    )(groups, slot, pad_from, pad_to)


def _row_copy(src_ref, src_row, dst_ref, dst_row, sem):
    src = src_ref.at[pl.ds(pl.multiple_of(src_row * ROW_SEG, ROW_SEG), ROW_SEG), :]
    dst = dst_ref.at[pl.ds(pl.multiple_of(dst_row * ROW_SEG, ROW_SEG), ROW_SEG), :]
    return pltpu.make_async_copy(src, dst, sem)


def _expert_kernel(tok_ref, dst_ref, te_ref, nv_ref, nu_ref, h_ref, wg_ref, wu_ref, wd_ref, y_ref,
                   xbuf, ybuf, gsem, ssem, *, tm):
    del te_ref
    i = pl.program_id(0)
    n_used = nu_ref[0]
    used = i < n_used
    cur = i % 2

    def for_rows(n_rows, body):
        def group(g, carry):
            for u in range(DMA_UNROLL):
                body(g * DMA_UNROLL + u)
            return carry

        def single(j, carry):
            body(j)
            return carry

        n_groups = n_rows // DMA_UNROLL
        lax.fori_loop(0, n_groups, group, 0)
        if not isinstance(n_rows, int) or n_rows % DMA_UNROLL:
            lax.fori_loop(n_groups * DMA_UNROLL, n_rows, single, 0)

    def wait_rows(buf_ref, n_rows, sem):
        part = buf_ref.at[pl.ds(0, n_rows * ROW_SEG), :]
        pltpu.make_async_copy(part, part, sem).wait()

    def gather(tile, buf, wait):
        if wait:
            wait_rows(xbuf.at[buf], tm, gsem.at[buf])
        else:
            for_rows(tm, lambda j: _row_copy(h_ref, tok_ref[tile * tm + j], xbuf.at[buf], j,
                                             gsem.at[buf]).start(priority=1))

    def scatter(tile, buf, wait):
        if wait:
            wait_rows(ybuf.at[buf], nv_ref[tile], ssem.at[buf])
        else:
            for_rows(nv_ref[tile], lambda j: _row_copy(ybuf.at[buf], j, y_ref, dst_ref[tile * tm + j],
                                                       ssem.at[buf]).start())

    @pl.when(i == 0)
    def _():
        gather(0, 0, False)

    @pl.when(i + 1 < n_used)
    def _():
        gather(i + 1, 1 - cur, False)

    @pl.when(used)
    def _():
        gather(i, cur, True)
        x = _from_tall(xbuf.at[cur]).astype(BF16)
        a = _dot(x, wg_ref[0, 0])
        he = a * _sigmoid(a) * _dot(x, wu_ref[0, 0])
        _to_tall(ybuf.at[cur], _dot(he.astype(BF16), wd_ref[0, 0]))

    @pl.when(jnp.logical_and(i >= 1, i - 1 < n_used))
    def _():
        scatter(i - 1, 1 - cur, True)

    @pl.when(used)
    def _():
        scatter(i, cur, False)

    @pl.when(jnp.logical_and(i == pl.num_programs(0) - 1, used))
    def _():
        scatter(i, cur, True)


def _experts(h, tok_of_slot, dst_of_slot, tile_e, n_valid, n_used, experts, layer, tm):
    wg, wu, wd = experts
    d, f = wg.shape[2:]
    wspec = lambda shape: pl.BlockSpec((1,) + shape, lambda i, t, s, e, v, u: (layer, e[i], 0, 0))
    buf = pltpu.VMEM((2, tm * ROW_SEG, LANE), F32)
    return pl.pallas_call(
        functools.partial(_expert_kernel, tm=tm),
        grid_spec=pltpu.PrefetchScalarGridSpec(
            num_scalar_prefetch=5,
            grid=(tile_e.shape[0],),
            in_specs=[pl.BlockSpec(memory_space=pl.ANY), wspec((1, d, f)), wspec((1, d, f)), wspec((1, f, d))],
            out_specs=pl.BlockSpec(memory_space=pl.ANY),
            scratch_shapes=[buf, buf, pltpu.SemaphoreType.DMA((2,)), pltpu.SemaphoreType.DMA((2,))]),
        out_shape=jax.ShapeDtypeStruct((2 * h.shape[0], LANE), F32),
        compiler_params=_cparams(("arbitrary",)),
        name="moe_experts",
    )(tok_of_slot, dst_of_slot, tile_e, n_valid, n_used, h, wg, wu, wd)


def _combine_kernel(x_ref, w_ref, gt_ref, gf_ref, y1_ref, y2_ref, o_ref, *, final):
    w = w_ref[...]
    f = w[:, 0:1] * _from_tall(y1_ref) + w[:, 1:2] * _from_tall(y2_ref)
    xo = x_ref[...] + gt_ref[0] * f
    if final:
        xo = _rms(xo, gf_ref[...])
    o_ref[...] = xo


def _combine(x, wts, mod3, g_final, y, dims, n_tok, tm, final):
    nl, t, b = dims
    d = x.shape[1]
    grp = functools.partial(_group_of, tm=tm, nl=nl, t=t, b=b)
    return pl.pallas_call(
        functools.partial(_combine_kernel, final=final),
        grid=(n_tok // tm,),
        in_specs=[pl.BlockSpec((tm, d), lambda i: (i, 0)),
                  pl.BlockSpec((tm, 2), lambda i: (i, 0)),
                  pl.BlockSpec((1, 1, d), lambda i: (grp(i) * 6 + 5, 0, 0)),
                  pl.BlockSpec((1, d), lambda i: (0, 0)),
                  pl.BlockSpec((tm * ROW_SEG, LANE), lambda i: (i, 0)),
                  pl.BlockSpec((tm * ROW_SEG, LANE), lambda i: (n_tok // tm + i, 0))],
        out_specs=pl.BlockSpec((tm, d), lambda i: (i, 0)),
        out_shape=jax.ShapeDtypeStruct((n_tok, d), F32),
        compiler_params=_cparams(("parallel",)),
        name="moe_combine",
    )(x, wts, mod3, g_final, y, y)


def _moe(h2, route, xn, mod3, g_final, experts, layer, dims, final):
    n_tok = xn.shape[0]
    max_tiles = (2 * n_tok) // MOE_TM + N_EXPERTS
    tok_of_slot, dst_of_slot, tile_e, n_valid, n_used, wts = _slot_plan(route, MOE_TM, max_tiles)
    y = _experts(h2, tok_of_slot, dst_of_slot, tile_e, n_valid, n_used, experts, layer, MOE_TM)
    nl, t, _ = dims
    tm = 2 * ROW_TM if t % (2 * ROW_TM) == 0 and (n_tok - nl) % (2 * ROW_TM) == 0 else ROW_TM
    return _combine(xn, wts, mod3, g_final, y, dims, n_tok, tm, final)


def _gate_weight(w_cg):
    d = w_cg.shape[0]
    per_head = w_cg.reshape(d, 4, M_HEADS).transpose(0, 2, 1)
    return jnp.pad(per_head, ((0, 0), (0, 0), (0, LANE - 4))).reshape(d, M_HEADS * LANE)


def _layer(x, mod3, dims, ctx_len, last, g1, g2, w_in, g_sgu, w_sp, b_sp, rpb, b_mgate, g_mnorm, w_branch, w_out,
           w_rt, b_r, experts, layer, g_final):
    nl, t, b = dims
    nt, d = x.shape
    fits = lambda rows: t % rows == 0 and (nt - nl) % rows == 0
    tm = 512 if fits(512) else 256
    tm_proj = 1024 if fits(1024) else tm

    w_main = jnp.concatenate([w_in[:, :OFF_CG], w_in[:, OFF_CG + M_NGATE:]], axis=1).astype(BF16)
    w_gate = _gate_weight(w_in[:, OFF_CG:OFF_CG + M_NGATE]).astype(BF16)
    g = _inproj(x, mod3, g1, w_gate, dims, tm_proj, M_HEADS * LANE, F32)
    if last:
        p = _inproj(x, mod3, g1, w_main, dims, tm_proj, 1024, BF16, 0, nl)
        pc = (_inproj(x, mod3, g1, w_main[:, OFF_BK:OFF_CO], dims, tm_proj, 1024, BF16, nl, nt - nl), 0, OFF_BK)
    else:
        p = _inproj(x, mod3, g1, w_main, dims, tm_proj, 1024, BF16)
        pc = (p, nl, 0)

    n_rows = nl if last else nt
    b_full = jnp.repeat(b_sp.T, A_GDIM, axis=1)
    ya = _sgu(p, g_sgu[None, :], w_sp.astype(BF16), b_full, n_rows, tm)

    yb_l, yb_c = _natten(p, pc, _na_bias_table(rpb, t // GRID_W), dims, ctx_len)

    bm = jnp.pad(b_mgate.T, ((0, 0), (0, LANE - 4))).reshape(1, M_HEADS * LANE)
    yc_l, yc_c = _mlstm(p, pc, g, bm, g_mnorm[None, :], dims, ctx_len)

    if last:
        yb, yc = yb_l, yc_l
    else:
        yb = jnp.concatenate([yb_l, yb_c], axis=0)
        yc = jnp.concatenate([yc_l, yc_c], axis=0)

    merged = _merge(ya, yb, yc, p, w_branch.astype(BF16), n_rows, tm, d)
    xn, h2, route = _outproj(merged, x, mod3, g2[None, :], w_out.astype(BF16), w_rt, b_r, dims, n_rows, tm)
    return _moe(h2, route, xn, mod3, g_final, experts, layer, dims, last)


def kernel(x, c, ctx, c_ctx, w_ada, b_ada, g_norm1, g_norm2, w_in, g_sgu, w_spatial, b_spatial, na_rpb, b_mgate,
           g_mnorm, w_branch, w_out, w_router, b_router, w_e_gate, w_e_up, w_e_down, g_final):
    b, t, d = x.shape
    ctx_len = ctx.shape[1]
    nl = b * t
    dims = (nl, t, b)
    depth = w_ada.shape[0]
    xs = jnp.concatenate([x.reshape(nl, d), ctx.reshape(b * ctx_len, d)], axis=0)
    n_mod = -(-(b + 1) // 8) * 8
    cc = jnp.zeros((n_mod, d), F32).at[:b].set(c).at[b].set(c_ctx)
    w_rt = w_router.T.astype(BF16)
    b_r = b_router[:, None].astype(F32)
    experts = tuple(w.astype(BF16) for w in (w_e_gate, w_e_up, w_e_down))
    for layer in range(depth):
        mod = _ada(cc, w_ada, layer, b_ada[layer][None, :])
        mod3 = mod.reshape(n_mod * 6, 1, d)
        xs = _layer(xs, mod3, dims, ctx_len, layer == depth - 1, g_norm1[layer][None, :], g_norm2[layer],
                    w_in[layer], g_sgu[layer], w_spatial[layer], b_spatial[layer], na_rpb[layer], b_mgate[layer],
                    g_mnorm[layer], w_branch[layer], w_out[layer], w_rt, b_r, experts, layer, g_final[None, :])
    return xs.reshape(b, t, d)
```
